```python
import math
import jax, jax.numpy as jnp
from jax import lax
import numpy as np

D_MODEL = 2048
BATCH = 4
SEQ = 4096
DEPTH = 2

BLOCK = 128
EPS = 1e-6
N_HEADS_MLA = 8
Q_LORA = 512
KV_LORA = 512
QK_NOPE = 128
QK_ROPE = 64
V_DIM = 128
ROPE_THETA = 10000.0
N_HEADS_DIL = 8
HEAD_DIM = 128
DIL_PATTERNS = ((128, 1), (512, 4), (2048, 16))
N_BUCKETS = 32
BUCKET_MAX_DIST = 2048
N_HEADS_SB = 16

MIX_A = N_HEADS_MLA * V_DIM
MIX_B = N_HEADS_DIL * HEAD_DIM
MIX_EVEN = MIX_A + MIX_B
MIX_ODD = N_HEADS_SB * HEAD_DIM
EVEN_SPLITS = (Q_LORA, KV_LORA, QK_ROPE, MIX_B, MIX_B, MIX_B, MIX_EVEN)
ODD_SPLITS = (MIX_ODD, MIX_ODD, MIX_ODD, MIX_ODD)
IN_EVEN = sum(EVEN_SPLITS)
IN_ODD = sum(ODD_SPLITS)
N_EVEN = (DEPTH + 1) // 2
N_ODD = DEPTH // 2

kernel_name = "hybrid_mla_dilated_stickbreaking"


def rms_norm(x, gain):
    x32 = x.astype(jnp.float32)
    y = x32 * lax.rsqrt(jnp.mean(x32 * x32, axis=-1, keepdims=True) + EPS)
    return y.astype(x.dtype) * gain


def split_cols(t, widths):
    offs = np.cumsum(widths)[:-1].tolist()
    return jnp.split(t, offs, axis=-1)


def to_heads(t, n):
    b, s, _ = t.shape
    return t.reshape(b, s, n, -1).transpose(0, 2, 1, 3)


def from_heads(t):
    b, n, s, d = t.shape
    return t.transpose(0, 2, 1, 3).reshape(b, s, n * d)


def rope(x, pos):
    half = QK_ROPE // 2
    inv = 1.0 / (ROPE_THETA ** (jnp.arange(half, dtype=jnp.float32) / half))
    ang = pos.astype(jnp.float32)[:, None] * inv[None, :]
    cos = jnp.cos(ang).astype(x.dtype)
    sin = jnp.sin(ang).astype(x.dtype)
    x1, x2 = x[..., :half], x[..., half:]
    return jnp.concatenate([x1 * cos - x2 * sin, x1 * sin + x2 * cos], axis=-1)


def t5_bucket(dist):
    max_exact = N_BUCKETS // 2
    d = jnp.maximum(dist.astype(jnp.float32), 1.0)
    large = max_exact + (jnp.log(d / max_exact) / math.log(BUCKET_MAX_DIST / max_exact)
                         * (N_BUCKETS - max_exact)).astype(jnp.int32)
    large = jnp.minimum(large, N_BUCKETS - 1)
    return jnp.where(dist < max_exact, dist, large)


def block_queries(q):
    b, h, s, d = q.shape
    return q.reshape(b, h, s // BLOCK, BLOCK, d).transpose(2, 0, 1, 3, 4)


def unblock(o):
    nb, b, h, blk, d = o.shape
    return o.transpose(1, 2, 0, 3, 4).reshape(b, h, nb * blk, d)


def causal_softmax_attention(q, k, v):
    s_len = q.shape[2]
    scale = q.shape[-1] ** -0.5
    kpos = jnp.arange(s_len)

    def one_block(args):
        qi, i = args
        s = jnp.einsum('bhqd,bhkd->bhqk', qi, k).astype(jnp.float32) * scale
        qpos = i * BLOCK + jnp.arange(BLOCK)
        s = jnp.where(kpos[None, :] <= qpos[:, None], s, -jnp.inf)
        p = jax.nn.softmax(s, axis=-1).astype(v.dtype)
        return jnp.einsum('bhqk,bhkd->bhqd', p, v)

    nb = s_len // BLOCK
    return unblock(lax.map(one_block, (block_queries(q), jnp.arange(nb))))


def dilated_partial(q, k, v, rel_bias, window, dilation):
    b, h, s_len, d = q.shape
    sub_len = s_len // dilation
    nb = -(-sub_len // BLOCK)
    padded = nb * BLOCK
    span = window // dilation

    def to_blocks(t):
        t = t.reshape(b, h, sub_len, dilation, d).transpose(0, 1, 3, 2, 4)
        t = jnp.pad(t, ((0, 0), (0, 0), (0, 0), (0, padded - sub_len), (0, 0)))
        return t.reshape(b, h, dilation, nb, BLOCK, d)

    def with_prev(t):
        prev = jnp.pad(t[:, :, :, :-1], ((0, 0), (0, 0), (0, 0), (1, 0), (0, 0), (0, 0)))
        return jnp.concatenate([prev, t], axis=4)

    qb = to_blocks(q)
    kb = with_prev(to_blocks(k))
    vb = with_prev(to_blocks(v))

    qi = jnp.arange(BLOCK)[:, None]
    kj = jnp.arange(2 * BLOCK)[None, :]
    rel = BLOCK + qi - kj
    blk = jnp.arange(nb)[:, None, None]
    valid = (rel >= 0) & (rel <= span) & (blk * BLOCK - BLOCK + kj >= 0)
    bias = rel_bias[t5_bucket(jnp.maximum(rel, 0) * dilation)]
    bias = bias.transpose(2, 0, 1).astype(jnp.float32)[:, None, None]

    s = jnp.einsum('bhrnqd,bhrnkd->bhrnqk', qb, kb).astype(jnp.float32) * (d ** -0.5)
    s = jnp.where(valid, s + bias, -jnp.inf)
    mx = jnp.max(s, axis=-1)
    p = jnp.exp(s - mx[..., None])
    den = jnp.sum(p, axis=-1)
    num = jnp.einsum('bhrnqk,bhrnkd->bhrnqd', p, vb.astype(jnp.float32))

    def from_blocks(t):
        t = t.reshape(b, h, dilation, padded, *t.shape[5:])[:, :, :, :sub_len]
        t = jnp.swapaxes(t, 2, 3)
        return t.reshape(b, h, s_len, *t.shape[4:])

    return from_blocks(num), from_blocks(den), from_blocks(mx)


def dilated_attention(q, k, v, rel_bias):
    parts = [dilated_partial(q, k, v, rel_bias, w, dl) for (w, dl) in DIL_PATTERNS]
    mx = parts[0][2]
    for part in parts[1:]:
        mx = jnp.maximum(mx, part[2])
    w0 = jnp.exp(parts[0][2] - mx)
    num = parts[0][0] * w0[..., None]
    den = parts[0][1] * w0
    for part in parts[1:]:
        wi = jnp.exp(part[2] - mx)
        num = num + part[0] * wi[..., None]
        den = den + part[1] * wi
    return (num / den[..., None]).astype(q.dtype)


def stick_breaking_attention(q, k, v):
    s_len = q.shape[2]
    scale = q.shape[-1] ** -0.5
    kpos = jnp.arange(s_len)

    def one_block(args):
        qi, i = args
        z = jnp.einsum('bhqd,bhkd->bhqk', qi, k).astype(jnp.float32) * scale
        qpos = i * BLOCK + jnp.arange(BLOCK)
        before = kpos[None, :] < qpos[:, None]
        log_beta = jax.nn.log_sigmoid(z)
        log_rest = jnp.where(before, jax.nn.log_sigmoid(-z), 0.0)
        between = lax.cumsum(log_rest, axis=3, reverse=True) - log_rest
        a = jnp.where(before, jnp.exp(log_beta + between), 0.0).astype(v.dtype)
        return jnp.einsum('bhqk,bhkd->bhqd', a, v)

    nb = s_len // BLOCK
    return unblock(lax.map(one_block, (block_queries(q), jnp.arange(nb))))


def even_mixer(h, w_in, q_norm_gain, kv_norm_gain, w_uq, w_ukv, w_out, rel_bias, pos):
    b, s_len, _ = h.shape
    proj = h @ w_in
    c_q, c_kv, k_rope, q_b, k_b, v_b, gate = split_cols(proj, EVEN_SPLITS)
    q_a = to_heads(rms_norm(c_q, q_norm_gain) @ w_uq, N_HEADS_MLA)
    q_nope, q_rot = q_a[..., :QK_NOPE], q_a[..., QK_NOPE:]
    kv_a = to_heads(rms_norm(c_kv, kv_norm_gain) @ w_ukv, N_HEADS_MLA)
    k_nope, v_a = kv_a[..., :QK_NOPE], kv_a[..., QK_NOPE:]
    k_rot = rope(k_rope, pos)[:, None]
    q_full = jnp.concatenate([q_nope, rope(q_rot, pos)], axis=-1)
    k_full = jnp.concatenate([k_nope, jnp.broadcast_to(k_rot, (b, N_HEADS_MLA, s_len, QK_ROPE))], axis=-1)
    o_a = from_heads(causal_softmax_attention(q_full, k_full, v_a))
    o_b = from_heads(dilated_attention(to_heads(q_b, N_HEADS_DIL), to_heads(k_b, N_HEADS_DIL),
                                       to_heads(v_b, N_HEADS_DIL), rel_bias))
    mix = jnp.concatenate([o_a, o_b], axis=-1) * jax.nn.silu(gate)
    return mix @ w_out


def odd_mixer(h, w_in, w_out):
    proj = h @ w_in
    q, k, v, gate = split_cols(proj, ODD_SPLITS)
    o = stick_breaking_attention(to_heads(q, N_HEADS_SB), to_heads(k, N_HEADS_SB), to_heads(v, N_HEADS_SB))
    return (from_heads(o) * jax.nn.silu(gate)) @ w_out


def setup_inputs(seed: int = 0) -> dict:
    key = jax.random.key(seed)
    ks = jax.random.split(key, 12)
    f32 = jnp.float32
    return {
        "x": jax.random.normal(ks[0], (BATCH, SEQ, D_MODEL), f32),
        "norm_gain": 1.0 + 0.02 * jax.random.normal(ks[1], (DEPTH, D_MODEL), f32),
        "w_in_even": jax.random.normal(ks[2], (N_EVEN, D_MODEL, IN_EVEN), f32) * D_MODEL ** -0.5,
        "q_norm_gain": 1.0 + 0.02 * jax.random.normal(ks[3], (N_EVEN, Q_LORA), f32),
        "kv_norm_gain": 1.0 + 0.02 * jax.random.normal(ks[4], (N_EVEN, KV_LORA), f32),
        "w_uq": jax.random.normal(ks[5], (N_EVEN, Q_LORA, N_HEADS_MLA * (QK_NOPE + QK_ROPE)), f32) * Q_LORA ** -0.5,
        "w_ukv": jax.random.normal(ks[6], (N_EVEN, KV_LORA, N_HEADS_MLA * (QK_NOPE + V_DIM)), f32) * KV_LORA ** -0.5,
        "w_out_even": jax.random.normal(ks[7], (N_EVEN, MIX_EVEN, D_MODEL), f32) * MIX_EVEN ** -0.5,
        "rel_bias": 0.5 * jax.random.normal(ks[8], (N_BUCKETS, N_HEADS_DIL), f32),
        "w_in_odd": jax.random.normal(ks[9], (N_ODD, D_MODEL, IN_ODD), f32) * D_MODEL ** -0.5,
        "w_out_odd": jax.random.normal(ks[10], (N_ODD, MIX_ODD, D_MODEL), f32) * MIX_ODD ** -0.5,
        "final_norm_gain": 1.0 + 0.02 * jax.random.normal(ks[11], (D_MODEL,), f32),
    }


def reference(x, norm_gain, w_in_even, q_norm_gain, kv_norm_gain, w_uq, w_ukv, w_out_even,
              rel_bias, w_in_odd, w_out_odd, final_norm_gain):
    pos = jnp.arange(x.shape[1])
    for layer in range(DEPTH):
        h = rms_norm(x, norm_gain[layer])
        j = layer // 2
        if layer % 2 == 0:
            x = x + even_mixer(h, w_in_even[j], q_norm_gain[j], kv_norm_gain[j], w_uq[j], w_ukv[j],
                               w_out_even[j], rel_bias, pos)
        else:
            x = x + odd_mixer(h, w_in_odd[j], w_out_odd[j])
    return rms_norm(x, final_norm_gain)
```

```python
import functools
import math

import numpy as np
import jax
import jax.numpy as jnp
from jax import lax
from jax.experimental import pallas as pl
from jax.experimental.pallas import tpu as pltpu

F32 = jnp.float32
BF16 = jnp.bfloat16

EPS = 1e-6
N_HEADS_MLA = 8
Q_LORA = 512
KV_LORA = 512
QK_NOPE = 128
QK_ROPE = 64
V_DIM = 128
ROPE_THETA = 10000.0
N_HEADS_DIL = 8
HEAD_DIM = 128
DIL_PATTERNS = ((128, 1), (512, 4), (2048, 16))
N_BUCKETS = 32
BUCKET_MAX_DIST = 2048
N_HEADS_SB = 16
BLOCK = 128
MASK_VALUE = -1e30

LANES = 128
VMEM_LIMIT = 48 * 1024 * 1024

MIX_A = N_HEADS_MLA * V_DIM
MIX_B = N_HEADS_DIL * HEAD_DIM
MIX_EVEN = MIX_A + MIX_B
MIX_ODD = N_HEADS_SB * HEAD_DIM
E_GATE = 0
E_QB = E_GATE + MIX_EVEN
E_KB = E_QB + MIX_B
E_VB = E_KB + MIX_B
E_CQ = E_VB + MIX_B
E_CKV = E_CQ + Q_LORA
E_KR = E_CKV + KV_LORA
E_END = E_KR + 2 * QK_ROPE
PROJ_TN = 1280
E_WIDTH = -(-E_END // PROJ_TN) * PROJ_TN
QK_PAD = 2 * LANES


def _cparams(sem):
    return pltpu.CompilerParams(dimension_semantics=sem, vmem_limit_bytes=VMEM_LIMIT)


def _bucket_tables():
    qi = np.arange(BLOCK)[:, None]
    kj = np.arange(2 * BLOCK)[None, :]
    rel = BLOCK + qi - kj
    max_exact = N_BUCKETS // 2
    tabs = []
    for window, dil in DIL_PATTERNS:
        span = window // dil
        dist = np.maximum(rel, 0) * dil
        d = np.maximum(dist.astype(np.float64), 1.0)
        frac = np.log(d / max_exact) / math.log(BUCKET_MAX_DIST / max_exact) * (N_BUCKETS - max_exact)
        large = np.minimum(max_exact + np.trunc(frac).astype(np.int64), N_BUCKETS - 1)
        bucket = np.where(dist < max_exact, dist, large)
        valid = (rel >= 0) & (rel <= span)
        tabs.append(np.where(valid, bucket, -1).astype(np.int32))
    return np.stack(tabs)


def _bias_tiles_kernel(bucket_ref, rel_bias_ref, o_ref):
    h = pl.program_id(1)
    bucket = bucket_ref[0]
    out = jnp.full(bucket.shape, MASK_VALUE, F32)
    for i in range(N_BUCKETS):
        out = jnp.where(bucket == i, rel_bias_ref[i, h], out)
    o_ref[0, 0] = out


def _bias_tiles(rel_bias):
    n_pat = len(DIL_PATTERNS)
    buckets = jnp.asarray(_bucket_tables())
    return pl.pallas_call(
        _bias_tiles_kernel,
        out_shape=jax.ShapeDtypeStruct((n_pat, N_HEADS_DIL, BLOCK, 2 * BLOCK), F32),
        grid=(n_pat, N_HEADS_DIL),
        in_specs=[pl.BlockSpec((1, BLOCK, 2 * BLOCK), lambda p, h: (p, 0, 0)),
                  pl.BlockSpec(memory_space=pltpu.SMEM)],
        out_specs=pl.BlockSpec((1, 1, BLOCK, 2 * BLOCK), lambda p, h: (p, h, 0, 0)),
        compiler_params=_cparams(("arbitrary", "arbitrary")),
        name="bias_tiles",
    )(buckets, rel_bias)


def _rms_proj_kernel(x_ref, g_ref, w_ref, cs_ref, o_ref, h_ref):
    @pl.when(pl.program_id(1) == 0)
    def _():
        x = x_ref[...]
        ms = jnp.mean(x * x, axis=-1, keepdims=True)
        h_ref[...] = (x * lax.rsqrt(ms + EPS) * g_ref[...]).astype(BF16)

    acc = jnp.dot(h_ref[...], w_ref[...], preferred_element_type=F32)
    o_ref[...] = (acc * cs_ref[...]).astype(o_ref.dtype)


def _rms_proj(x2d, gain, w, col_scale, tm, tn):
    t, d = x2d.shape
    n = w.shape[1]
    return pl.pallas_call(
        _rms_proj_kernel,
        out_shape=jax.ShapeDtypeStruct((t, n), BF16),
        grid=(t // tm, n // tn),
        in_specs=[pl.BlockSpec((tm, d), lambda i, j: (i, 0)),
                  pl.BlockSpec((1, d), lambda i, j: (0, 0)),
                  pl.BlockSpec((d, tn), lambda i, j: (0, j)),
                  pl.BlockSpec((1, tn), lambda i, j: (0, j))],
        out_specs=pl.BlockSpec((tm, tn), lambda i, j: (i, j)),
        scratch_shapes=[pltpu.VMEM((tm, d), BF16)],
        compiler_params=_cparams(("parallel", "arbitrary")),
        name="rms_proj",
    )(x2d, gain, w, col_scale)


def _rope_chunk(chunk, cs):
    r = chunk * cs
    r = r + pltpu.roll(r, QK_ROPE, axis=1)
    lane = lax.broadcasted_iota(jnp.int32, r.shape, 1)
    return jnp.where(lane < QK_ROPE, r, 0.0)


def _mla_prep_kernel(cq_ref, ckv_ref, kr_ref, cs_ref, gq_ref, gkv_ref, wuq_ref, wukv_ref,
                     q_ref, k_ref, v_ref):
    def latent_norm(c_ref, g_ref):
        c = c_ref[...].astype(F32)
        ms = jnp.mean(c * c, axis=-1, keepdims=True)
        return (c * lax.rsqrt(ms + EPS) * g_ref[...]).astype(BF16)

    cs = cs_ref[...]
    scale = (QK_NOPE + QK_ROPE) ** -0.5
    q = jnp.dot(latent_norm(cq_ref, gq_ref), wuq_ref[...], preferred_element_type=F32)
    for h in range(N_HEADS_MLA):
        lo = h * QK_PAD
        q_ref[:, lo:lo + LANES] = (q[:, lo:lo + LANES] * scale).astype(BF16)
        q_ref[:, lo + LANES:lo + QK_PAD] = (_rope_chunk(q[:, lo + LANES:lo + QK_PAD], cs) * scale).astype(BF16)

    k_rot = _rope_chunk(kr_ref[...].astype(F32), cs).astype(BF16)
    kv = jnp.dot(latent_norm(ckv_ref, gkv_ref), wukv_ref[...], preferred_element_type=F32)
    for h in range(N_HEADS_MLA):
        lo = h * (QK_NOPE + V_DIM)
        k_ref[:, h * QK_PAD:h * QK_PAD + LANES] = kv[:, lo:lo + QK_NOPE].astype(BF16)
        k_ref[:, h * QK_PAD + LANES:(h + 1) * QK_PAD] = k_rot
        v_ref[:, h * V_DIM:(h + 1) * V_DIM] = kv[:, lo + QK_NOPE:lo + QK_NOPE + V_DIM].astype(BF16)


def _mla_prep(proj, cos_sin, gq, gkv, wuq, wukv, seq, tm):
    t = proj.shape[0]
    n_seq_tiles = seq // tm
    const = lambda i: (0, 0)
    return pl.pallas_call(
        _mla_prep_kernel,
        out_shape=(jax.ShapeDtypeStruct((t, N_HEADS_MLA * QK_PAD), BF16),
                   jax.ShapeDtypeStruct((t, N_HEADS_MLA * QK_PAD), BF16),
                   jax.ShapeDtypeStruct((t, MIX_A), BF16)),
        grid=(t // tm,),
        in_specs=[pl.BlockSpec((tm, Q_LORA), lambda i: (i, E_CQ // Q_LORA)),
                  pl.BlockSpec((tm, KV_LORA), lambda i: (i, E_CKV // KV_LORA)),
                  pl.BlockSpec((tm, LANES), lambda i: (i, E_KR // LANES)),
                  pl.BlockSpec((tm, LANES), lambda i: (i % n_seq_tiles, 0)),
                  pl.BlockSpec((1, Q_LORA), const),
                  pl.BlockSpec((1, KV_LORA), const),
                  pl.BlockSpec(wuq.shape, const),
                  pl.BlockSpec(wukv.shape, const)],
        out_specs=(pl.BlockSpec((tm, N_HEADS_MLA * QK_PAD), lambda i: (i, 0)),
                   pl.BlockSpec((tm, N_HEADS_MLA * QK_PAD), lambda i: (i, 0)),
                   pl.BlockSpec((tm, MIX_A), lambda i: (i, 0))),
        compiler_params=_cparams(("parallel",)),
        name="mla_prep",
    )(proj, proj, proj, cos_sin, gq, gkv, wuq, wukv)


def _mla_attn_kernel(q_ref, k_ref, v_ref, o_ref, m_ref, l_ref, acc_ref, *, tq):
    i = pl.program_id(2)
    q = q_ref[...]
    m_ref[...] = jnp.full(m_ref.shape, -jnp.inf, F32)
    l_ref[...] = jnp.zeros(l_ref.shape, F32)
    acc_ref[...] = jnp.zeros(acc_ref.shape, F32)

    def step(j, masked):
        start = pl.multiple_of(j * tq, tq)
        k = k_ref[pl.ds(start, tq), :]
        v = v_ref[pl.ds(start, tq), :]
        s = lax.dot_general(q, k, (((1,), (1,)), ((), ())), preferred_element_type=F32)
        if masked:
            row = lax.broadcasted_iota(jnp.int32, s.shape, 0)
            col = lax.broadcasted_iota(jnp.int32, s.shape, 1)
            s = jnp.where(col <= row, s, MASK_VALUE)
        m_old = m_ref[...]
        m_new = jnp.maximum(m_old, jnp.max(s, axis=-1, keepdims=True))
        alpha = jnp.exp(m_old - m_new)
        p = jnp.exp(s - m_new)
        l_ref[...] = alpha * l_ref[...] + jnp.sum(p, axis=-1, keepdims=True)
        acc_ref[...] = alpha * acc_ref[...] + jnp.dot(p.astype(BF16), v, preferred_element_type=F32)
        m_ref[...] = m_new

    def body(j, carry):
        step(j, False)
        return carry

    lax.fori_loop(0, i, body, 0)
    step(i, True)
    o_ref[...] = (acc_ref[...] / l_ref[...]).astype(o_ref.dtype)


def _mla_attn(q, k, v, batch, seq, tq):
    nq = seq // tq
    return pl.pallas_call(
        functools.partial(_mla_attn_kernel, tq=tq),
        out_shape=jax.ShapeDtypeStruct((batch * seq, MIX_A), BF16),
        grid=(batch, N_HEADS_MLA, nq),
        in_specs=[pl.BlockSpec((tq, QK_PAD), lambda b, h, i: (b * nq + i, h)),
                  pl.BlockSpec((seq, QK_PAD), lambda b, h, i: (b, h)),
                  pl.BlockSpec((seq, V_DIM), lambda b, h, i: (b, h))],
        out_specs=pl.BlockSpec((tq, V_DIM), lambda b, h, i: (b * nq + i, h)),
        scratch_shapes=[pltpu.VMEM((tq, 1), F32), pltpu.VMEM((tq, 1), F32),
                        pltpu.VMEM((tq, V_DIM), F32)],
        compiler_params=_cparams(("parallel", "parallel", "arbitrary")),
        name="mla_attn",
    )(q, k, v)


def _dilated_kernel(q_ref, k_ref, v_ref, bias_ref, o_ref, qf_ref, kf_ref, vf_ref,
                    m_ref, l_ref, acc_ref, *, seq):
    qf_ref[...] = q_ref[...].astype(F32)
    kf_ref[...] = k_ref[...].astype(F32)
    vf_ref[...] = v_ref[...].astype(F32)

    def rows(ref, start, n, dil):
        if dil == 1:
            return ref[pl.ds(start, n), :]
        return ref[pl.ds(start, n, stride=dil), :]

    def block(pat, dil, q_start, with_prev, first_pattern):
        q = rows(qf_ref, q_start, BLOCK, dil).astype(BF16)
        if with_prev:
            k_start, nk, bias = q_start - BLOCK * dil, 2 * BLOCK, bias_ref[pat, 0]
        else:
            k_start, nk, bias = q_start, BLOCK, bias_ref[pat, 0, :, BLOCK:]
        k = rows(kf_ref, k_start, nk, dil).astype(BF16)
        v = rows(vf_ref, k_start, nk, dil).astype(BF16)
        s = lax.dot_general(q, k, (((1,), (1,)), ((), ())), preferred_element_type=F32) + bias
        m_p = jnp.max(s, axis=-1, keepdims=True)
        p = jnp.exp(s - m_p)
        l_p = jnp.sum(p, axis=-1, keepdims=True)
        num_p = jnp.dot(p.astype(BF16), v, preferred_element_type=F32)
        m_p = jnp.broadcast_to(m_p, (BLOCK, LANES))
        l_p = jnp.broadcast_to(l_p, (BLOCK, LANES))
        if first_pattern:
            m_new, l_new, acc_new = m_p, l_p, num_p
        else:
            m_old = rows(m_ref, q_start, BLOCK, dil)
            m_new = jnp.maximum(m_old, m_p)
            a = jnp.exp(m_old - m_new)
            b = jnp.exp(m_p - m_new)
            l_new = rows(l_ref, q_start, BLOCK, dil) * a + l_p * b
            acc_new = rows(acc_ref, q_start, BLOCK, dil) * a + num_p * b
        if dil == 1:
            idx = pl.ds(q_start, BLOCK)
        else:
            idx = pl.ds(q_start, BLOCK, stride=dil)
        m_ref[idx, :] = m_new
        l_ref[idx, :] = l_new
        acc_ref[idx, :] = acc_new

    for pat, (_, dil) in enumerate(DIL_PATTERNS):
        n_blocks = seq // dil // BLOCK
        first = pat == 0

        def residue(r, carry, pat=pat, dil=dil, n_blocks=n_blocks, first=first):
            block(pat, dil, r, False, first)

            def nxt(n, c):
                block(pat, dil, r + n * (BLOCK * dil), True, first)
                return c

            lax.fori_loop(1, n_blocks, nxt, 0)
            return carry

        if dil == 1:
            residue(0, 0)
        else:
            lax.fori_loop(0, dil, residue, 0)

    o_ref[...] = (acc_ref[...] / l_ref[...]).astype(o_ref.dtype)


def _dilated_attn(proj, bias, batch, seq):
    n_pat = len(DIL_PATTERNS)
    col = lambda off: (lambda b, h: (b, off // HEAD_DIM + h))
    return pl.pallas_call(
        functools.partial(_dilated_kernel, seq=seq),
        out_shape=jax.ShapeDtypeStruct((batch * seq, MIX_B), BF16),
        grid=(batch, N_HEADS_DIL),
        in_specs=[pl.BlockSpec((seq, HEAD_DIM), col(E_QB)),
                  pl.BlockSpec((seq, HEAD_DIM), col(E_KB)),
                  pl.BlockSpec((seq, HEAD_DIM), col(E_VB)),
                  pl.BlockSpec((n_pat, 1, BLOCK, 2 * BLOCK), lambda b, h: (0, h, 0, 0))],
        out_specs=pl.BlockSpec((seq, HEAD_DIM), lambda b, h: (b, h)),
        scratch_shapes=[pltpu.VMEM((seq, HEAD_DIM), F32)] * 6,
        compiler_params=_cparams(("parallel", "parallel")),
        name="dilated_attn",
    )(proj, proj, proj, bias)


def _gate_out_kernel(oa_ref, ob_ref, gate_ref, x_ref, w_ref, gf_ref, o_ref, *, final_norm):
    half = oa_ref.shape[1]
    g = gate_ref[...].astype(F32)
    sg = g * jax.nn.sigmoid(g)
    mix_a = (oa_ref[...].astype(F32) * sg[:, :half]).astype(BF16)
    mix_b = (ob_ref[...].astype(F32) * sg[:, half:]).astype(BF16)
    y = jnp.dot(mix_a, w_ref[:half, :], preferred_element_type=F32)
    y = y + jnp.dot(mix_b, w_ref[half:, :], preferred_element_type=F32)
    y = x_ref[...] + y
    if final_norm:
        ms = jnp.mean(y * y, axis=-1, keepdims=True)
        y = y * lax.rsqrt(ms + EPS) * gf_ref[...]
    o_ref[...] = y


def _gate_out_proj(oa, oa_blk, ob, ob_blk, proj, gate_blk, x2d, w, gf, final_norm, tm):
    t, d = x2d.shape
    half = w.shape[0] // 2
    mix = w.shape[0]
    return pl.pallas_call(
        functools.partial(_gate_out_kernel, final_norm=final_norm),
        out_shape=jax.ShapeDtypeStruct((t, d), F32),
        grid=(t // tm,),
        in_specs=[pl.BlockSpec((tm, half), lambda i: (i, oa_blk)),
                  pl.BlockSpec((tm, half), lambda i: (i, ob_blk)),
                  pl.BlockSpec((tm, mix), lambda i: (i, gate_blk)),
                  pl.BlockSpec((tm, d), lambda i: (i, 0)),
                  pl.BlockSpec(w.shape, lambda i: (0, 0)),
                  pl.BlockSpec((1, d), lambda i: (0, 0))],
        out_specs=pl.BlockSpec((tm, d), lambda i: (i, 0)),
        compiler_params=_cparams(("parallel",)),
        name="gate_out_proj",
    )(oa, ob, proj, x2d, w, gf)


def _sb_attn_kernel(q_ref, k_ref, v_ref, u_ref, o_ref, acc_ref, carry_ref, *, tq):
    i = pl.program_id(2)
    q = q_ref[...]
    u = u_ref[...]
    acc_ref[...] = jnp.zeros(acc_ref.shape, F32)
    carry_ref[...] = jnp.zeros(carry_ref.shape, F32)

    def step(j, masked):
        start = pl.multiple_of(j * tq, tq)
        k = k_ref[pl.ds(start, tq), :]
        v = v_ref[pl.ds(start, tq), :]
        z = lax.dot_general(q, k, (((1,), (1,)), ((), ())), preferred_element_type=F32)
        log_rest = -(jnp.maximum(z, 0.0) + jnp.log1p(jnp.exp(-jnp.abs(z))))
        if masked:
            row = lax.broadcasted_iota(jnp.int32, z.shape, 0)
            col = lax.broadcasted_iota(jnp.int32, z.shape, 1)
            before = col < row
            log_rest = jnp.where(before, log_rest, 0.0)
        hi = log_rest.astype(BF16)
        lo = (log_rest - hi.astype(F32)).astype(BF16)
        cum = jnp.dot(hi, u, preferred_element_type=F32) + jnp.dot(lo, u, preferred_element_type=F32)
        carry = carry_ref[...]
        a = jnp.exp(z + cum + carry)
        if masked:
            a = jnp.where(before, a, 0.0)
        acc_ref[...] += jnp.dot(a.astype(BF16), v, preferred_element_type=F32)
        carry_ref[...] = carry + cum[:, 0:1]

    step(i, True)

    def body(jj, c):
        step(i - 1 - jj, False)
        return c

    lax.fori_loop(0, i, body, 0)
    o_ref[...] = acc_ref[...].astype(o_ref.dtype)


def _sb_attn(proj, batch, seq, tq):
    nq = seq // tq
    row = np.arange(tq)[:, None]
    colv = np.arange(tq)[None, :]
    u = jnp.asarray((row >= colv).astype(np.float32), BF16)
    hcol = lambda off: (lambda b, h, i: (b, off // HEAD_DIM + h))
    return pl.pallas_call(
        functools.partial(_sb_attn_kernel, tq=tq),
        out_shape=jax.ShapeDtypeStruct((batch * seq, MIX_ODD), BF16),
        grid=(batch, N_HEADS_SB, nq),
        in_specs=[pl.BlockSpec((tq, HEAD_DIM), lambda b, h, i: (b * nq + i, h)),
                  pl.BlockSpec((seq, HEAD_DIM), hcol(MIX_ODD)),
                  pl.BlockSpec((seq, HEAD_DIM), hcol(2 * MIX_ODD)),
                  pl.BlockSpec((tq, tq), lambda b, h, i: (0, 0))],
        out_specs=pl.BlockSpec((tq, HEAD_DIM), lambda b, h, i: (b * nq + i, h)),
        scratch_shapes=[pltpu.VMEM((tq, HEAD_DIM), F32), pltpu.VMEM((tq, 1), F32)],
        compiler_params=_cparams(("parallel", "parallel", "arbitrary")),
        name="sb_attn",
    )(proj, proj, proj, u)


def _rotate_half_partner(w):
    half = QK_ROPE // 2
    return jnp.concatenate([-w[..., half:], w[..., :half]], axis=-1)


def _even_in_weight(w_in):
    c_q, c_kv, k_rope, q_b, k_b, v_b, gate = jnp.split(
        w_in, np.cumsum((Q_LORA, KV_LORA, QK_ROPE, MIX_B, MIX_B, MIX_B)).tolist(), axis=-1)
    pad = jnp.zeros((w_in.shape[0], E_WIDTH - E_END), w_in.dtype)
    w = jnp.concatenate([gate, q_b, k_b, v_b, c_q, c_kv, k_rope, _rotate_half_partner(k_rope), pad], axis=-1)
    return w.astype(BF16)


def _uq_weight(w_uq):
    w = w_uq.reshape(Q_LORA, N_HEADS_MLA, QK_NOPE + QK_ROPE)
    rot = w[..., QK_NOPE:]
    w = jnp.concatenate([w[..., :QK_NOPE], rot, _rotate_half_partner(rot)], axis=-1)
    return w.reshape(Q_LORA, N_HEADS_MLA * QK_PAD).astype(BF16)


def _col_scale(width, start, stop, scale):
    col = np.ones((1, width), np.float32)
    col[:, start:stop] = scale
    return jnp.asarray(col)


def _cos_sin_table(seq):
    half = QK_ROPE // 2
    inv = 1.0 / (ROPE_THETA ** (jnp.arange(half, dtype=F32) / half))
    ang = jnp.arange(seq).astype(F32)[:, None] * inv[None, :]
    cos, sin = jnp.cos(ang), jnp.sin(ang)
    return jnp.concatenate([cos, cos, sin, sin], axis=-1)


def kernel(x, norm_gain, w_in_even, q_norm_gain, kv_norm_gain, w_uq, w_ukv, w_out_even, rel_bias,
           w_in_odd, w_out_odd, final_norm_gain):
    batch, seq, d_model = x.shape
    t = batch * seq
    x2d = x.reshape(t, d_model)
    head_scale = HEAD_DIM ** -0.5

    proj0 = _rms_proj(x2d, norm_gain[0:1], _even_in_weight(w_in_even[0]),
                      _col_scale(E_WIDTH, E_QB, E_KB, head_scale), tm=512, tn=PROJ_TN)
    q_a, k_a, v_a = _mla_prep(proj0, _cos_sin_table(seq), q_norm_gain[0:1], kv_norm_gain[0:1],
                              _uq_weight(w_uq[0]), w_ukv[0].astype(BF16), seq, tm=512)
    o_a = _mla_attn(q_a, k_a, v_a, batch, seq, tq=512)
    o_b = _dilated_attn(proj0, _bias_tiles(rel_bias), batch, seq)
    x1 = _gate_out_proj(o_a, 0, o_b, 0, proj0, E_GATE // MIX_EVEN, x2d, w_out_even[0].astype(BF16),
                        final_norm_gain[None, :], final_norm=False, tm=256)

    proj1 = _rms_proj(x1, norm_gain[1:2], w_in_odd[0].astype(BF16),
                      _col_scale(4 * MIX_ODD, 0, MIX_ODD, head_scale), tm=512, tn=1024)
    o_c = _sb_attn(proj1, batch, seq, tq=256)
    out = _gate_out_proj(o_c, 0, o_c, 1, proj1, 3, x1, w_out_odd[0].astype(BF16),
                         final_norm_gain[None, :], final_norm=True, tm=256)
    return out.reshape(batch, seq, d_model)
```

```python
import functools
import math

import numpy as np
import jax
import jax.numpy as jnp
from jax import lax
from jax.experimental import pallas as pl
from jax.experimental.pallas import tpu as pltpu

F32 = jnp.float32
BF16 = jnp.bfloat16

EPS = 1e-6
N_HEADS_MLA = 8
Q_LORA = 512
KV_LORA = 512
QK_NOPE = 128
QK_ROPE = 64
V_DIM = 128
ROPE_THETA = 10000.0
N_HEADS_DIL = 8
HEAD_DIM = 128
DIL_PATTERNS = ((128, 1), (512, 4), (2048, 16))
N_BUCKETS = 32
BUCKET_MAX_DIST = 2048
N_HEADS_SB = 16
BLOCK = 128
MASK_VALUE = -1e30
LOG2E = 1.4426950408889634
SIGN_BIT = 0x80000000

LANES = 128
VMEM_LIMIT = 48 * 1024 * 1024

MIX_A = N_HEADS_MLA * V_DIM
MIX_B = N_HEADS_DIL * HEAD_DIM
MIX_EVEN = MIX_A + MIX_B
MIX_ODD = N_HEADS_SB * HEAD_DIM
E_GATE = 0
E_QB = E_GATE + MIX_EVEN
E_KB = E_QB + MIX_B
E_VB = E_KB + MIX_B
E_CQ = E_VB + MIX_B
E_CKV = E_CQ + Q_LORA
E_KR = E_CKV + KV_LORA
E_END = E_KR + 2 * QK_ROPE
PROJ_TN = 1280
E_WIDTH = -(-E_END // PROJ_TN) * PROJ_TN
QK_PAD = 2 * LANES
V_PAD = 2 * LANES


def _cparams(sem):
    return pltpu.CompilerParams(dimension_semantics=sem, vmem_limit_bytes=VMEM_LIMIT)


def _bucket_tables():
    qi = np.arange(BLOCK)[:, None]
    kj = np.arange(2 * BLOCK)[None, :]
    rel = BLOCK + qi - kj
    max_exact = N_BUCKETS // 2
    tabs = []
    for window, dil in DIL_PATTERNS:
        span = window // dil
        dist = np.maximum(rel, 0) * dil
        d = np.maximum(dist.astype(np.float64), 1.0)
        frac = np.log(d / max_exact) / math.log(BUCKET_MAX_DIST / max_exact) * (N_BUCKETS - max_exact)
        large = np.minimum(max_exact + np.trunc(frac).astype(np.int64), N_BUCKETS - 1)
        bucket = np.where(dist < max_exact, dist, large)
        valid = (rel >= 0) & (rel <= span)
        tabs.append(np.where(valid, bucket, -1).astype(np.int32))
    return np.stack(tabs)


def _bias_tiles_kernel(bucket_ref, rel_bias_ref, o_ref):
    h = pl.program_id(1)
    bucket = bucket_ref[0]
    out = jnp.full(bucket.shape, MASK_VALUE, F32)
    for i in range(N_BUCKETS):
        out = jnp.where(bucket == i, rel_bias_ref[i, h] * LOG2E, out)
    o_ref[0, 0, 0] = out
    o_ref[0, 0, 1, :, :BLOCK] = out[:, BLOCK:]
    o_ref[0, 0, 1, :, BLOCK:] = jnp.full((BLOCK, BLOCK), MASK_VALUE, F32)


def _bias_tiles(rel_bias):
    n_pat = len(DIL_PATTERNS)
    buckets = jnp.asarray(_bucket_tables())
    return pl.pallas_call(
        _bias_tiles_kernel,
        out_shape=jax.ShapeDtypeStruct((n_pat, N_HEADS_DIL, 2, BLOCK, 2 * BLOCK), F32),
        grid=(n_pat, N_HEADS_DIL),
        in_specs=[pl.BlockSpec((1, BLOCK, 2 * BLOCK), lambda p, h: (p, 0, 0)),
                  pl.BlockSpec(memory_space=pltpu.SMEM)],
        out_specs=pl.BlockSpec((1, 1, 2, BLOCK, 2 * BLOCK), lambda p, h: (p, h, 0, 0, 0)),
        compiler_params=_cparams(("arbitrary", "arbitrary")),
        name="bias_tiles",
    )(buckets, rel_bias)


def _rms_proj_kernel(x_ref, g_ref, w_ref, cs_ref, o_ref, h_ref):
    @pl.when(pl.program_id(1) == 0)
    def _():
        x = x_ref[...]
        ms = jnp.mean(x * x, axis=-1, keepdims=True)
        h_ref[...] = (x * lax.rsqrt(ms + EPS) * g_ref[...]).astype(BF16)

    acc = jnp.dot(h_ref[...], w_ref[...], preferred_element_type=F32)
    o_ref[...] = (acc * cs_ref[...]).astype(o_ref.dtype)


def _rms_proj(x2d, gain, w, col_scale, tm, tn):
    t, d = x2d.shape
    n = w.shape[1]
    return pl.pallas_call(
        _rms_proj_kernel,
        out_shape=jax.ShapeDtypeStruct((t, n), BF16),
        grid=(t // tm, n // tn),
        in_specs=[pl.BlockSpec((tm, d), lambda i, j: (i, 0)),
                  pl.BlockSpec((1, d), lambda i, j: (0, 0)),
                  pl.BlockSpec((d, tn), lambda i, j: (0, j)),
                  pl.BlockSpec((1, tn), lambda i, j: (0, j))],
        out_specs=pl.BlockSpec((tm, tn), lambda i, j: (i, j)),
        scratch_shapes=[pltpu.VMEM((tm, d), BF16)],
        compiler_params=_cparams(("parallel", "arbitrary")),
        name="rms_proj",
    )(x2d, gain, w, col_scale)


def _rope_chunk(chunk, cs):
    r = chunk * cs
    r = r + pltpu.roll(r, QK_ROPE, axis=1)
    lane = lax.broadcasted_iota(jnp.int32, r.shape, 1)
    return jnp.where(lane < QK_ROPE, r, 0.0)


def _mla_prep_kernel(cq_ref, ckv_ref, kr_ref, cs_ref, gq_ref, gkv_ref, wuq_ref, wukv_ref,
                     q_ref, k_ref, v_ref):
    def latent_norm(c_ref, g_ref):
        c = c_ref[...].astype(F32)
        ms = jnp.mean(c * c, axis=-1, keepdims=True)
        return (c * lax.rsqrt(ms + EPS) * g_ref[...]).astype(BF16)

    cs = cs_ref[...]
    scale = LOG2E * (QK_NOPE + QK_ROPE) ** -0.5
    q = jnp.dot(latent_norm(cq_ref, gq_ref), wuq_ref[...], preferred_element_type=F32)
    for h in range(N_HEADS_MLA):
        lo = h * QK_PAD
        q_ref[:, lo:lo + LANES] = (q[:, lo:lo + LANES] * scale).astype(BF16)
        q_ref[:, lo + LANES:lo + QK_PAD] = (_rope_chunk(q[:, lo + LANES:lo + QK_PAD], cs) * scale).astype(BF16)

    k_rot = _rope_chunk(kr_ref[...].astype(F32), cs).astype(BF16)
    ones_col = jnp.ones((cs.shape[0], V_PAD - V_DIM), BF16)
    kv = jnp.dot(latent_norm(ckv_ref, gkv_ref), wukv_ref[...], preferred_element_type=F32)
    for h in range(N_HEADS_MLA):
        lo = h * (QK_NOPE + V_DIM)
        k_ref[:, h * QK_PAD:h * QK_PAD + LANES] = kv[:, lo:lo + QK_NOPE].astype(BF16)
        k_ref[:, h * QK_PAD + LANES:(h + 1) * QK_PAD] = k_rot
        v_ref[:, h * V_PAD:h * V_PAD + V_DIM] = kv[:, lo + QK_NOPE:lo + QK_NOPE + V_DIM].astype(BF16)
        v_ref[:, h * V_PAD + V_DIM:(h + 1) * V_PAD] = ones_col


def _mla_prep(proj, cos_sin, gq, gkv, wuq, wukv, seq, tm):
    t = proj.shape[0]
    n_seq_tiles = seq // tm
    const = lambda i: (0, 0)
    return pl.pallas_call(
        _mla_prep_kernel,
        out_shape=(jax.ShapeDtypeStruct((t, N_HEADS_MLA * QK_PAD), BF16),
                   jax.ShapeDtypeStruct((t, N_HEADS_MLA * QK_PAD), BF16),
                   jax.ShapeDtypeStruct((t, N_HEADS_MLA * V_PAD), BF16)),
        grid=(t // tm,),
        in_specs=[pl.BlockSpec((tm, Q_LORA), lambda i: (i, E_CQ // Q_LORA)),
                  pl.BlockSpec((tm, KV_LORA), lambda i: (i, E_CKV // KV_LORA)),
                  pl.BlockSpec((tm, LANES), lambda i: (i, E_KR // LANES)),
                  pl.BlockSpec((tm, LANES), lambda i: (i % n_seq_tiles, 0)),
                  pl.BlockSpec((1, Q_LORA), const),
                  pl.BlockSpec((1, KV_LORA), const),
                  pl.BlockSpec(wuq.shape, const),
                  pl.BlockSpec(wukv.shape, const)],
        out_specs=(pl.BlockSpec((tm, N_HEADS_MLA * QK_PAD), lambda i: (i, 0)),
                   pl.BlockSpec((tm, N_HEADS_MLA * QK_PAD), lambda i: (i, 0)),
                   pl.BlockSpec((tm, N_HEADS_MLA * V_PAD), lambda i: (i, 0))),
        compiler_params=_cparams(("parallel",)),
        name="mla_prep",
    )(proj, proj, proj, cos_sin, gq, gkv, wuq, wukv)


MLA_TQ = 1024
MLA_TK = 512


def _mla_attn_kernel(q_ref, k_ref, v_ref, o_ref, m_ref, acc_ref):
    sup = pl.program_id(2)
    n_sub = MLA_TQ // MLA_TK
    m_ref[...] = jnp.full(m_ref.shape, -jnp.inf, F32)
    acc_ref[...] = jnp.zeros(acc_ref.shape, F32)

    def step(row_lo, kb, masked):
        start = pl.multiple_of(kb * MLA_TK, MLA_TK)
        k = k_ref[pl.ds(start, MLA_TK), :]
        v = v_ref[pl.ds(start, MLA_TK), :]
        q = q_ref[row_lo:, :]
        s = lax.dot_general(q, k, (((1,), (1,)), ((), ())), preferred_element_type=F32)
        if masked:
            row = lax.broadcasted_iota(jnp.int32, s.shape, 0)
            col = lax.broadcasted_iota(jnp.int32, s.shape, 1)
            s = jnp.where(col <= row, s, MASK_VALUE)
        m_old = m_ref[row_lo:, :]
        m_new = jnp.maximum(m_old, jnp.max(s, axis=-1, keepdims=True))
        alpha = jnp.exp2(m_old - m_new)
        p = jnp.exp2(s - jnp.concatenate([m_new] * (MLA_TK // LANES), axis=1))
        pv = jnp.dot(p.astype(BF16), v, preferred_element_type=F32)
        acc_ref[row_lo:, :] = jnp.concatenate([alpha] * (V_PAD // LANES), axis=1) * acc_ref[row_lo:, :] + pv
        m_ref[row_lo:, :] = m_new

    for b in range(n_sub):
        step(b * MLA_TK, sup * n_sub + b, True)

    def body(j, carry):
        step(0, j, False)
        return carry

    lax.fori_loop(0, sup * n_sub, body, 0)
    acc = acc_ref[...]
    o_ref[...] = (acc[:, :V_DIM] / acc[:, V_DIM:]).astype(o_ref.dtype)


def _mla_attn(q, k, v, batch, seq):
    nq = seq // MLA_TQ
    return pl.pallas_call(
        _mla_attn_kernel,
        out_shape=jax.ShapeDtypeStruct((batch * seq, MIX_A), BF16),
        grid=(batch, N_HEADS_MLA, nq),
        in_specs=[pl.BlockSpec((MLA_TQ, QK_PAD), lambda b, h, i: (b * nq + i, h)),
                  pl.BlockSpec((seq, QK_PAD), lambda b, h, i: (b, h)),
                  pl.BlockSpec((seq, V_PAD), lambda b, h, i: (b, h))],
        out_specs=pl.BlockSpec((MLA_TQ, V_DIM), lambda b, h, i: (b * nq + i, h)),
        scratch_shapes=[pltpu.VMEM((MLA_TQ, LANES), F32), pltpu.VMEM((MLA_TQ, V_PAD), F32)],
        compiler_params=_cparams(("parallel", "parallel", "arbitrary")),
        name="mla_attn",
    )(q, k, v)


DIL_GROUP = 4


def _dilated_kernel(q_ref, k_ref, v_ref, bias_ref, o_ref, qf_ref, kf_ref, vf_ref,
                    m_ref, l_ref, acc_ref, *, seq):
    qf_ref[...] = q_ref[...].astype(F32)
    kf_ref[...] = k_ref[...].astype(F32)
    vf_ref[...] = v_ref[...].astype(F32)

    def idx(start, n, dil):
        return pl.ds(start, n) if dil == 1 else pl.ds(start, n, stride=dil)

    def block(pat, dil, unit):
        r = lax.rem(unit, dil)
        n = lax.div(unit, dil)
        q_start = r + n * (BLOCK * dil)
        k_start = r + jnp.maximum(n - 1, 0) * (BLOCK * dil)
        if dil == 1:
            q_start = pl.multiple_of(q_start, BLOCK)
            k_start = pl.multiple_of(k_start, BLOCK)
        q = qf_ref[idx(q_start, BLOCK, dil), :].astype(BF16)
        k = kf_ref[idx(k_start, 2 * BLOCK, dil), :].astype(BF16)
        v = vf_ref[idx(k_start, 2 * BLOCK, dil), :].astype(BF16)
        bias = bias_ref[pat, 0, jnp.where(n == 0, 1, 0)]
        s = lax.dot_general(q, k, (((1,), (1,)), ((), ())), preferred_element_type=F32) + bias
        m_p = jnp.broadcast_to(jnp.max(s, axis=-1, keepdims=True), (BLOCK, LANES))
        p = jnp.exp2(s - jnp.concatenate([m_p, m_p], axis=1))
        v_ones = jnp.concatenate([v, jnp.ones((2 * BLOCK, LANES), BF16)], axis=1)
        num_l = jnp.dot(p.astype(BF16), v_ones, preferred_element_type=F32)
        return idx(q_start, BLOCK, dil), m_p, num_l[:, HEAD_DIM:], num_l[:, :HEAD_DIM]

    def merge(where, m_p, l_p, num_p, first_pattern):
        if first_pattern:
            return m_p, l_p, num_p
        m_old = m_ref[where, :]
        m_new = jnp.maximum(m_old, m_p)
        a = jnp.exp2(m_old - m_new)
        b = jnp.exp2(m_p - m_new)
        return m_new, l_ref[where, :] * a + l_p * b, acc_ref[where, :] * a + num_p * b

    for pat, (_, dil) in enumerate(DIL_PATTERNS):
        n_units = seq // BLOCK

        def group(g, carry, pat=pat, dil=dil):
            parts = [block(pat, dil, g * DIL_GROUP + u) for u in range(DIL_GROUP)]
            merged = [(part[0],) + merge(*part, first_pattern=pat == 0) for part in parts]
            for where, m_new, l_new, acc_new in merged:
                m_ref[where, :] = m_new
                l_ref[where, :] = l_new
                acc_ref[where, :] = acc_new
            return carry

        lax.fori_loop(0, n_units // DIL_GROUP, group, 0)

    o_ref[...] = (acc_ref[...] / l_ref[...]).astype(o_ref.dtype)


def _dilated_attn(proj, bias, batch, seq):
    n_pat = len(DIL_PATTERNS)
    assert all(seq // dil // BLOCK >= 2 for _, dil in DIL_PATTERNS), "every residue class needs two key blocks"
    col = lambda off: (lambda b, h: (b, off // HEAD_DIM + h))
    return pl.pallas_call(
        functools.partial(_dilated_kernel, seq=seq),
        out_shape=jax.ShapeDtypeStruct((batch * seq, MIX_B), BF16),
        grid=(batch, N_HEADS_DIL),
        in_specs=[pl.BlockSpec((seq, HEAD_DIM), col(E_QB)),
                  pl.BlockSpec((seq, HEAD_DIM), col(E_KB)),
                  pl.BlockSpec((seq, HEAD_DIM), col(E_VB)),
                  pl.BlockSpec((n_pat, 1, 2, BLOCK, 2 * BLOCK), lambda b, h: (0, h, 0, 0, 0))],
        out_specs=pl.BlockSpec((seq, HEAD_DIM), lambda b, h: (b, h)),
        scratch_shapes=[pltpu.VMEM((seq, HEAD_DIM), F32)] * 6,
        compiler_params=_cparams(("parallel", "parallel")),
        name="dilated_attn",
    )(proj, proj, proj, bias)


def _gate_out_kernel(oa_ref, ob_ref, gate_ref, x_ref, w_ref, gf_ref, o_ref, *, final_norm):
    half = oa_ref.shape[1]
    g = gate_ref[...].astype(F32)
    sg = g * jax.nn.sigmoid(g)
    mix_a = (oa_ref[...].astype(F32) * sg[:, :half]).astype(BF16)
    mix_b = (ob_ref[...].astype(F32) * sg[:, half:]).astype(BF16)
    y = jnp.dot(mix_a, w_ref[:half, :], preferred_element_type=F32)
    y = y + jnp.dot(mix_b, w_ref[half:, :], preferred_element_type=F32)
    y = x_ref[...] + y
    if final_norm:
        ms = jnp.mean(y * y, axis=-1, keepdims=True)
        y = y * lax.rsqrt(ms + EPS) * gf_ref[...]
    o_ref[...] = y


def _gate_out_proj(oa, oa_blk, ob, ob_blk, proj, gate_blk, x2d, w, gf, final_norm, tm):
    t, d = x2d.shape
    half = w.shape[0] // 2
    mix = w.shape[0]
    return pl.pallas_call(
        functools.partial(_gate_out_kernel, final_norm=final_norm),
        out_shape=jax.ShapeDtypeStruct((t, d), F32),
        grid=(t // tm,),
        in_specs=[pl.BlockSpec((tm, half), lambda i: (i, oa_blk)),
                  pl.BlockSpec((tm, half), lambda i: (i, ob_blk)),
                  pl.BlockSpec((tm, mix), lambda i: (i, gate_blk)),
                  pl.BlockSpec((tm, d), lambda i: (i, 0)),
                  pl.BlockSpec(w.shape, lambda i: (0, 0)),
                  pl.BlockSpec((1, d), lambda i: (0, 0))],
        out_specs=pl.BlockSpec((tm, d), lambda i: (i, 0)),
        compiler_params=_cparams(("parallel",)),
        name="gate_out_proj",
    )(oa, ob, proj, x2d, w, gf)


SB_TQ = 1024
SB_TK = 256


def _sb_attn_kernel(q_ref, k_ref, v_ref, u_ref, o_ref, acc_ref, carry_ref):
    sup = pl.program_id(2)
    n_sub = SB_TQ // SB_TK
    acc_ref[...] = jnp.zeros(acc_ref.shape, F32)
    carry_ref[...] = jnp.zeros(carry_ref.shape, F32)

    def step(row_lo, kb, masked):
        start = pl.multiple_of(kb * SB_TK, SB_TK)
        k = k_ref[pl.ds(start, SB_TK), :]
        v = v_ref[pl.ds(start, SB_TK), :]
        q = q_ref[row_lo:, :]
        nz = lax.dot_general(q, k, (((1,), (1,)), ((), ())), preferred_element_type=F32)
        neg_abs = lax.bitcast_convert_type(
            lax.bitcast_convert_type(nz, jnp.uint32) | jnp.uint32(SIGN_BIT), F32)
        log_rest = jnp.minimum(nz, 0.0) - jnp.log(1.0 + jnp.exp2(neg_abs)) * LOG2E
        if masked:
            row = lax.broadcasted_iota(jnp.int32, nz.shape, 0)
            col = lax.broadcasted_iota(jnp.int32, nz.shape, 1)
            before = col < row
            log_rest = jnp.where(before, log_rest, 0.0)
        hi = log_rest.astype(BF16)
        lo = (log_rest - hi.astype(F32)).astype(BF16)
        cum = jnp.dot(jnp.concatenate([hi, lo], axis=1), u_ref[...], preferred_element_type=F32)
        carry = carry_ref[row_lo:, :]
        a = jnp.exp2(cum + carry - nz)
        if masked:
            a = jnp.where(before, a, 0.0)
        acc_ref[row_lo:, :] += jnp.dot(a.astype(BF16), v, preferred_element_type=F32)
        carry_ref[row_lo:, :] = carry + cum[:, 0:1]

    for b in reversed(range(n_sub)):
        step(b * SB_TK, sup * n_sub + b, True)

    def body(jj, c):
        step(0, sup * n_sub - 1 - jj, False)
        return c

    lax.fori_loop(0, sup * n_sub, body, 0)
    o_ref[...] = acc_ref[...].astype(o_ref.dtype)


def _sb_attn(proj, batch, seq):
    nq = seq // SB_TQ
    row = np.arange(SB_TK)[:, None]
    colv = np.arange(SB_TK)[None, :]
    tri = (row >= colv).astype(np.float32)
    u = jnp.asarray(np.concatenate([tri, tri], axis=0), BF16)
    hcol = lambda off: (lambda b, h, i: (b, off // HEAD_DIM + h))
    return pl.pallas_call(
        _sb_attn_kernel,
        out_shape=jax.ShapeDtypeStruct((batch * seq, MIX_ODD), BF16),
        grid=(batch, N_HEADS_SB, nq),
        in_specs=[pl.BlockSpec((SB_TQ, HEAD_DIM), lambda b, h, i: (b * nq + i, h)),
                  pl.BlockSpec((seq, HEAD_DIM), hcol(MIX_ODD)),
                  pl.BlockSpec((seq, HEAD_DIM), hcol(2 * MIX_ODD)),
                  pl.BlockSpec((2 * SB_TK, SB_TK), lambda b, h, i: (0, 0))],
        out_specs=pl.BlockSpec((SB_TQ, HEAD_DIM), lambda b, h, i: (b * nq + i, h)),
        scratch_shapes=[pltpu.VMEM((SB_TQ, HEAD_DIM), F32), pltpu.VMEM((SB_TQ, 1), F32)],
        compiler_params=_cparams(("parallel", "parallel", "arbitrary")),
        name="sb_attn",
    )(proj, proj, proj, u)


def _rotate_half_partner(w):
    half = QK_ROPE // 2
    return jnp.concatenate([-w[..., half:], w[..., :half]], axis=-1)


def _even_in_weight(w_in):
    c_q, c_kv, k_rope, q_b, k_b, v_b, gate = jnp.split(
        w_in, np.cumsum((Q_LORA, KV_LORA, QK_ROPE, MIX_B, MIX_B, MIX_B)).tolist(), axis=-1)
    pad = jnp.zeros((w_in.shape[0], E_WIDTH - E_END), w_in.dtype)
    w = jnp.concatenate([gate, q_b, k_b, v_b, c_q, c_kv, k_rope, _rotate_half_partner(k_rope), pad], axis=-1)
    return w.astype(BF16)


def _uq_weight(w_uq):
    w = w_uq.reshape(Q_LORA, N_HEADS_MLA, QK_NOPE + QK_ROPE)
    rot = w[..., QK_NOPE:]
    w = jnp.concatenate([w[..., :QK_NOPE], rot, _rotate_half_partner(rot)], axis=-1)
    return w.reshape(Q_LORA, N_HEADS_MLA * QK_PAD).astype(BF16)


def _col_scale(width, start, stop, scale):
    col = np.ones((1, width), np.float32)
    col[:, start:stop] = scale
    return jnp.asarray(col)


def _cos_sin_table(seq):
    half = QK_ROPE // 2
    inv = 1.0 / (ROPE_THETA ** (jnp.arange(half, dtype=F32) / half))
    ang = jnp.arange(seq).astype(F32)[:, None] * inv[None, :]
    cos, sin = jnp.cos(ang), jnp.sin(ang)
    return jnp.concatenate([cos, cos, sin, sin], axis=-1)


def kernel(x, norm_gain, w_in_even, q_norm_gain, kv_norm_gain, w_uq, w_ukv, w_out_even, rel_bias,
           w_in_odd, w_out_odd, final_norm_gain):
    batch, seq, d_model = x.shape
    t = batch * seq
    x2d = x.reshape(t, d_model)
    head_scale = HEAD_DIM ** -0.5

    proj0 = _rms_proj(x2d, norm_gain[0:1], _even_in_weight(w_in_even[0]),
                      _col_scale(E_WIDTH, E_QB, E_KB, head_scale * LOG2E), tm=512, tn=PROJ_TN)
    q_a, k_a, v_a = _mla_prep(proj0, _cos_sin_table(seq), q_norm_gain[0:1], kv_norm_gain[0:1],
                              _uq_weight(w_uq[0]), w_ukv[0].astype(BF16), seq, tm=512)
    o_a = _mla_attn(q_a, k_a, v_a, batch, seq)
    o_b = _dilated_attn(proj0, _bias_tiles(rel_bias), batch, seq)
    x1 = _gate_out_proj(o_a, 0, o_b, 0, proj0, E_GATE // MIX_EVEN, x2d, w_out_even[0].astype(BF16),
                        final_norm_gain[None, :], final_norm=False, tm=256)

    proj1 = _rms_proj(x1, norm_gain[1:2], w_in_odd[0].astype(BF16),
                      _col_scale(4 * MIX_ODD, 0, MIX_ODD, -head_scale * LOG2E), tm=512, tn=1024)
    o_c = _sb_attn(proj1, batch, seq)
    out = _gate_out_proj(o_c, 0, o_c, 1, proj1, 3, x1, w_out_odd[0].astype(BF16),
                         final_norm_gain[None, :], final_norm=True, tm=256)
    return out.reshape(batch, seq, d_model)
```

```python
import functools
import math

import numpy as np
import jax
import jax.numpy as jnp
from jax import lax
from jax.experimental import pallas as pl
from jax.experimental.pallas import tpu as pltpu

F32 = jnp.float32
BF16 = jnp.bfloat16

EPS = 1e-6
N_HEADS_MLA = 8
Q_LORA = 512
KV_LORA = 512
QK_NOPE = 128
QK_ROPE = 64
V_DIM = 128
ROPE_THETA = 10000.0
N_HEADS_DIL = 8
HEAD_DIM = 128
DIL_PATTERNS = ((128, 1), (512, 4), (2048, 16))
N_BUCKETS = 32
BUCKET_MAX_DIST = 2048
N_HEADS_SB = 16
BLOCK = 128
MASK_VALUE = -1e30
LOG2E = 1.4426950408889634
SIGN_BIT = 0x80000000

LANES = 128
VMEM_LIMIT = 48 * 1024 * 1024

MIX_A = N_HEADS_MLA * V_DIM
MIX_B = N_HEADS_DIL * HEAD_DIM
MIX_EVEN = MIX_A + MIX_B
MIX_ODD = N_HEADS_SB * HEAD_DIM
E_GATE = 0
E_QB = E_GATE + MIX_EVEN
E_KB = E_QB + MIX_B
E_VB = E_KB + MIX_B
E_CQ = E_VB + MIX_B
E_CKV = E_CQ + Q_LORA
E_KR = E_CKV + KV_LORA
E_END = E_KR + 2 * QK_ROPE
PROJ_TN = 1280
E_WIDTH = -(-E_END // PROJ_TN) * PROJ_TN
QK_PAD = 2 * LANES
V_PAD = 2 * LANES


def _cparams(sem):
    return pltpu.CompilerParams(dimension_semantics=sem, vmem_limit_bytes=VMEM_LIMIT)


def _bucket_tables():
    qi = np.arange(BLOCK)[:, None]
    kj = np.arange(2 * BLOCK)[None, :]
    rel = BLOCK + qi - kj
    max_exact = N_BUCKETS // 2
    tabs = []
    for window, dil in DIL_PATTERNS:
        span = window // dil
        dist = np.maximum(rel, 0) * dil
        d = np.maximum(dist.astype(np.float64), 1.0)
        frac = np.log(d / max_exact) / math.log(BUCKET_MAX_DIST / max_exact) * (N_BUCKETS - max_exact)
        large = np.minimum(max_exact + np.trunc(frac).astype(np.int64), N_BUCKETS - 1)
        bucket = np.where(dist < max_exact, dist, large)
        valid = (rel >= 0) & (rel <= span)
        tabs.append(np.where(valid, bucket, -1).astype(np.int32))
    return np.stack(tabs)


def _bias_tiles_kernel(bucket_ref, rel_bias_ref, o_ref):
    h = pl.program_id(1)
    bucket = bucket_ref[0]
    out = jnp.full(bucket.shape, MASK_VALUE, F32)
    for i in range(N_BUCKETS):
        out = jnp.where(bucket == i, rel_bias_ref[i, h] * LOG2E, out)
    o_ref[0, 0, 0] = out
    o_ref[0, 0, 1, :, :BLOCK] = out[:, BLOCK:]
    o_ref[0, 0, 1, :, BLOCK:] = jnp.full((BLOCK, BLOCK), MASK_VALUE, F32)


def _bias_tiles(rel_bias):
    n_pat = len(DIL_PATTERNS)
    buckets = jnp.asarray(_bucket_tables())
    return pl.pallas_call(
        _bias_tiles_kernel,
        out_shape=jax.ShapeDtypeStruct((n_pat, N_HEADS_DIL, 2, BLOCK, 2 * BLOCK), F32),
        grid=(n_pat, N_HEADS_DIL),
        in_specs=[pl.BlockSpec((1, BLOCK, 2 * BLOCK), lambda p, h: (p, 0, 0)),
                  pl.BlockSpec(memory_space=pltpu.SMEM)],
        out_specs=pl.BlockSpec((1, 1, 2, BLOCK, 2 * BLOCK), lambda p, h: (p, h, 0, 0, 0)),
        compiler_params=_cparams(("arbitrary", "arbitrary")),
        name="bias_tiles",
    )(buckets, rel_bias)


def _rms_proj_kernel(x_ref, g_ref, w_ref, cs_ref, o_ref, h_ref):
    @pl.when(pl.program_id(1) == 0)
    def _():
        x = x_ref[...]
        ms = jnp.mean(x * x, axis=-1, keepdims=True)
        h_ref[...] = (x * lax.rsqrt(ms + EPS) * g_ref[...]).astype(BF16)

    acc = jnp.dot(h_ref[...], w_ref[...], preferred_element_type=F32)
    o_ref[...] = (acc * cs_ref[...]).astype(o_ref.dtype)


def _rms_proj(x2d, gain, w, col_scale, tm, tn):
    t, d = x2d.shape
    n = w.shape[1]
    return pl.pallas_call(
        _rms_proj_kernel,
        out_shape=jax.ShapeDtypeStruct((t, n), BF16),
        grid=(t // tm, n // tn),
        in_specs=[pl.BlockSpec((tm, d), lambda i, j: (i, 0)),
                  pl.BlockSpec((1, d), lambda i, j: (0, 0)),
                  pl.BlockSpec((d, tn), lambda i, j: (0, j)),
                  pl.BlockSpec((1, tn), lambda i, j: (0, j))],
        out_specs=pl.BlockSpec((tm, tn), lambda i, j: (i, j)),
        scratch_shapes=[pltpu.VMEM((tm, d), BF16)],
        compiler_params=_cparams(("parallel", "arbitrary")),
        name="rms_proj",
    )(x2d, gain, w, col_scale)


def _rope_chunk(chunk, cs):
    r = chunk * cs
    r = r + pltpu.roll(r, QK_ROPE, axis=1)
    lane = lax.broadcasted_iota(jnp.int32, r.shape, 1)
    return jnp.where(lane < QK_ROPE, r, 0.0)


def _mla_prep_kernel(cq_ref, ckv_ref, kr_ref, cs_ref, gq_ref, gkv_ref, wuq_ref, wukv_ref,
                     q_ref, k_ref, v_ref):
    def latent_norm(c_ref, g_ref):
        c = c_ref[...].astype(F32)
        ms = jnp.mean(c * c, axis=-1, keepdims=True)
        return (c * lax.rsqrt(ms + EPS) * g_ref[...]).astype(BF16)

    cs = cs_ref[...]
    scale = LOG2E * (QK_NOPE + QK_ROPE) ** -0.5
    q = jnp.dot(latent_norm(cq_ref, gq_ref), wuq_ref[...], preferred_element_type=F32)
    for h in range(N_HEADS_MLA):
        lo = h * QK_PAD
        q_ref[:, lo:lo + LANES] = (q[:, lo:lo + LANES] * scale).astype(BF16)
        q_ref[:, lo + LANES:lo + QK_PAD] = (_rope_chunk(q[:, lo + LANES:lo + QK_PAD], cs) * scale).astype(BF16)

    k_rot = _rope_chunk(kr_ref[...].astype(F32), cs).astype(BF16)
    ones_col = jnp.ones((cs.shape[0], V_PAD - V_DIM), BF16)
    kv = jnp.dot(latent_norm(ckv_ref, gkv_ref), wukv_ref[...], preferred_element_type=F32)
    for h in range(N_HEADS_MLA):
        lo = h * (QK_NOPE + V_DIM)
        k_ref[:, h * QK_PAD:h * QK_PAD + LANES] = kv[:, lo:lo + QK_NOPE].astype(BF16)
        k_ref[:, h * QK_PAD + LANES:(h + 1) * QK_PAD] = k_rot
        v_ref[:, h * V_PAD:h * V_PAD + V_DIM] = kv[:, lo + QK_NOPE:lo + QK_NOPE + V_DIM].astype(BF16)
        v_ref[:, h * V_PAD + V_DIM:(h + 1) * V_PAD] = ones_col


def _mla_prep(proj, cos_sin, gq, gkv, wuq, wukv, seq, tm):
    t = proj.shape[0]
    n_seq_tiles = seq // tm
    const = lambda i: (0, 0)
    return pl.pallas_call(
        _mla_prep_kernel,
        out_shape=(jax.ShapeDtypeStruct((t, N_HEADS_MLA * QK_PAD), BF16),
                   jax.ShapeDtypeStruct((t, N_HEADS_MLA * QK_PAD), BF16),
                   jax.ShapeDtypeStruct((t, N_HEADS_MLA * V_PAD), BF16)),
        grid=(t // tm,),
        in_specs=[pl.BlockSpec((tm, Q_LORA), lambda i: (i, E_CQ // Q_LORA)),
                  pl.BlockSpec((tm, KV_LORA), lambda i: (i, E_CKV // KV_LORA)),
                  pl.BlockSpec((tm, LANES), lambda i: (i, E_KR // LANES)),
                  pl.BlockSpec((tm, LANES), lambda i: (i % n_seq_tiles, 0)),
                  pl.BlockSpec((1, Q_LORA), const),
                  pl.BlockSpec((1, KV_LORA), const),
                  pl.BlockSpec(wuq.shape, const),
                  pl.BlockSpec(wukv.shape, const)],
        out_specs=(pl.BlockSpec((tm, N_HEADS_MLA * QK_PAD), lambda i: (i, 0)),
                   pl.BlockSpec((tm, N_HEADS_MLA * QK_PAD), lambda i: (i, 0)),
                   pl.BlockSpec((tm, N_HEADS_MLA * V_PAD), lambda i: (i, 0))),
        compiler_params=_cparams(("parallel",)),
        name="mla_prep",
    )(proj, proj, proj, cos_sin, gq, gkv, wuq, wukv)


MLA_TQ = 1024
MLA_TK = 512


def _mla_attn_kernel(q_ref, k_ref, v_ref, o_ref, m_ref, acc_ref):
    n_sub = MLA_TQ // MLA_TK

    def step(sup, row_lo, kb, masked):
        start = pl.multiple_of(kb * MLA_TK, MLA_TK)
        k = k_ref[pl.ds(start, MLA_TK), :]
        v = v_ref[pl.ds(start, MLA_TK), :]
        q = q_ref[pl.ds(pl.multiple_of(sup * MLA_TQ + row_lo, MLA_TK), MLA_TQ - row_lo), :]
        s = lax.dot_general(q, k, (((1,), (1,)), ((), ())), preferred_element_type=F32)
        if masked:
            row = lax.broadcasted_iota(jnp.int32, s.shape, 0)
            col = lax.broadcasted_iota(jnp.int32, s.shape, 1)
            s = jnp.where(col <= row, s, MASK_VALUE)
        m_old = m_ref[row_lo:, :]
        m_new = jnp.maximum(m_old, jnp.max(s, axis=-1, keepdims=True))
        alpha = jnp.exp2(m_old - m_new)
        p = jnp.exp2(s - jnp.concatenate([m_new] * (MLA_TK // LANES), axis=1))
        pv = jnp.dot(p.astype(BF16), v, preferred_element_type=F32)
        acc_ref[row_lo:, :] = jnp.concatenate([alpha] * (V_PAD // LANES), axis=1) * acc_ref[row_lo:, :] + pv
        m_ref[row_lo:, :] = m_new

    def query_tile(sup, c):
        m_ref[...] = jnp.full(m_ref.shape, -jnp.inf, F32)
        acc_ref[...] = jnp.zeros(acc_ref.shape, F32)
        for b in range(n_sub):
            step(sup, b * MLA_TK, sup * n_sub + b, True)

        def body(j, carry):
            step(sup, 0, j, False)
            return carry

        lax.fori_loop(0, sup * n_sub, body, 0)
        acc = acc_ref[...]
        o_ref[pl.ds(pl.multiple_of(sup * MLA_TQ, MLA_TQ), MLA_TQ), :] = (
            acc[:, :V_DIM] / acc[:, V_DIM:]).astype(o_ref.dtype)
        return c

    lax.fori_loop(0, q_ref.shape[0] // MLA_TQ, query_tile, 0)


def _mla_attn(q, k, v, batch, seq):
    return pl.pallas_call(
        _mla_attn_kernel,
        out_shape=jax.ShapeDtypeStruct((batch * seq, MIX_A), BF16),
        grid=(batch, N_HEADS_MLA),
        in_specs=[pl.BlockSpec((seq, QK_PAD), lambda b, h: (b, h)),
                  pl.BlockSpec((seq, QK_PAD), lambda b, h: (b, h)),
                  pl.BlockSpec((seq, V_PAD), lambda b, h: (b, h))],
        out_specs=pl.BlockSpec((seq, V_DIM), lambda b, h: (b, h)),
        scratch_shapes=[pltpu.VMEM((MLA_TQ, LANES), F32), pltpu.VMEM((MLA_TQ, V_PAD), F32)],
        compiler_params=_cparams(("parallel", "parallel")),
        name="mla_attn",
    )(q, k, v)


DIL_GROUP = 4


def _dilated_kernel(q_ref, k_ref, v_ref, bias_ref, o_ref, qf_ref, kf_ref, vf_ref,
                    m_ref, l_ref, acc_ref, *, seq):
    qf_ref[...] = q_ref[...].astype(F32)
    kf_ref[...] = k_ref[...].astype(F32)
    vf_ref[...] = v_ref[...].astype(F32)

    def idx(start, n, dil):
        return pl.ds(start, n) if dil == 1 else pl.ds(start, n, stride=dil)

    def block(pat, dil, unit):
        r = lax.rem(unit, dil)
        n = lax.div(unit, dil)
        q_start = r + n * (BLOCK * dil)
        k_start = r + jnp.maximum(n - 1, 0) * (BLOCK * dil)
        if dil == 1:
            q_start = pl.multiple_of(q_start, BLOCK)
            k_start = pl.multiple_of(k_start, BLOCK)
        q = qf_ref[idx(q_start, BLOCK, dil), :].astype(BF16)
        k = kf_ref[idx(k_start, 2 * BLOCK, dil), :].astype(BF16)
        v = vf_ref[idx(k_start, 2 * BLOCK, dil), :].astype(BF16)
        bias = bias_ref[pat, 0, jnp.where(n == 0, 1, 0)]
        s = lax.dot_general(q, k, (((1,), (1,)), ((), ())), preferred_element_type=F32) + bias
        m_p = jnp.broadcast_to(jnp.max(s, axis=-1, keepdims=True), (BLOCK, LANES))
        p = jnp.exp2(s - jnp.concatenate([m_p, m_p], axis=1))
        v_ones = jnp.concatenate([v, jnp.ones((2 * BLOCK, LANES), BF16)], axis=1)
        num_l = jnp.dot(p.astype(BF16), v_ones, preferred_element_type=F32)
        return idx(q_start, BLOCK, dil), m_p, num_l[:, HEAD_DIM:], num_l[:, :HEAD_DIM]

    def merge(where, m_p, l_p, num_p, first_pattern):
        if first_pattern:
            return m_p, l_p, num_p
        m_old = m_ref[where, :]
        m_new = jnp.maximum(m_old, m_p)
        a = jnp.exp2(m_old - m_new)
        b = jnp.exp2(m_p - m_new)
        return m_new, l_ref[where, :] * a + l_p * b, acc_ref[where, :] * a + num_p * b

    for pat, (_, dil) in enumerate(DIL_PATTERNS):
        n_units = seq // BLOCK

        def group(g, carry, pat=pat, dil=dil):
            parts = [block(pat, dil, g * DIL_GROUP + u) for u in range(DIL_GROUP)]
            merged = [(part[0],) + merge(*part, first_pattern=pat == 0) for part in parts]
            for where, m_new, l_new, acc_new in merged:
                m_ref[where, :] = m_new
                l_ref[where, :] = l_new
                acc_ref[where, :] = acc_new
            return carry

        lax.fori_loop(0, n_units // DIL_GROUP, group, 0)

    o_ref[...] = (acc_ref[...] / l_ref[...]).astype(o_ref.dtype)


def _dilated_attn(proj, bias, batch, seq):
    n_pat = len(DIL_PATTERNS)
    assert all(seq // dil // BLOCK >= 2 for _, dil in DIL_PATTERNS), "every residue class needs two key blocks"
    col = lambda off: (lambda b, h: (b, off // HEAD_DIM + h))
    return pl.pallas_call(
        functools.partial(_dilated_kernel, seq=seq),
        out_shape=jax.ShapeDtypeStruct((batch * seq, MIX_B), BF16),
        grid=(batch, N_HEADS_DIL),
        in_specs=[pl.BlockSpec((seq, HEAD_DIM), col(E_QB)),
                  pl.BlockSpec((seq, HEAD_DIM), col(E_KB)),
                  pl.BlockSpec((seq, HEAD_DIM), col(E_VB)),
                  pl.BlockSpec((n_pat, 1, 2, BLOCK, 2 * BLOCK), lambda b, h: (0, h, 0, 0, 0))],
        out_specs=pl.BlockSpec((seq, HEAD_DIM), lambda b, h: (b, h)),
        scratch_shapes=[pltpu.VMEM((seq, HEAD_DIM), F32)] * 6,
        compiler_params=_cparams(("parallel", "parallel")),
        name="dilated_attn",
    )(proj, proj, proj, bias)


def _gate_out_kernel(oa_ref, ob_ref, gate_ref, x_ref, w_ref, gf_ref, o_ref, *, final_norm):
    half = oa_ref.shape[1]
    g = gate_ref[...].astype(F32)
    sg = g * jax.nn.sigmoid(g)
    mix_a = (oa_ref[...].astype(F32) * sg[:, :half]).astype(BF16)
    mix_b = (ob_ref[...].astype(F32) * sg[:, half:]).astype(BF16)
    y = jnp.dot(mix_a, w_ref[:half, :], preferred_element_type=F32)
    y = y + jnp.dot(mix_b, w_ref[half:, :], preferred_element_type=F32)
    y = x_ref[...] + y
    if final_norm:
        ms = jnp.mean(y * y, axis=-1, keepdims=True)
        y = y * lax.rsqrt(ms + EPS) * gf_ref[...]
    o_ref[...] = y


def _gate_out_proj(oa, oa_blk, ob, ob_blk, proj, gate_blk, x2d, w, gf, final_norm, tm):
    t, d = x2d.shape
    half = w.shape[0] // 2
    mix = w.shape[0]
    return pl.pallas_call(
        functools.partial(_gate_out_kernel, final_norm=final_norm),
        out_shape=jax.ShapeDtypeStruct((t, d), F32),
        grid=(t // tm,),
        in_specs=[pl.BlockSpec((tm, half), lambda i: (i, oa_blk)),
                  pl.BlockSpec((tm, half), lambda i: (i, ob_blk)),
                  pl.BlockSpec((tm, mix), lambda i: (i, gate_blk)),
                  pl.BlockSpec((tm, d), lambda i: (i, 0)),
                  pl.BlockSpec(w.shape, lambda i: (0, 0)),
                  pl.BlockSpec((1, d), lambda i: (0, 0))],
        out_specs=pl.BlockSpec((tm, d), lambda i: (i, 0)),
        compiler_params=_cparams(("parallel",)),
        name="gate_out_proj",
    )(oa, ob, proj, x2d, w, gf)


SB_TQ = 512
SB_TK = 256
SB_DEAD = -160.0


def _sb_attn_kernel(q_ref, k_ref, v_ref, u_ref, o_ref, acc_ref, carry_ref):
    n_sub = SB_TQ // SB_TK

    def step(sup, row_lo, kb, masked):
        start = pl.multiple_of(kb * SB_TK, SB_TK)
        k = k_ref[pl.ds(start, SB_TK), :]
        v = v_ref[pl.ds(start, SB_TK), :]
        q = q_ref[pl.ds(pl.multiple_of(sup * SB_TQ + row_lo, SB_TK), SB_TQ - row_lo), :]
        nz = lax.dot_general(q, k, (((1,), (1,)), ((), ())), preferred_element_type=F32)
        neg_abs = lax.bitcast_convert_type(
            lax.bitcast_convert_type(nz, jnp.uint32) | jnp.uint32(SIGN_BIT), F32)
        log_rest = jnp.minimum(nz, 0.0) - jnp.log(1.0 + jnp.exp2(neg_abs)) * LOG2E
        if masked:
            row = lax.broadcasted_iota(jnp.int32, nz.shape, 0)
            col = lax.broadcasted_iota(jnp.int32, nz.shape, 1)
            before = col < row
            log_rest = jnp.where(before, log_rest, 0.0)
        hi = log_rest.astype(BF16)
        lo = (log_rest - hi.astype(F32)).astype(BF16)
        cum = jnp.dot(jnp.concatenate([hi, lo], axis=1), u_ref[...], preferred_element_type=F32)
        carry = carry_ref[row_lo:, :]
        a = jnp.exp2(cum + carry - nz)
        if masked:
            a = jnp.where(before, a, 0.0)
        acc_ref[row_lo:, :] += jnp.dot(a.astype(BF16), v, preferred_element_type=F32)
        carry_ref[row_lo:, :] = carry + cum[:, 0:1]

    def alive():
        return jnp.max(carry_ref[...]) > SB_DEAD

    def query_tile(sup, c):
        acc_ref[...] = jnp.zeros(acc_ref.shape, F32)
        carry_ref[...] = jnp.zeros(carry_ref.shape, F32)
        for b in reversed(range(n_sub)):
            step(sup, b * SB_TK, sup * n_sub + b, True)

        def cond(state):
            jj, go = state
            return jnp.logical_and(jj < sup * n_sub, go)

        def body(state):
            jj, _ = state
            step(sup, 0, sup * n_sub - 1 - jj, False)
            return jj + 1, alive()

        lax.while_loop(cond, body, (jnp.int32(0), alive()))
        o_ref[pl.ds(pl.multiple_of(sup * SB_TQ, SB_TQ), SB_TQ), :] = acc_ref[...].astype(o_ref.dtype)
        return c

    lax.fori_loop(0, q_ref.shape[0] // SB_TQ, query_tile, 0)


def _sb_attn(proj, batch, seq):
    row = np.arange(SB_TK)[:, None]
    colv = np.arange(SB_TK)[None, :]
    tri = (row >= colv).astype(np.float32)
    u = jnp.asarray(np.concatenate([tri, tri], axis=0), BF16)
    hcol = lambda off: (lambda b, h: (b, off // HEAD_DIM + h))
    return pl.pallas_call(
        _sb_attn_kernel,
        out_shape=jax.ShapeDtypeStruct((batch * seq, MIX_ODD), BF16),
        grid=(batch, N_HEADS_SB),
        in_specs=[pl.BlockSpec((seq, HEAD_DIM), hcol(0)),
                  pl.BlockSpec((seq, HEAD_DIM), hcol(MIX_ODD)),
                  pl.BlockSpec((seq, HEAD_DIM), hcol(2 * MIX_ODD)),
                  pl.BlockSpec((2 * SB_TK, SB_TK), lambda b, h: (0, 0))],
        out_specs=pl.BlockSpec((seq, HEAD_DIM), lambda b, h: (b, h)),
        scratch_shapes=[pltpu.VMEM((SB_TQ, HEAD_DIM), F32), pltpu.VMEM((SB_TQ, 1), F32)],
        compiler_params=_cparams(("parallel", "parallel")),
        name="sb_attn",
    )(proj, proj, proj, u)


def _rotate_half_partner(w):
    half = QK_ROPE // 2
    return jnp.concatenate([-w[..., half:], w[..., :half]], axis=-1)


def _even_in_weight(w_in):
    c_q, c_kv, k_rope, q_b, k_b, v_b, gate = jnp.split(
        w_in, np.cumsum((Q_LORA, KV_LORA, QK_ROPE, MIX_B, MIX_B, MIX_B)).tolist(), axis=-1)
    pad = jnp.zeros((w_in.shape[0], E_WIDTH - E_END), w_in.dtype)
    w = jnp.concatenate([gate, q_b, k_b, v_b, c_q, c_kv, k_rope, _rotate_half_partner(k_rope), pad], axis=-1)
    return w.astype(BF16)


def _uq_weight(w_uq):
    w = w_uq.reshape(Q_LORA, N_HEADS_MLA, QK_NOPE + QK_ROPE)
    rot = w[..., QK_NOPE:]
    w = jnp.concatenate([w[..., :QK_NOPE], rot, _rotate_half_partner(rot)], axis=-1)
    return w.reshape(Q_LORA, N_HEADS_MLA * QK_PAD).astype(BF16)


def _col_scale(width, start, stop, scale):
    col = np.ones((1, width), np.float32)
    col[:, start:stop] = scale
    return jnp.asarray(col)


def _cos_sin_table(seq):
    half = QK_ROPE // 2
    inv = 1.0 / (ROPE_THETA ** (jnp.arange(half, dtype=F32) / half))
    ang = jnp.arange(seq).astype(F32)[:, None] * inv[None, :]
    cos, sin = jnp.cos(ang), jnp.sin(ang)
    return jnp.concatenate([cos, cos, sin, sin], axis=-1)


def kernel(x, norm_gain, w_in_even, q_norm_gain, kv_norm_gain, w_uq, w_ukv, w_out_even, rel_bias,
           w_in_odd, w_out_odd, final_norm_gain):
    batch, seq, d_model = x.shape
    t = batch * seq
    x2d = x.reshape(t, d_model)
    head_scale = HEAD_DIM ** -0.5

    proj0 = _rms_proj(x2d, norm_gain[0:1], _even_in_weight(w_in_even[0]),
                      _col_scale(E_WIDTH, E_QB, E_KB, head_scale * LOG2E), tm=512, tn=PROJ_TN)
    q_a, k_a, v_a = _mla_prep(proj0, _cos_sin_table(seq), q_norm_gain[0:1], kv_norm_gain[0:1],
                              _uq_weight(w_uq[0]), w_ukv[0].astype(BF16), seq, tm=512)
    o_a = _mla_attn(q_a, k_a, v_a, batch, seq)
    o_b = _dilated_attn(proj0, _bias_tiles(rel_bias), batch, seq)
    x1 = _gate_out_proj(o_a, 0, o_b, 0, proj0, E_GATE // MIX_EVEN, x2d, w_out_even[0].astype(BF16),
                        final_norm_gain[None, :], final_norm=False, tm=256)

    proj1 = _rms_proj(x1, norm_gain[1:2], w_in_odd[0].astype(BF16),
                      _col_scale(4 * MIX_ODD, 0, MIX_ODD, -head_scale * LOG2E), tm=512, tn=1024)
    o_c = _sb_attn(proj1, batch, seq)
    out = _gate_out_proj(o_c, 0, o_c, 1, proj1, 3, x1, w_out_odd[0].astype(BF16),
                         final_norm_gain[None, :], final_norm=True, tm=256)
    return out.reshape(batch, seq, d_model)
```

```python
import functools
import math

import numpy as np
import jax
import jax.numpy as jnp
from jax import lax
from jax.experimental import pallas as pl
from jax.experimental.pallas import tpu as pltpu

F32 = jnp.float32
BF16 = jnp.bfloat16

EPS = 1e-6
N_HEADS_MLA = 8
Q_LORA = 512
KV_LORA = 512
QK_NOPE = 128
QK_ROPE = 64
V_DIM = 128
ROPE_THETA = 10000.0
N_HEADS_DIL = 8
HEAD_DIM = 128
DIL_PATTERNS = ((128, 1), (512, 4), (2048, 16))
N_BUCKETS = 32
BUCKET_MAX_DIST = 2048
N_HEADS_SB = 16
BLOCK = 128
MASK_VALUE = -1e30
LOG2E = 1.4426950408889634
SIGN_BIT = 0x80000000

LANES = 128
VMEM_LIMIT = 48 * 1024 * 1024

MIX_A = N_HEADS_MLA * V_DIM
MIX_B = N_HEADS_DIL * HEAD_DIM
MIX_EVEN = MIX_A + MIX_B
MIX_ODD = N_HEADS_SB * HEAD_DIM
E_GATE = 0
E_QB = E_GATE + MIX_EVEN
E_KB = E_QB + MIX_B
E_VB = E_KB + MIX_B
E_CQ = E_VB + MIX_B
E_CKV = E_CQ + Q_LORA
E_KR = E_CKV + KV_LORA
E_END = E_KR + 2 * QK_ROPE
PROJ_TN = 1280
E_WIDTH = -(-E_END // PROJ_TN) * PROJ_TN
QK_PAD = 2 * LANES
V_PAD = 2 * LANES


def _cparams(sem):
    return pltpu.CompilerParams(dimension_semantics=sem, vmem_limit_bytes=VMEM_LIMIT)


def _bucket_tables():
    qi = np.arange(BLOCK)[:, None]
    kj = np.arange(2 * BLOCK)[None, :]
    rel = BLOCK + qi - kj
    max_exact = N_BUCKETS // 2
    tabs = []
    for window, dil in DIL_PATTERNS:
        span = window // dil
        dist = np.maximum(rel, 0) * dil
        d = np.maximum(dist.astype(np.float64), 1.0)
        frac = np.log(d / max_exact) / math.log(BUCKET_MAX_DIST / max_exact) * (N_BUCKETS - max_exact)
        large = np.minimum(max_exact + np.trunc(frac).astype(np.int64), N_BUCKETS - 1)
        bucket = np.where(dist < max_exact, dist, large)
        valid = (rel >= 0) & (rel <= span)
        tabs.append(np.where(valid, bucket, -1).astype(np.int32))
    return np.stack(tabs)


def _bias_tiles_kernel(bucket_ref, rel_bias_ref, o_ref):
    h = pl.program_id(1)
    bucket = bucket_ref[0]
    out = jnp.full(bucket.shape, MASK_VALUE, F32)
    for i in range(N_BUCKETS):
        out = jnp.where(bucket == i, rel_bias_ref[i, h] * LOG2E, out)
    o_ref[0, 0, 0] = out
    o_ref[0, 0, 1, :, :BLOCK] = out[:, BLOCK:]
    o_ref[0, 0, 1, :, BLOCK:] = jnp.full((BLOCK, BLOCK), MASK_VALUE, F32)


def _bias_tiles(rel_bias):
    n_pat = len(DIL_PATTERNS)
    buckets = jnp.asarray(_bucket_tables())
    return pl.pallas_call(
        _bias_tiles_kernel,
        out_shape=jax.ShapeDtypeStruct((n_pat, N_HEADS_DIL, 2, BLOCK, 2 * BLOCK), F32),
        grid=(n_pat, N_HEADS_DIL),
        in_specs=[pl.BlockSpec((1, BLOCK, 2 * BLOCK), lambda p, h: (p, 0, 0)),
                  pl.BlockSpec(memory_space=pltpu.SMEM)],
        out_specs=pl.BlockSpec((1, 1, 2, BLOCK, 2 * BLOCK), lambda p, h: (p, h, 0, 0, 0)),
        compiler_params=_cparams(("arbitrary", "arbitrary")),
        name="bias_tiles",
    )(buckets, rel_bias)


def _rms_proj_kernel(x_ref, g_ref, w_ref, cs_ref, o_ref, h_ref):
    @pl.when(pl.program_id(1) == 0)
    def _():
        x = x_ref[...]
        ms = jnp.mean(x * x, axis=-1, keepdims=True)
        h_ref[...] = (x * lax.rsqrt(ms + EPS) * g_ref[...]).astype(BF16)

    acc = jnp.dot(h_ref[...], w_ref[...], preferred_element_type=F32)
    o_ref[...] = (acc * cs_ref[...]).astype(o_ref.dtype)


def _rms_proj(x2d, gain, w, col_scale, tm, tn):
    t, d = x2d.shape
    n = w.shape[1]
    return pl.pallas_call(
        _rms_proj_kernel,
        out_shape=jax.ShapeDtypeStruct((t, n), BF16),
        grid=(t // tm, n // tn),
        in_specs=[pl.BlockSpec((tm, d), lambda i, j: (i, 0)),
                  pl.BlockSpec((1, d), lambda i, j: (0, 0)),
                  pl.BlockSpec((d, tn), lambda i, j: (0, j)),
                  pl.BlockSpec((1, tn), lambda i, j: (0, j))],
        out_specs=pl.BlockSpec((tm, tn), lambda i, j: (i, j)),
        scratch_shapes=[pltpu.VMEM((tm, d), BF16)],
        compiler_params=_cparams(("parallel", "arbitrary")),
        name="rms_proj",
    )(x2d, gain, w, col_scale)


def _rope_chunk(chunk, cs):
    r = chunk * cs
    r = r + pltpu.roll(r, QK_ROPE, axis=1)
    lane = lax.broadcasted_iota(jnp.int32, r.shape, 1)
    return jnp.where(lane < QK_ROPE, r, 0.0)


def _mla_prep_kernel(cq_ref, ckv_ref, kr_ref, cs_ref, gq_ref, gkv_ref, wuq_ref, wukv_ref,
                     q_ref, k_ref, v_ref):
    def latent_norm(c_ref, g_ref):
        c = c_ref[...].astype(F32)
        ms = jnp.mean(c * c, axis=-1, keepdims=True)
        return (c * lax.rsqrt(ms + EPS) * g_ref[...]).astype(BF16)

    cs = cs_ref[...]
    scale = LOG2E * (QK_NOPE + QK_ROPE) ** -0.5
    q = jnp.dot(latent_norm(cq_ref, gq_ref), wuq_ref[...], preferred_element_type=F32)
    for h in range(N_HEADS_MLA):
        lo = h * QK_PAD
        q_ref[:, lo:lo + LANES] = (q[:, lo:lo + LANES] * scale).astype(BF16)
        q_ref[:, lo + LANES:lo + QK_PAD] = (_rope_chunk(q[:, lo + LANES:lo + QK_PAD], cs) * scale).astype(BF16)

    k_rot = _rope_chunk(kr_ref[...].astype(F32), cs).astype(BF16)
    ones_col = jnp.ones((cs.shape[0], V_PAD - V_DIM), BF16)
    kv = jnp.dot(latent_norm(ckv_ref, gkv_ref), wukv_ref[...], preferred_element_type=F32)
    for h in range(N_HEADS_MLA):
        lo = h * (QK_NOPE + V_DIM)
        k_ref[:, h * QK_PAD:h * QK_PAD + LANES] = kv[:, lo:lo + QK_NOPE].astype(BF16)
        k_ref[:, h * QK_PAD + LANES:(h + 1) * QK_PAD] = k_rot
        v_ref[:, h * V_PAD:h * V_PAD + V_DIM] = kv[:, lo + QK_NOPE:lo + QK_NOPE + V_DIM].astype(BF16)
        v_ref[:, h * V_PAD + V_DIM:(h + 1) * V_PAD] = ones_col


def _mla_prep(proj, cos_sin, gq, gkv, wuq, wukv, seq, tm):
    t = proj.shape[0]
    n_seq_tiles = seq // tm
    const = lambda i: (0, 0)
    return pl.pallas_call(
        _mla_prep_kernel,
        out_shape=(jax.ShapeDtypeStruct((t, N_HEADS_MLA * QK_PAD), BF16),
                   jax.ShapeDtypeStruct((t, N_HEADS_MLA * QK_PAD), BF16),
                   jax.ShapeDtypeStruct((t, N_HEADS_MLA * V_PAD), BF16)),
        grid=(t // tm,),
        in_specs=[pl.BlockSpec((tm, Q_LORA), lambda i: (i, E_CQ // Q_LORA)),
                  pl.BlockSpec((tm, KV_LORA), lambda i: (i, E_CKV // KV_LORA)),
                  pl.BlockSpec((tm, LANES), lambda i: (i, E_KR // LANES)),
                  pl.BlockSpec((tm, LANES), lambda i: (i % n_seq_tiles, 0)),
                  pl.BlockSpec((1, Q_LORA), const),
                  pl.BlockSpec((1, KV_LORA), const),
                  pl.BlockSpec(wuq.shape, const),
                  pl.BlockSpec(wukv.shape, const)],
        out_specs=(pl.BlockSpec((tm, N_HEADS_MLA * QK_PAD), lambda i: (i, 0)),
                   pl.BlockSpec((tm, N_HEADS_MLA * QK_PAD), lambda i: (i, 0)),
                   pl.BlockSpec((tm, N_HEADS_MLA * V_PAD), lambda i: (i, 0))),
        compiler_params=_cparams(("parallel",)),
        name="mla_prep",
    )(proj, proj, proj, cos_sin, gq, gkv, wuq, wukv)


MLA_TQ = 1024
MLA_TK = 512


def _mla_attn_kernel(q_ref, k_ref, v_ref, o_ref, m_ref, acc_ref):
    n_sub = MLA_TQ // MLA_TK

    def step(sup, row_lo, kb, masked):
        start = pl.multiple_of(kb * MLA_TK, MLA_TK)
        k = k_ref[pl.ds(start, MLA_TK), :]
        v = v_ref[pl.ds(start, MLA_TK), :]
        q = q_ref[pl.ds(pl.multiple_of(sup * MLA_TQ + row_lo, MLA_TK), MLA_TQ - row_lo), :]
        s = lax.dot_general(q, k, (((1,), (1,)), ((), ())), preferred_element_type=F32)
        if masked:
            row = lax.broadcasted_iota(jnp.int32, s.shape, 0)
            col = lax.broadcasted_iota(jnp.int32, s.shape, 1)
            s = jnp.where(col <= row, s, MASK_VALUE)
        m_old = m_ref[row_lo:, :]
        m_new = jnp.maximum(m_old, jnp.max(s, axis=-1, keepdims=True))
        alpha = jnp.exp2(m_old - m_new)
        p = jnp.exp2(s - jnp.concatenate([m_new] * (MLA_TK // LANES), axis=1))
        pv = jnp.dot(p.astype(BF16), v, preferred_element_type=F32)
        acc_ref[row_lo:, :] = jnp.concatenate([alpha] * (V_PAD // LANES), axis=1) * acc_ref[row_lo:, :] + pv
        m_ref[row_lo:, :] = m_new

    def query_tile(sup, c):
        m_ref[...] = jnp.full(m_ref.shape, -jnp.inf, F32)
        acc_ref[...] = jnp.zeros(acc_ref.shape, F32)
        for b in range(n_sub):
            step(sup, b * MLA_TK, sup * n_sub + b, True)

        def body(j, carry):
            step(sup, 0, j, False)
            return carry

        lax.fori_loop(0, sup * n_sub, body, 0)
        acc = acc_ref[...]
        o_ref[pl.ds(pl.multiple_of(sup * MLA_TQ, MLA_TQ), MLA_TQ), :] = (
            acc[:, :V_DIM] / acc[:, V_DIM:]).astype(o_ref.dtype)
        return c

    lax.fori_loop(0, q_ref.shape[0] // MLA_TQ, query_tile, 0)


def _mla_attn(q, k, v, batch, seq):
    return pl.pallas_call(
        _mla_attn_kernel,
        out_shape=jax.ShapeDtypeStruct((batch * seq, MIX_A), BF16),
        grid=(batch, N_HEADS_MLA),
        in_specs=[pl.BlockSpec((seq, QK_PAD), lambda b, h: (b, h)),
                  pl.BlockSpec((seq, QK_PAD), lambda b, h: (b, h)),
                  pl.BlockSpec((seq, V_PAD), lambda b, h: (b, h))],
        out_specs=pl.BlockSpec((seq, V_DIM), lambda b, h: (b, h)),
        scratch_shapes=[pltpu.VMEM((MLA_TQ, LANES), F32), pltpu.VMEM((MLA_TQ, V_PAD), F32)],
        compiler_params=_cparams(("parallel", "parallel")),
        name="mla_attn",
    )(q, k, v)


DIL_GROUP = 4


def _dilated_kernel(q_ref, k_ref, v_ref, bias_ref, o_ref, qf_ref, kf_ref, vf_ref,
                    m_ref, l_ref, acc_ref, *, seq):
    qf_ref[...] = q_ref[...].astype(F32)
    kf_ref[...] = k_ref[...].astype(F32)
    vf_ref[...] = v_ref[...].astype(F32)

    def idx(start, n, dil):
        return pl.ds(start, n) if dil == 1 else pl.ds(start, n, stride=dil)

    def block(pat, dil, unit):
        r = lax.rem(unit, dil)
        n = lax.div(unit, dil)
        q_start = r + n * (BLOCK * dil)
        k_start = r + jnp.maximum(n - 1, 0) * (BLOCK * dil)
        if dil == 1:
            q_start = pl.multiple_of(q_start, BLOCK)
            k_start = pl.multiple_of(k_start, BLOCK)
        q = qf_ref[idx(q_start, BLOCK, dil), :].astype(BF16)
        k = kf_ref[idx(k_start, 2 * BLOCK, dil), :].astype(BF16)
        v = vf_ref[idx(k_start, 2 * BLOCK, dil), :].astype(BF16)
        bias = bias_ref[pat, 0, jnp.where(n == 0, 1, 0)]
        s = lax.dot_general(q, k, (((1,), (1,)), ((), ())), preferred_element_type=F32) + bias
        m_p = jnp.broadcast_to(jnp.max(s, axis=-1, keepdims=True), (BLOCK, LANES))
        p = jnp.exp2(s - jnp.concatenate([m_p, m_p], axis=1))
        v_ones = jnp.concatenate([v, jnp.ones((2 * BLOCK, LANES), BF16)], axis=1)
        num_l = jnp.dot(p.astype(BF16), v_ones, preferred_element_type=F32)
        return idx(q_start, BLOCK, dil), m_p, num_l[:, HEAD_DIM:], num_l[:, :HEAD_DIM]

    def merge(where, m_p, l_p, num_p, first_pattern):
        if first_pattern:
            return m_p, l_p, num_p
        m_old = m_ref[where, :]
        m_new = jnp.maximum(m_old, m_p)
        a = jnp.exp2(m_old - m_new)
        b = jnp.exp2(m_p - m_new)
        return m_new, l_ref[where, :] * a + l_p * b, acc_ref[where, :] * a + num_p * b

    for pat, (_, dil) in enumerate(DIL_PATTERNS):
        n_units = seq // BLOCK

        def group(g, carry, pat=pat, dil=dil):
            parts = [block(pat, dil, g * DIL_GROUP + u) for u in range(DIL_GROUP)]
            merged = [(part[0],) + merge(*part, first_pattern=pat == 0) for part in parts]
            for where, m_new, l_new, acc_new in merged:
                m_ref[where, :] = m_new
                l_ref[where, :] = l_new
                acc_ref[where, :] = acc_new
            return carry

        lax.fori_loop(0, n_units // DIL_GROUP, group, 0)

    o_ref[...] = (acc_ref[...] / l_ref[...]).astype(o_ref.dtype)


def _dilated_attn(proj, bias, batch, seq):
    n_pat = len(DIL_PATTERNS)
    assert all(seq // dil // BLOCK >= 2 for _, dil in DIL_PATTERNS), "every residue class needs two key blocks"
    col = lambda off: (lambda b, h: (b, off // HEAD_DIM + h))
    return pl.pallas_call(
        functools.partial(_dilated_kernel, seq=seq),
        out_shape=jax.ShapeDtypeStruct((batch * seq, MIX_B), BF16),
        grid=(batch, N_HEADS_DIL),
        in_specs=[pl.BlockSpec((seq, HEAD_DIM), col(E_QB)),
                  pl.BlockSpec((seq, HEAD_DIM), col(E_KB)),
                  pl.BlockSpec((seq, HEAD_DIM), col(E_VB)),
                  pl.BlockSpec((n_pat, 1, 2, BLOCK, 2 * BLOCK), lambda b, h: (0, h, 0, 0, 0))],
        out_specs=pl.BlockSpec((seq, HEAD_DIM), lambda b, h: (b, h)),
        scratch_shapes=[pltpu.VMEM((seq, HEAD_DIM), F32)] * 6,
        compiler_params=_cparams(("parallel", "parallel")),
        name="dilated_attn",
    )(proj, proj, proj, bias)


def _gate_out_kernel(oa_ref, ob_ref, gate_ref, x_ref, w_ref, gf_ref, o_ref, *, final_norm):
    half = oa_ref.shape[1]
    g = gate_ref[...].astype(F32)
    sg = g * jax.nn.sigmoid(g)
    mix_a = (oa_ref[...].astype(F32) * sg[:, :half]).astype(BF16)
    mix_b = (ob_ref[...].astype(F32) * sg[:, half:]).astype(BF16)
    y = jnp.dot(mix_a, w_ref[:half, :], preferred_element_type=F32)
    y = y + jnp.dot(mix_b, w_ref[half:, :], preferred_element_type=F32)
    y = x_ref[...] + y
    if final_norm:
        ms = jnp.mean(y * y, axis=-1, keepdims=True)
        y = y * lax.rsqrt(ms + EPS) * gf_ref[...]
    o_ref[...] = y


def _gate_out_proj(oa, oa_blk, ob, ob_blk, proj, gate_blk, x2d, w, gf, final_norm, tm):
    t, d = x2d.shape
    half = w.shape[0] // 2
    mix = w.shape[0]
    return pl.pallas_call(
        functools.partial(_gate_out_kernel, final_norm=final_norm),
        out_shape=jax.ShapeDtypeStruct((t, d), F32),
        grid=(t // tm,),
        in_specs=[pl.BlockSpec((tm, half), lambda i: (i, oa_blk)),
                  pl.BlockSpec((tm, half), lambda i: (i, ob_blk)),
                  pl.BlockSpec((tm, mix), lambda i: (i, gate_blk)),
                  pl.BlockSpec((tm, d), lambda i: (i, 0)),
                  pl.BlockSpec(w.shape, lambda i: (0, 0)),
                  pl.BlockSpec((1, d), lambda i: (0, 0))],
        out_specs=pl.BlockSpec((tm, d), lambda i: (i, 0)),
        compiler_params=_cparams(("parallel",)),
        name="gate_out_proj",
    )(oa, ob, proj, x2d, w, gf)


SB_TQ = 512
SB_TK = 256
SB_DEAD = -160.0


def _sb_attn_kernel(q_ref, k_ref, v_ref, u_ref, o_ref, acc_ref, carry_ref):
    n_sub = SB_TQ // SB_TK

    def step(sup, row_lo, kb, masked):
        start = pl.multiple_of(kb * SB_TK, SB_TK)
        k = k_ref[pl.ds(start, SB_TK), :]
        v = v_ref[pl.ds(start, SB_TK), :]
        q = q_ref[pl.ds(pl.multiple_of(sup * SB_TQ + row_lo, SB_TK), SB_TQ - row_lo), :]
        nz = lax.dot_general(q, k, (((1,), (1,)), ((), ())), preferred_element_type=F32)
        neg_abs = lax.bitcast_convert_type(
            lax.bitcast_convert_type(nz, jnp.uint32) | jnp.uint32(SIGN_BIT), F32)
        log_rest = jnp.minimum(nz, 0.0) - jnp.log(1.0 + jnp.exp2(neg_abs)) * LOG2E
        if masked:
            row = lax.broadcasted_iota(jnp.int32, nz.shape, 0)
            col = lax.broadcasted_iota(jnp.int32, nz.shape, 1)
            before = col < row
            log_rest = jnp.where(before, log_rest, 0.0)
        hi = log_rest.astype(BF16)
        lo = (log_rest - hi.astype(F32)).astype(BF16)
        cum = jnp.dot(jnp.concatenate([hi, lo], axis=1), u_ref[...], preferred_element_type=F32)
        carry = carry_ref[row_lo:, :]
        a = jnp.exp2(cum + carry - nz)
        if masked:
            a = jnp.where(before, a, 0.0)
        acc_ref[row_lo:, :] += jnp.dot(a.astype(BF16), v, preferred_element_type=F32)
        carry_ref[row_lo:, :] = carry + cum[:, 0:1]

    def alive():
        return jnp.max(carry_ref[...]) > SB_DEAD

    def query_tile(sup, c):
        acc_ref[...] = jnp.zeros(acc_ref.shape, F32)
        carry_ref[...] = jnp.zeros(carry_ref.shape, F32)
        for b in reversed(range(n_sub)):
            step(sup, b * SB_TK, sup * n_sub + b, True)

        def cond(state):
            jj, go = state
            return jnp.logical_and(jj < sup * n_sub, go)

        def body(state):
            jj, _ = state
            step(sup, 0, sup * n_sub - 1 - jj, False)
            return jj + 1, alive()

        lax.while_loop(cond, body, (jnp.int32(0), alive()))
        o_ref[pl.ds(pl.multiple_of(sup * SB_TQ, SB_TQ), SB_TQ), :] = acc_ref[...].astype(o_ref.dtype)
        return c

    lax.fori_loop(0, q_ref.shape[0] // SB_TQ, query_tile, 0)


def _sb_attn(proj, batch, seq):
    row = np.arange(SB_TK)[:, None]
    colv = np.arange(SB_TK)[None, :]
    tri = (row >= colv).astype(np.float32)
    u = jnp.asarray(np.concatenate([tri, tri], axis=0), BF16)
    hcol = lambda off: (lambda b, h: (b, off // HEAD_DIM + h))
    return pl.pallas_call(
        _sb_attn_kernel,
        out_shape=jax.ShapeDtypeStruct((batch * seq, MIX_ODD), BF16),
        grid=(batch, N_HEADS_SB),
        in_specs=[pl.BlockSpec((seq, HEAD_DIM), hcol(0)),
                  pl.BlockSpec((seq, HEAD_DIM), hcol(MIX_ODD)),
                  pl.BlockSpec((seq, HEAD_DIM), hcol(2 * MIX_ODD)),
                  pl.BlockSpec((2 * SB_TK, SB_TK), lambda b, h: (0, 0))],
        out_specs=pl.BlockSpec((seq, HEAD_DIM), lambda b, h: (b, h)),
        scratch_shapes=[pltpu.VMEM((SB_TQ, HEAD_DIM), F32), pltpu.VMEM((SB_TQ, 1), F32)],
        compiler_params=_cparams(("parallel", "parallel")),
        name="sb_attn",
    )(proj, proj, proj, u)


def _rotate_half_partner(w):
    half = QK_ROPE // 2
    return jnp.concatenate([-w[..., half:], w[..., :half]], axis=-1)


def _even_in_weight(w_in):
    c_q, c_kv, k_rope, q_b, k_b, v_b, gate = jnp.split(
        w_in, np.cumsum((Q_LORA, KV_LORA, QK_ROPE, MIX_B, MIX_B, MIX_B)).tolist(), axis=-1)
    pad = jnp.zeros((w_in.shape[0], E_WIDTH - E_END), w_in.dtype)
    w = jnp.concatenate([gate, q_b, k_b, v_b, c_q, c_kv, k_rope, _rotate_half_partner(k_rope), pad], axis=-1)
    return w.astype(BF16)


def _uq_weight(w_uq):
    w = w_uq.reshape(Q_LORA, N_HEADS_MLA, QK_NOPE + QK_ROPE)
    rot = w[..., QK_NOPE:]
    w = jnp.concatenate([w[..., :QK_NOPE], rot, _rotate_half_partner(rot)], axis=-1)
    return w.reshape(Q_LORA, N_HEADS_MLA * QK_PAD).astype(BF16)


def _col_scale(width, start, stop, scale):
    col = np.ones((1, width), np.float32)
    col[:, start:stop] = scale
    return jnp.asarray(col)


def _cos_sin_table(seq):
    half = QK_ROPE // 2
    inv = 1.0 / (ROPE_THETA ** (jnp.arange(half, dtype=F32) / half))
    ang = jnp.arange(seq).astype(F32)[:, None] * inv[None, :]
    cos, sin = jnp.cos(ang), jnp.sin(ang)
    return jnp.concatenate([cos, cos, sin, sin], axis=-1)


def kernel(x, norm_gain, w_in_even, q_norm_gain, kv_norm_gain, w_uq, w_ukv, w_out_even, rel_bias,
           w_in_odd, w_out_odd, final_norm_gain):
    batch, seq, d_model = x.shape
    t = batch * seq
    x2d = x.reshape(t, d_model)
    head_scale = HEAD_DIM ** -0.5

    proj0 = _rms_proj(x2d, norm_gain[0:1], _even_in_weight(w_in_even[0]),
                      _col_scale(E_WIDTH, E_QB, E_KB, head_scale * LOG2E), tm=1024, tn=PROJ_TN)
    q_a, k_a, v_a = _mla_prep(proj0, _cos_sin_table(seq), q_norm_gain[0:1], kv_norm_gain[0:1],
                              _uq_weight(w_uq[0]), w_ukv[0].astype(BF16), seq, tm=512)
    o_a = _mla_attn(q_a, k_a, v_a, batch, seq)
    o_b = _dilated_attn(proj0, _bias_tiles(rel_bias), batch, seq)
    x1 = _gate_out_proj(o_a, 0, o_b, 0, proj0, E_GATE // MIX_EVEN, x2d, w_out_even[0].astype(BF16),
                        final_norm_gain[None, :], final_norm=False, tm=512)

    proj1 = _rms_proj(x1, norm_gain[1:2], w_in_odd[0].astype(BF16),
                      _col_scale(4 * MIX_ODD, 0, MIX_ODD, -head_scale * LOG2E), tm=1024, tn=1024)
    o_c = _sb_attn(proj1, batch, seq)
    out = _gate_out_proj(o_c, 0, o_c, 1, proj1, 3, x1, w_out_odd[0].astype(BF16),
                         final_norm_gain[None, :], final_norm=True, tm=512)
    return out.reshape(batch, seq, d_model)
```

```python
import functools
import math

import numpy as np
import jax
import jax.numpy as jnp
from jax import lax
from jax.experimental import pallas as pl
from jax.experimental.pallas import tpu as pltpu

F32 = jnp.float32
BF16 = jnp.bfloat16

EPS = 1e-6
N_HEADS_MLA = 8
Q_LORA = 512
KV_LORA = 512
QK_NOPE = 128
QK_ROPE = 64
V_DIM = 128
ROPE_THETA = 10000.0
N_HEADS_DIL = 8
HEAD_DIM = 128
DIL_PATTERNS = ((128, 1), (512, 4), (2048, 16))
N_BUCKETS = 32
BUCKET_MAX_DIST = 2048
N_HEADS_SB = 16
BLOCK = 128
MASK_VALUE = -1e30
LOG2E = 1.4426950408889634
SIGN_BIT = 0x80000000

LANES = 128
VMEM_LIMIT = 48 * 1024 * 1024

MIX_A = N_HEADS_MLA * V_DIM
MIX_B = N_HEADS_DIL * HEAD_DIM
MIX_EVEN = MIX_A + MIX_B
MIX_ODD = N_HEADS_SB * HEAD_DIM
E_GATE = 0
E_QB = E_GATE + MIX_EVEN
E_KB = E_QB + MIX_B
E_VB = E_KB + MIX_B
E_CQ = E_VB + MIX_B
E_CKV = E_CQ + Q_LORA
E_KR = E_CKV + KV_LORA
E_END = E_KR + 2 * QK_ROPE
PROJ_TN = 1280
E_WIDTH = -(-E_END // PROJ_TN) * PROJ_TN
QK_PAD = 2 * LANES
V_PAD = 2 * LANES


def _cparams(sem):
    return pltpu.CompilerParams(dimension_semantics=sem, vmem_limit_bytes=VMEM_LIMIT)


def _bucket_tables():
    qi = np.arange(BLOCK)[:, None]
    kj = np.arange(2 * BLOCK)[None, :]
    rel = BLOCK + qi - kj
    max_exact = N_BUCKETS // 2
    tabs = []
    for window, dil in DIL_PATTERNS:
        span = window // dil
        dist = np.maximum(rel, 0) * dil
        d = np.maximum(dist.astype(np.float64), 1.0)
        frac = np.log(d / max_exact) / math.log(BUCKET_MAX_DIST / max_exact) * (N_BUCKETS - max_exact)
        large = np.minimum(max_exact + np.trunc(frac).astype(np.int64), N_BUCKETS - 1)
        bucket = np.where(dist < max_exact, dist, large)
        valid = (rel >= 0) & (rel <= span)
        tabs.append(np.where(valid, bucket, -1).astype(np.int32))
    return np.stack(tabs)


def _bias_tiles_kernel(bucket_ref, rel_bias_ref, o_ref):
    h = pl.program_id(1)
    bucket = bucket_ref[0]
    out = jnp.full(bucket.shape, MASK_VALUE, F32)
    for i in range(N_BUCKETS):
        out = jnp.where(bucket == i, rel_bias_ref[i, h] * LOG2E, out)
    o_ref[0, 0, 0] = out
    o_ref[0, 0, 1, :, :BLOCK] = out[:, BLOCK:]
    o_ref[0, 0, 1, :, BLOCK:] = jnp.full((BLOCK, BLOCK), MASK_VALUE, F32)


def _bias_tiles(rel_bias):
    n_pat = len(DIL_PATTERNS)
    buckets = jnp.asarray(_bucket_tables())
    return pl.pallas_call(
        _bias_tiles_kernel,
        out_shape=jax.ShapeDtypeStruct((n_pat, N_HEADS_DIL, 2, BLOCK, 2 * BLOCK), F32),
        grid=(n_pat, N_HEADS_DIL),
        in_specs=[pl.BlockSpec((1, BLOCK, 2 * BLOCK), lambda p, h: (p, 0, 0)),
                  pl.BlockSpec(memory_space=pltpu.SMEM)],
        out_specs=pl.BlockSpec((1, 1, 2, BLOCK, 2 * BLOCK), lambda p, h: (p, h, 0, 0, 0)),
        compiler_params=_cparams(("arbitrary", "arbitrary")),
        name="bias_tiles",
    )(buckets, rel_bias)


def _rms_proj_kernel(x_ref, g_ref, w_ref, cs_ref, o_ref, h_ref):
    @pl.when(pl.program_id(1) == 0)
    def _():
        x = x_ref[...]
        ms = jnp.mean(x * x, axis=-1, keepdims=True)
        h_ref[...] = (x * lax.rsqrt(ms + EPS) * g_ref[...]).astype(BF16)

    acc = jnp.dot(h_ref[...], w_ref[...], preferred_element_type=F32)
    o_ref[...] = (acc * cs_ref[...]).astype(o_ref.dtype)


def _rms_proj(x2d, gain, w, col_scale, tm, tn):
    t, d = x2d.shape
    n = w.shape[1]
    return pl.pallas_call(
        _rms_proj_kernel,
        out_shape=jax.ShapeDtypeStruct((t, n), BF16),
        grid=(t // tm, n // tn),
        in_specs=[pl.BlockSpec((tm, d), lambda i, j: (i, 0)),
                  pl.BlockSpec((1, d), lambda i, j: (0, 0)),
                  pl.BlockSpec((d, tn), lambda i, j: (0, j)),
                  pl.BlockSpec((1, tn), lambda i, j: (0, j))],
        out_specs=pl.BlockSpec((tm, tn), lambda i, j: (i, j)),
        scratch_shapes=[pltpu.VMEM((tm, d), BF16)],
        compiler_params=_cparams(("parallel", "arbitrary")),
        name="rms_proj",
    )(x2d, gain, w, col_scale)


def _rope_chunk(chunk, cs):
    r = chunk * cs
    r = r + pltpu.roll(r, QK_ROPE, axis=1)
    lane = lax.broadcasted_iota(jnp.int32, r.shape, 1)
    return jnp.where(lane < QK_ROPE, r, 0.0)


def _mla_prep_kernel(cq_ref, ckv_ref, kr_ref, cs_ref, gq_ref, gkv_ref, wuq_ref, wukv_ref,
                     q_ref, k_ref, v_ref):
    def latent_norm(c_ref, g_ref):
        c = c_ref[...].astype(F32)
        ms = jnp.mean(c * c, axis=-1, keepdims=True)
        return (c * lax.rsqrt(ms + EPS) * g_ref[...]).astype(BF16)

    cs = cs_ref[...]
    scale = LOG2E * (QK_NOPE + QK_ROPE) ** -0.5
    q = jnp.dot(latent_norm(cq_ref, gq_ref), wuq_ref[...], preferred_element_type=F32)
    for h in range(N_HEADS_MLA):
        lo = h * QK_PAD
        q_ref[:, lo:lo + LANES] = (q[:, lo:lo + LANES] * scale).astype(BF16)
        q_ref[:, lo + LANES:lo + QK_PAD] = (_rope_chunk(q[:, lo + LANES:lo + QK_PAD], cs) * scale).astype(BF16)

    k_rot = _rope_chunk(kr_ref[...].astype(F32), cs).astype(BF16)
    ones_col = jnp.ones((cs.shape[0], V_PAD - V_DIM), BF16)
    kv = jnp.dot(latent_norm(ckv_ref, gkv_ref), wukv_ref[...], preferred_element_type=F32)
    for h in range(N_HEADS_MLA):
        lo = h * (QK_NOPE + V_DIM)
        k_ref[:, h * QK_PAD:h * QK_PAD + LANES] = kv[:, lo:lo + QK_NOPE].astype(BF16)
        k_ref[:, h * QK_PAD + LANES:(h + 1) * QK_PAD] = k_rot
        v_ref[:, h * V_PAD:h * V_PAD + V_DIM] = kv[:, lo + QK_NOPE:lo + QK_NOPE + V_DIM].astype(BF16)
        v_ref[:, h * V_PAD + V_DIM:(h + 1) * V_PAD] = ones_col


def _mla_prep(proj, cos_sin, gq, gkv, wuq, wukv, seq, tm):
    t = proj.shape[0]
    n_seq_tiles = seq // tm
    const = lambda i: (0, 0)
    return pl.pallas_call(
        _mla_prep_kernel,
        out_shape=(jax.ShapeDtypeStruct((t, N_HEADS_MLA * QK_PAD), BF16),
                   jax.ShapeDtypeStruct((t, N_HEADS_MLA * QK_PAD), BF16),
                   jax.ShapeDtypeStruct((t, N_HEADS_MLA * V_PAD), BF16)),
        grid=(t // tm,),
        in_specs=[pl.BlockSpec((tm, Q_LORA), lambda i: (i, E_CQ // Q_LORA)),
                  pl.BlockSpec((tm, KV_LORA), lambda i: (i, E_CKV // KV_LORA)),
                  pl.BlockSpec((tm, LANES), lambda i: (i, E_KR // LANES)),
                  pl.BlockSpec((tm, LANES), lambda i: (i % n_seq_tiles, 0)),
                  pl.BlockSpec((1, Q_LORA), const),
                  pl.BlockSpec((1, KV_LORA), const),
                  pl.BlockSpec(wuq.shape, const),
                  pl.BlockSpec(wukv.shape, const)],
        out_specs=(pl.BlockSpec((tm, N_HEADS_MLA * QK_PAD), lambda i: (i, 0)),
                   pl.BlockSpec((tm, N_HEADS_MLA * QK_PAD), lambda i: (i, 0)),
                   pl.BlockSpec((tm, N_HEADS_MLA * V_PAD), lambda i: (i, 0))),
        compiler_params=_cparams(("parallel",)),
        name="mla_prep",
    )(proj, proj, proj, cos_sin, gq, gkv, wuq, wukv)


MLA_TQ = 1024
MLA_TK = 512


def _mla_attn_kernel(q_ref, k_ref, v_ref, o_ref, m_ref, acc_ref):
    n_sub = MLA_TQ // MLA_TK

    def step(sup, row_lo, kb, masked):
        start = pl.multiple_of(kb * MLA_TK, MLA_TK)
        k = k_ref[pl.ds(start, MLA_TK), :]
        v = v_ref[pl.ds(start, MLA_TK), :]
        q = q_ref[pl.ds(pl.multiple_of(sup * MLA_TQ + row_lo, MLA_TK), MLA_TQ - row_lo), :]
        s = lax.dot_general(q, k, (((1,), (1,)), ((), ())), preferred_element_type=F32)
        if masked:
            row = lax.broadcasted_iota(jnp.int32, s.shape, 0)
            col = lax.broadcasted_iota(jnp.int32, s.shape, 1)
            s = jnp.where(col <= row, s, MASK_VALUE)
        m_old = m_ref[row_lo:, :]
        m_new = jnp.maximum(m_old, jnp.max(s, axis=-1, keepdims=True))
        alpha = jnp.exp2(m_old - m_new)
        p = jnp.exp2(s - jnp.concatenate([m_new] * (MLA_TK // LANES), axis=1))
        pv = jnp.dot(p.astype(BF16), v, preferred_element_type=F32)
        acc_ref[row_lo:, :] = jnp.concatenate([alpha] * (V_PAD // LANES), axis=1) * acc_ref[row_lo:, :] + pv
        m_ref[row_lo:, :] = m_new

    def query_tile(sup, c):
        m_ref[...] = jnp.full(m_ref.shape, -jnp.inf, F32)
        acc_ref[...] = jnp.zeros(acc_ref.shape, F32)
        for b in range(n_sub):
            step(sup, b * MLA_TK, sup * n_sub + b, True)

        def body(j, carry):
            step(sup, 0, j, False)
            return carry

        lax.fori_loop(0, sup * n_sub, body, 0)
        acc = acc_ref[...]
        o_ref[pl.ds(pl.multiple_of(sup * MLA_TQ, MLA_TQ), MLA_TQ), :] = (
            acc[:, :V_DIM] / acc[:, V_DIM:]).astype(o_ref.dtype)
        return c

    lax.fori_loop(0, q_ref.shape[0] // MLA_TQ, query_tile, 0)


def _mla_attn(q, k, v, batch, seq):
    return pl.pallas_call(
        _mla_attn_kernel,
        out_shape=jax.ShapeDtypeStruct((batch * seq, MIX_A), BF16),
        grid=(batch, N_HEADS_MLA),
        in_specs=[pl.BlockSpec((seq, QK_PAD), lambda b, h: (b, h)),
                  pl.BlockSpec((seq, QK_PAD), lambda b, h: (b, h)),
                  pl.BlockSpec((seq, V_PAD), lambda b, h: (b, h))],
        out_specs=pl.BlockSpec((seq, V_DIM), lambda b, h: (b, h)),
        scratch_shapes=[pltpu.VMEM((MLA_TQ, LANES), F32), pltpu.VMEM((MLA_TQ, V_PAD), F32)],
        compiler_params=_cparams(("parallel", "parallel")),
        name="mla_attn",
    )(q, k, v)


DIL_GROUP = 4


def _dilated_kernel(q_ref, k_ref, v_ref, bias_ref, o_ref, qf_ref, kf_ref, vf_ref,
                    m_ref, l_ref, acc_ref, *, seq):
    qf_ref[...] = q_ref[...].astype(F32)
    kf_ref[...] = k_ref[...].astype(F32)
    vf_ref[...] = v_ref[...].astype(F32)

    def idx(start, n, dil):
        return pl.ds(start, n) if dil == 1 else pl.ds(start, n, stride=dil)

    def block(pat, dil, unit):
        r = lax.rem(unit, dil)
        n = lax.div(unit, dil)
        q_start = r + n * (BLOCK * dil)
        k_start = r + jnp.maximum(n - 1, 0) * (BLOCK * dil)
        if dil == 1:
            q_start = pl.multiple_of(q_start, BLOCK)
            k_start = pl.multiple_of(k_start, BLOCK)
        q = qf_ref[idx(q_start, BLOCK, dil), :].astype(BF16)
        k = kf_ref[idx(k_start, 2 * BLOCK, dil), :].astype(BF16)
        v = vf_ref[idx(k_start, 2 * BLOCK, dil), :].astype(BF16)
        bias = bias_ref[pat, 0, jnp.where(n == 0, 1, 0)]
        s = lax.dot_general(q, k, (((1,), (1,)), ((), ())), preferred_element_type=F32) + bias
        m_p = jnp.broadcast_to(jnp.max(s, axis=-1, keepdims=True), (BLOCK, LANES))
        p = jnp.exp2(s - jnp.concatenate([m_p, m_p], axis=1))
        v_ones = jnp.concatenate([v, jnp.ones((2 * BLOCK, LANES), BF16)], axis=1)
        num_l = jnp.dot(p.astype(BF16), v_ones, preferred_element_type=F32)
        return idx(q_start, BLOCK, dil), m_p, num_l[:, HEAD_DIM:], num_l[:, :HEAD_DIM]

    def merge(where, m_p, l_p, num_p, first_pattern):
        if first_pattern:
            return m_p, l_p, num_p
        m_old = m_ref[where, :]
        m_new = jnp.maximum(m_old, m_p)
        a = jnp.exp2(m_old - m_new)
        b = jnp.exp2(m_p - m_new)
        return m_new, l_ref[where, :] * a + l_p * b, acc_ref[where, :] * a + num_p * b

    for pat, (_, dil) in enumerate(DIL_PATTERNS):
        n_units = seq // BLOCK

        def group(g, carry, pat=pat, dil=dil):
            parts = [block(pat, dil, g * DIL_GROUP + u) for u in range(DIL_GROUP)]
            merged = [(part[0],) + merge(*part, first_pattern=pat == 0) for part in parts]
            for where, m_new, l_new, acc_new in merged:
                m_ref[where, :] = m_new
                l_ref[where, :] = l_new
                acc_ref[where, :] = acc_new
            return carry

        lax.fori_loop(0, n_units // DIL_GROUP, group, 0)

    o_ref[...] = (acc_ref[...] / l_ref[...]).astype(o_ref.dtype)


def _dilated_attn(proj, bias, batch, seq):
    n_pat = len(DIL_PATTERNS)
    assert all(seq // dil // BLOCK >= 2 for _, dil in DIL_PATTERNS), "every residue class needs two key blocks"
    col = lambda off: (lambda b, h: (b, off // HEAD_DIM + h))
    return pl.pallas_call(
        functools.partial(_dilated_kernel, seq=seq),
        out_shape=jax.ShapeDtypeStruct((batch * seq, MIX_B), BF16),
        grid=(batch, N_HEADS_DIL),
        in_specs=[pl.BlockSpec((seq, HEAD_DIM), col(E_QB)),
                  pl.BlockSpec((seq, HEAD_DIM), col(E_KB)),
                  pl.BlockSpec((seq, HEAD_DIM), col(E_VB)),
                  pl.BlockSpec((n_pat, 1, 2, BLOCK, 2 * BLOCK), lambda b, h: (0, h, 0, 0, 0))],
        out_specs=pl.BlockSpec((seq, HEAD_DIM), lambda b, h: (b, h)),
        scratch_shapes=[pltpu.VMEM((seq, HEAD_DIM), F32)] * 6,
        compiler_params=_cparams(("parallel", "parallel")),
        name="dilated_attn",
    )(proj, proj, proj, bias)


def _gate_out_kernel(oa_ref, ob_ref, gate_ref, x_ref, w_ref, gf_ref, o_ref, *, final_norm):
    half = oa_ref.shape[1]
    g = gate_ref[...].astype(F32)
    sg = g * jax.nn.sigmoid(g)
    mix_a = (oa_ref[...].astype(F32) * sg[:, :half]).astype(BF16)
    mix_b = (ob_ref[...].astype(F32) * sg[:, half:]).astype(BF16)
    y = jnp.dot(mix_a, w_ref[:half, :], preferred_element_type=F32)
    y = y + jnp.dot(mix_b, w_ref[half:, :], preferred_element_type=F32)
    y = x_ref[...] + y
    if final_norm:
        ms = jnp.mean(y * y, axis=-1, keepdims=True)
        y = y * lax.rsqrt(ms + EPS) * gf_ref[...]
    o_ref[...] = y


def _gate_out_proj(oa, oa_blk, ob, ob_blk, proj, gate_blk, x2d, w, gf, final_norm, tm):
    t, d = x2d.shape
    half = w.shape[0] // 2
    mix = w.shape[0]
    return pl.pallas_call(
        functools.partial(_gate_out_kernel, final_norm=final_norm),
        out_shape=jax.ShapeDtypeStruct((t, d), F32),
        grid=(t // tm,),
        in_specs=[pl.BlockSpec((tm, half), lambda i: (i, oa_blk)),
                  pl.BlockSpec((tm, half), lambda i: (i, ob_blk)),
                  pl.BlockSpec((tm, mix), lambda i: (i, gate_blk)),
                  pl.BlockSpec((tm, d), lambda i: (i, 0)),
                  pl.BlockSpec(w.shape, lambda i: (0, 0)),
                  pl.BlockSpec((1, d), lambda i: (0, 0))],
        out_specs=pl.BlockSpec((tm, d), lambda i: (i, 0)),
        compiler_params=_cparams(("parallel",)),
        name="gate_out_proj",
    )(oa, ob, proj, x2d, w, gf)


SB_TK = 256
SB_CHAINS = 4
SB_TQ = SB_CHAINS * SB_TK
SB_DEAD = -160.0


def _sb_attn_kernel(q_ref, k_ref, v_ref, u_ref, o_ref, acc_ref, carry_ref):
    def step(base, tiles):
        n = len(tiles)
        rows = [slice(c * SB_TK, (c + 1) * SB_TK) for c, _, _ in tiles]
        kv_at = [pl.ds(pl.multiple_of(kb * SB_TK, SB_TK), SB_TK) for _, kb, _ in tiles]
        nz = []
        for i, (c, _, _) in enumerate(tiles):
            q = q_ref[pl.ds(pl.multiple_of((base + c) * SB_TK, SB_TK), SB_TK), :]
            nz.append(lax.dot_general(q, k_ref[kv_at[i], :], (((1,), (1,)), ((), ())),
                                      preferred_element_type=F32))
        row = lax.broadcasted_iota(jnp.int32, (SB_TK, SB_TK), 0)
        col = lax.broadcasted_iota(jnp.int32, (SB_TK, SB_TK), 1)
        before = col < row
        hi_lo = []
        for i, (_, _, masked) in enumerate(tiles):
            neg_abs = lax.bitcast_convert_type(
                lax.bitcast_convert_type(nz[i], jnp.uint32) | jnp.uint32(SIGN_BIT), F32)
            log_rest = jnp.minimum(nz[i], 0.0) - jnp.log(1.0 + jnp.exp2(neg_abs)) * LOG2E
            if masked:
                log_rest = jnp.where(before, log_rest, 0.0)
            hi = log_rest.astype(BF16)
            lo = (log_rest - hi.astype(F32)).astype(BF16)
            hi_lo.append(jnp.concatenate([hi, lo], axis=1))
        cum_all = jnp.dot(jnp.concatenate(hi_lo, axis=0), u_ref[...], preferred_element_type=F32)
        weights, carries = [], []
        for i, (_, _, masked) in enumerate(tiles):
            cum = cum_all[i * SB_TK:(i + 1) * SB_TK]
            carry = carry_ref[rows[i], :]
            a = jnp.exp2(cum + carry - nz[i])
            if masked:
                a = jnp.where(before, a, 0.0)
            weights.append(a.astype(BF16))
            carries.append(carry + cum[:, 0:1])
        accs = [acc_ref[rows[i], :] + jnp.dot(weights[i], v_ref[kv_at[i], :], preferred_element_type=F32)
                for i in range(n)]
        for i in range(n):
            acc_ref[rows[i], :] = accs[i]
            carry_ref[rows[i], :] = carries[i]

    def alive():
        return jnp.max(carry_ref[...]) > SB_DEAD

    def group(g, carry_unused):
        base = g * SB_CHAINS
        acc_ref[...] = jnp.zeros(acc_ref.shape, F32)
        carry_ref[...] = jnp.zeros(carry_ref.shape, F32)
        step(base, [(c, base + c, True) for c in range(SB_CHAINS)])

        def cond(state):
            t, go = state
            return jnp.logical_and(t <= base, go)

        def body(state):
            t, _ = state
            step(base, [(c, base + c - t, False) for c in range(SB_CHAINS)])
            return t + 1, alive()

        lax.while_loop(cond, body, (jnp.int32(1), alive()))
        for j in range(1, SB_CHAINS):
            @pl.when(alive())
            def _():
                step(base, [(c, c - j, False) for c in range(j, SB_CHAINS)])

        o_ref[pl.ds(pl.multiple_of(base * SB_TK, SB_TQ), SB_TQ), :] = acc_ref[...].astype(o_ref.dtype)
        return carry_unused

    lax.fori_loop(0, q_ref.shape[0] // SB_TQ, group, 0)


def _sb_attn(proj, batch, seq):
    row = np.arange(SB_TK)[:, None]
    colv = np.arange(SB_TK)[None, :]
    tri = (row >= colv).astype(np.float32)
    u = jnp.asarray(np.concatenate([tri, tri], axis=0), BF16)
    hcol = lambda off: (lambda b, h: (b, off // HEAD_DIM + h))
    return pl.pallas_call(
        _sb_attn_kernel,
        out_shape=jax.ShapeDtypeStruct((batch * seq, MIX_ODD), BF16),
        grid=(batch, N_HEADS_SB),
        in_specs=[pl.BlockSpec((seq, HEAD_DIM), hcol(0)),
                  pl.BlockSpec((seq, HEAD_DIM), hcol(MIX_ODD)),
                  pl.BlockSpec((seq, HEAD_DIM), hcol(2 * MIX_ODD)),
                  pl.BlockSpec((2 * SB_TK, SB_TK), lambda b, h: (0, 0))],
        out_specs=pl.BlockSpec((seq, HEAD_DIM), lambda b, h: (b, h)),
        scratch_shapes=[pltpu.VMEM((SB_TQ, HEAD_DIM), F32), pltpu.VMEM((SB_TQ, 1), F32)],
        compiler_params=_cparams(("parallel", "parallel")),
        name="sb_attn",
    )(proj, proj, proj, u)


def _rotate_half_partner(w):
    half = QK_ROPE // 2
    return jnp.concatenate([-w[..., half:], w[..., :half]], axis=-1)


def _even_in_weight(w_in):
    c_q, c_kv, k_rope, q_b, k_b, v_b, gate = jnp.split(
        w_in, np.cumsum((Q_LORA, KV_LORA, QK_ROPE, MIX_B, MIX_B, MIX_B)).tolist(), axis=-1)
    pad = jnp.zeros((w_in.shape[0], E_WIDTH - E_END), w_in.dtype)
    w = jnp.concatenate([gate, q_b, k_b, v_b, c_q, c_kv, k_rope, _rotate_half_partner(k_rope), pad], axis=-1)
    return w.astype(BF16)


def _uq_weight(w_uq):
    w = w_uq.reshape(Q_LORA, N_HEADS_MLA, QK_NOPE + QK_ROPE)
    rot = w[..., QK_NOPE:]
    w = jnp.concatenate([w[..., :QK_NOPE], rot, _rotate_half_partner(rot)], axis=-1)
    return w.reshape(Q_LORA, N_HEADS_MLA * QK_PAD).astype(BF16)


def _col_scale(width, start, stop, scale):
    col = np.ones((1, width), np.float32)
    col[:, start:stop] = scale
    return jnp.asarray(col)


def _cos_sin_table(seq):
    half = QK_ROPE // 2
    inv = 1.0 / (ROPE_THETA ** (jnp.arange(half, dtype=F32) / half))
    ang = jnp.arange(seq).astype(F32)[:, None] * inv[None, :]
    cos, sin = jnp.cos(ang), jnp.sin(ang)
    return jnp.concatenate([cos, cos, sin, sin], axis=-1)


def kernel(x, norm_gain, w_in_even, q_norm_gain, kv_norm_gain, w_uq, w_ukv, w_out_even, rel_bias,
           w_in_odd, w_out_odd, final_norm_gain):
    batch, seq, d_model = x.shape
    t = batch * seq
    x2d = x.reshape(t, d_model)
    head_scale = HEAD_DIM ** -0.5

    proj0 = _rms_proj(x2d, norm_gain[0:1], _even_in_weight(w_in_even[0]),
                      _col_scale(E_WIDTH, E_QB, E_KB, head_scale * LOG2E), tm=1024, tn=PROJ_TN)
    q_a, k_a, v_a = _mla_prep(proj0, _cos_sin_table(seq), q_norm_gain[0:1], kv_norm_gain[0:1],
                              _uq_weight(w_uq[0]), w_ukv[0].astype(BF16), seq, tm=512)
    o_a = _mla_attn(q_a, k_a, v_a, batch, seq)
    o_b = _dilated_attn(proj0, _bias_tiles(rel_bias), batch, seq)
    x1 = _gate_out_proj(o_a, 0, o_b, 0, proj0, E_GATE // MIX_EVEN, x2d, w_out_even[0].astype(BF16),
                        final_norm_gain[None, :], final_norm=False, tm=512)

    proj1 = _rms_proj(x1, norm_gain[1:2], w_in_odd[0].astype(BF16),
                      _col_scale(4 * MIX_ODD, 0, MIX_ODD, -head_scale * LOG2E), tm=1024, tn=1024)
    o_c = _sb_attn(proj1, batch, seq)
    out = _gate_out_proj(o_c, 0, o_c, 1, proj1, 3, x1, w_out_odd[0].astype(BF16),
                         final_norm_gain[None, :], final_norm=True, tm=512)
    return out.reshape(batch, seq, d_model)
```

```python
import functools
import math

import numpy as np
import jax
import jax.numpy as jnp
from jax import lax
from jax.experimental import pallas as pl
from jax.experimental.pallas import tpu as pltpu

F32 = jnp.float32
BF16 = jnp.bfloat16

EPS = 1e-6
N_HEADS_MLA = 8
Q_LORA = 512
KV_LORA = 512
QK_NOPE = 128
QK_ROPE = 64
V_DIM = 128
ROPE_THETA = 10000.0
N_HEADS_DIL = 8
HEAD_DIM = 128
DIL_PATTERNS = ((128, 1), (512, 4), (2048, 16))
N_BUCKETS = 32
BUCKET_MAX_DIST = 2048
N_HEADS_SB = 16
BLOCK = 128
MASK_VALUE = -1e30
LOG2E = 1.4426950408889634
SIGN_BIT = 0x80000000

LANES = 128
VMEM_LIMIT = 48 * 1024 * 1024

MIX_A = N_HEADS_MLA * V_DIM
MIX_B = N_HEADS_DIL * HEAD_DIM
MIX_EVEN = MIX_A + MIX_B
MIX_ODD = N_HEADS_SB * HEAD_DIM
E_GATE = 0
E_QB = E_GATE + MIX_EVEN
E_KB = E_QB + MIX_B
E_VB = E_KB + MIX_B
E_CQ = E_VB + MIX_B
E_CKV = E_CQ + Q_LORA
E_KR = E_CKV + KV_LORA
E_END = E_KR + 2 * QK_ROPE
PROJ_TN = 1280
E_WIDTH = -(-E_END // PROJ_TN) * PROJ_TN
QK_PAD = 2 * LANES
V_PAD = 2 * LANES


def _cparams(sem):
    return pltpu.CompilerParams(dimension_semantics=sem, vmem_limit_bytes=VMEM_LIMIT)


def _bucket_tables():
    qi = np.arange(BLOCK)[:, None]
    kj = np.arange(2 * BLOCK)[None, :]
    rel = BLOCK + qi - kj
    max_exact = N_BUCKETS // 2
    tabs = []
    for window, dil in DIL_PATTERNS:
        span = window // dil
        dist = np.maximum(rel, 0) * dil
        d = np.maximum(dist.astype(np.float64), 1.0)
        frac = np.log(d / max_exact) / math.log(BUCKET_MAX_DIST / max_exact) * (N_BUCKETS - max_exact)
        large = np.minimum(max_exact + np.trunc(frac).astype(np.int64), N_BUCKETS - 1)
        bucket = np.where(dist < max_exact, dist, large)
        valid = (rel >= 0) & (rel <= span)
        tabs.append(np.where(valid, bucket, -1).astype(np.int32))
    return np.stack(tabs)


def _bias_tiles_kernel(bucket_ref, rel_bias_ref, o_ref):
    h = pl.program_id(1)
    bucket = bucket_ref[0]
    out = jnp.full(bucket.shape, MASK_VALUE, F32)
    for i in range(N_BUCKETS):
        out = jnp.where(bucket == i, rel_bias_ref[i, h] * LOG2E, out)
    o_ref[0, 0, 0] = out
    o_ref[0, 0, 1, :, :BLOCK] = out[:, BLOCK:]
    o_ref[0, 0, 1, :, BLOCK:] = jnp.full((BLOCK, BLOCK), MASK_VALUE, F32)


def _bias_tiles(rel_bias):
    n_pat = len(DIL_PATTERNS)
    buckets = jnp.asarray(_bucket_tables())
    return pl.pallas_call(
        _bias_tiles_kernel,
        out_shape=jax.ShapeDtypeStruct((n_pat, N_HEADS_DIL, 2, BLOCK, 2 * BLOCK), F32),
        grid=(n_pat, N_HEADS_DIL),
        in_specs=[pl.BlockSpec((1, BLOCK, 2 * BLOCK), lambda p, h: (p, 0, 0)),
                  pl.BlockSpec(memory_space=pltpu.SMEM)],
        out_specs=pl.BlockSpec((1, 1, 2, BLOCK, 2 * BLOCK), lambda p, h: (p, h, 0, 0, 0)),
        compiler_params=_cparams(("arbitrary", "arbitrary")),
        name="bias_tiles",
    )(buckets, rel_bias)


def _rms_proj_kernel(x_ref, g_ref, w_ref, cs_ref, o_ref, h_ref):
    @pl.when(pl.program_id(1) == 0)
    def _():
        x = x_ref[...]
        ms = jnp.mean(x * x, axis=-1, keepdims=True)
        h_ref[...] = (x * lax.rsqrt(ms + EPS) * g_ref[...]).astype(BF16)

    acc = jnp.dot(h_ref[...], w_ref[...], preferred_element_type=F32)
    o_ref[...] = (acc * cs_ref[...]).astype(o_ref.dtype)


def _rms_proj(x2d, gain, w, col_scale, tm, tn):
    t, d = x2d.shape
    n = w.shape[1]
    return pl.pallas_call(
        _rms_proj_kernel,
        out_shape=jax.ShapeDtypeStruct((t, n), BF16),
        grid=(t // tm, n // tn),
        in_specs=[pl.BlockSpec((tm, d), lambda i, j: (i, 0)),
                  pl.BlockSpec((1, d), lambda i, j: (0, 0)),
                  pl.BlockSpec((d, tn), lambda i, j: (0, j)),
                  pl.BlockSpec((1, tn), lambda i, j: (0, j))],
        out_specs=pl.BlockSpec((tm, tn), lambda i, j: (i, j)),
        scratch_shapes=[pltpu.VMEM((tm, d), BF16)],
        compiler_params=_cparams(("parallel", "arbitrary")),
        name="rms_proj",
    )(x2d, gain, w, col_scale)


def _rope_chunk(chunk, cs):
    r = chunk * cs
    r = r + pltpu.roll(r, QK_ROPE, axis=1)
    lane = lax.broadcasted_iota(jnp.int32, r.shape, 1)
    return jnp.where(lane < QK_ROPE, r, 0.0)


def _mla_prep_kernel(cq_ref, ckv_ref, kr_ref, cs_ref, gq_ref, gkv_ref, wuq_ref, wukv_ref,
                     q_ref, k_ref, v_ref):
    def latent_norm(c_ref, g_ref):
        c = c_ref[...].astype(F32)
        ms = jnp.mean(c * c, axis=-1, keepdims=True)
        return (c * lax.rsqrt(ms + EPS) * g_ref[...]).astype(BF16)

    cs = cs_ref[...]
    scale = LOG2E * (QK_NOPE + QK_ROPE) ** -0.5
    q = jnp.dot(latent_norm(cq_ref, gq_ref), wuq_ref[...], preferred_element_type=F32)
    for h in range(N_HEADS_MLA):
        lo = h * QK_PAD
        q_ref[:, lo:lo + LANES] = (q[:, lo:lo + LANES] * scale).astype(BF16)
        q_ref[:, lo + LANES:lo + QK_PAD] = (_rope_chunk(q[:, lo + LANES:lo + QK_PAD], cs) * scale).astype(BF16)

    k_rot = _rope_chunk(kr_ref[...].astype(F32), cs).astype(BF16)
    ones_col = jnp.ones((cs.shape[0], V_PAD - V_DIM), BF16)
    kv = jnp.dot(latent_norm(ckv_ref, gkv_ref), wukv_ref[...], preferred_element_type=F32)
    for h in range(N_HEADS_MLA):
        lo = h * (QK_NOPE + V_DIM)
        k_ref[:, h * QK_PAD:h * QK_PAD + LANES] = kv[:, lo:lo + QK_NOPE].astype(BF16)
        k_ref[:, h * QK_PAD + LANES:(h + 1) * QK_PAD] = k_rot
        v_ref[:, h * V_PAD:h * V_PAD + V_DIM] = kv[:, lo + QK_NOPE:lo + QK_NOPE + V_DIM].astype(BF16)
        v_ref[:, h * V_PAD + V_DIM:(h + 1) * V_PAD] = ones_col


def _mla_prep(proj, cos_sin, gq, gkv, wuq, wukv, seq, tm):
    t = proj.shape[0]
    n_seq_tiles = seq // tm
    const = lambda i: (0, 0)
    return pl.pallas_call(
        _mla_prep_kernel,
        out_shape=(jax.ShapeDtypeStruct((t, N_HEADS_MLA * QK_PAD), BF16),
                   jax.ShapeDtypeStruct((t, N_HEADS_MLA * QK_PAD), BF16),
                   jax.ShapeDtypeStruct((t, N_HEADS_MLA * V_PAD), BF16)),
        grid=(t // tm,),
        in_specs=[pl.BlockSpec((tm, Q_LORA), lambda i: (i, E_CQ // Q_LORA)),
                  pl.BlockSpec((tm, KV_LORA), lambda i: (i, E_CKV // KV_LORA)),
                  pl.BlockSpec((tm, LANES), lambda i: (i, E_KR // LANES)),
                  pl.BlockSpec((tm, LANES), lambda i: (i % n_seq_tiles, 0)),
                  pl.BlockSpec((1, Q_LORA), const),
                  pl.BlockSpec((1, KV_LORA), const),
                  pl.BlockSpec(wuq.shape, const),
                  pl.BlockSpec(wukv.shape, const)],
        out_specs=(pl.BlockSpec((tm, N_HEADS_MLA * QK_PAD), lambda i: (i, 0)),
                   pl.BlockSpec((tm, N_HEADS_MLA * QK_PAD), lambda i: (i, 0)),
                   pl.BlockSpec((tm, N_HEADS_MLA * V_PAD), lambda i: (i, 0))),
        compiler_params=_cparams(("parallel",)),
        name="mla_prep",
    )(proj, proj, proj, cos_sin, gq, gkv, wuq, wukv)


MLA_TQ = 1024
MLA_TK = 512


def _mla_attn_kernel(q_ref, k_ref, v_ref, o_ref, m_ref, acc_ref):
    n_sub = MLA_TQ // MLA_TK

    def step(sup, row_lo, kb, masked):
        start = pl.multiple_of(kb * MLA_TK, MLA_TK)
        k = k_ref[pl.ds(start, MLA_TK), :]
        v = v_ref[pl.ds(start, MLA_TK), :]
        q = q_ref[pl.ds(pl.multiple_of(sup * MLA_TQ + row_lo, MLA_TK), MLA_TQ - row_lo), :]
        s = lax.dot_general(q, k, (((1,), (1,)), ((), ())), preferred_element_type=F32)
        if masked:
            row = lax.broadcasted_iota(jnp.int32, s.shape, 0)
            col = lax.broadcasted_iota(jnp.int32, s.shape, 1)
            s = jnp.where(col <= row, s, MASK_VALUE)
        m_old = m_ref[row_lo:, :]
        m_new = jnp.maximum(m_old, jnp.max(s, axis=-1, keepdims=True))
        alpha = jnp.exp2(m_old - m_new)
        p = jnp.exp2(s - jnp.concatenate([m_new] * (MLA_TK // LANES), axis=1))
        pv = jnp.dot(p.astype(BF16), v, preferred_element_type=F32)
        acc_ref[row_lo:, :] = jnp.concatenate([alpha] * (V_PAD // LANES), axis=1) * acc_ref[row_lo:, :] + pv
        m_ref[row_lo:, :] = m_new

    def query_tile(sup, c):
        m_ref[...] = jnp.full(m_ref.shape, -jnp.inf, F32)
        acc_ref[...] = jnp.zeros(acc_ref.shape, F32)
        for b in range(n_sub):
            step(sup, b * MLA_TK, sup * n_sub + b, True)

        def body(j, carry):
            step(sup, 0, j, False)
            return carry

        lax.fori_loop(0, sup * n_sub, body, 0)
        acc = acc_ref[...]
        o_ref[pl.ds(pl.multiple_of(sup * MLA_TQ, MLA_TQ), MLA_TQ), :] = (
            acc[:, :V_DIM] / acc[:, V_DIM:]).astype(o_ref.dtype)
        return c

    lax.fori_loop(0, q_ref.shape[0] // MLA_TQ, query_tile, 0)


def _mla_attn(q, k, v, batch, seq):
    return pl.pallas_call(
        _mla_attn_kernel,
        out_shape=jax.ShapeDtypeStruct((batch * seq, MIX_A), BF16),
        grid=(batch, N_HEADS_MLA),
        in_specs=[pl.BlockSpec((seq, QK_PAD), lambda b, h: (b, h)),
                  pl.BlockSpec((seq, QK_PAD), lambda b, h: (b, h)),
                  pl.BlockSpec((seq, V_PAD), lambda b, h: (b, h))],
        out_specs=pl.BlockSpec((seq, V_DIM), lambda b, h: (b, h)),
        scratch_shapes=[pltpu.VMEM((MLA_TQ, LANES), F32), pltpu.VMEM((MLA_TQ, V_PAD), F32)],
        compiler_params=_cparams(("parallel", "parallel")),
        name="mla_attn",
    )(q, k, v)


DIL_GROUP = 4


def _dilated_kernel(q_ref, k_ref, v_ref, bias_ref, o_ref, nat_m, nat_l, nat_acc, q4_ref, k4_ref, v4_ref,
                    m_ref, l_ref, acc_ref, *, seq):
    quarter = seq // 4

    def by4(start, n):
        return pl.ds(start, n, stride=4)

    for src, tmp, dst in ((q_ref, nat_m, q4_ref), (k_ref, nat_l, k4_ref), (v_ref, nat_acc, v4_ref)):
        tmp[...] = src[...].astype(F32)
        for r4 in range(4):
            dst[r4 * quarter:(r4 + 1) * quarter, :] = tmp[by4(r4, quarter), :]

    def attend(blocks):
        s = [lax.dot_general(q.astype(BF16), k.astype(BF16), (((1,), (1,)), ((), ())),
                             preferred_element_type=F32) + bias for q, k, _, bias in blocks]
        m_p = [jnp.broadcast_to(jnp.max(si, axis=-1, keepdims=True), (BLOCK, LANES)) for si in s]
        p = [jnp.exp2(si - jnp.concatenate([mi, mi], axis=1)).astype(BF16) for si, mi in zip(s, m_p)]
        ones = jnp.ones((2 * BLOCK, LANES), BF16)
        num_l = [jnp.dot(pi, jnp.concatenate([v.astype(BF16), ones], axis=1), preferred_element_type=F32)
                 for pi, (_, _, v, _) in zip(p, blocks)]
        return [(mi, nl[:, HEAD_DIM:], nl[:, :HEAD_DIM]) for mi, nl in zip(m_p, num_l)]

    def merge(old, new):
        m_old, l_old, acc_old = old
        m_p, l_p, num_p = new
        m_new = jnp.maximum(m_old, m_p)
        a = jnp.exp2(m_old - m_new)
        b = jnp.exp2(m_p - m_new)
        return m_new, l_old * a + l_p * b, acc_old * a + num_p * b

    def bias_of(pat, n):
        return bias_ref[pat, 0, jnp.where(n == 0, 1, 0)]

    def prev_block(n):
        return jnp.maximum(n - 1, 0)

    def pass_natural(g, carry):
        where, blocks = [], []
        for u in range(DIL_GROUP):
            n = g * DIL_GROUP + u
            q_at = pl.ds(pl.multiple_of(n * BLOCK, BLOCK), BLOCK)
            k_at = pl.ds(pl.multiple_of(prev_block(n) * BLOCK, BLOCK), 2 * BLOCK)
            where.append(q_at)
            blocks.append((q_ref[q_at, :], k_ref[k_at, :], v_ref[k_at, :], bias_of(0, n)))
        for q_at, (m_p, l_p, num_p) in zip(where, attend(blocks)):
            nat_m[q_at, :] = m_p
            nat_l[q_at, :] = l_p
            nat_acc[q_at, :] = num_p
        return carry

    def pass_dil4(n, carry):
        where, blocks, old = [], [], []
        for r4 in range(4):
            q_at = pl.ds(pl.multiple_of(r4 * quarter + n * BLOCK, BLOCK), BLOCK)
            k_at = pl.ds(pl.multiple_of(r4 * quarter + prev_block(n) * BLOCK, BLOCK), 2 * BLOCK)
            nat_at = by4(r4 + n * (4 * BLOCK), BLOCK)
            where.append(q_at)
            blocks.append((q4_ref[q_at, :], k4_ref[k_at, :], v4_ref[k_at, :], bias_of(1, n)))
            old.append((nat_m[nat_at, :], nat_l[nat_at, :], nat_acc[nat_at, :]))
        for q_at, o, new in zip(where, old, attend(blocks)):
            m_ref[q_at, :], l_ref[q_at, :], acc_ref[q_at, :] = merge(o, new)
        return carry

    def pass_dil16(g, carry):
        where, blocks, old = [], [], []
        for u in range(DIL_GROUP):
            unit = g * DIL_GROUP + u
            r = lax.rem(unit, 16)
            n = lax.div(unit, 16)
            row0 = lax.rem(r, 4) * quarter + lax.div(r, 4)
            q_at = by4(row0 + n * (4 * BLOCK), BLOCK)
            k_at = by4(row0 + prev_block(n) * (4 * BLOCK), 2 * BLOCK)
            where.append(q_at)
            blocks.append((q4_ref[q_at, :], k4_ref[k_at, :], v4_ref[k_at, :], bias_of(2, n)))
            old.append((m_ref[q_at, :], l_ref[q_at, :], acc_ref[q_at, :]))
        for q_at, o, new in zip(where, old, attend(blocks)):
            m_ref[q_at, :], l_ref[q_at, :], acc_ref[q_at, :] = merge(o, new)
        return carry

    n_units = seq // BLOCK
    lax.fori_loop(0, n_units // DIL_GROUP, pass_natural, 0)
    lax.fori_loop(0, n_units // 4, pass_dil4, 0)
    lax.fori_loop(0, n_units // DIL_GROUP, pass_dil16, 0)

    for r4 in range(4):
        rows = slice(r4 * quarter, (r4 + 1) * quarter)
        nat_acc[by4(r4, quarter), :] = acc_ref[rows, :] / l_ref[rows, :]
    o_ref[...] = nat_acc[...].astype(o_ref.dtype)


def _dilated_attn(proj, bias, batch, seq):
    n_pat = len(DIL_PATTERNS)
    assert all(seq // dil // BLOCK >= 2 for _, dil in DIL_PATTERNS), "every residue class needs two key blocks"
    assert tuple(dil for _, dil in DIL_PATTERNS) == (1, 4, 16) and DIL_GROUP == 4
    col = lambda off: (lambda b, h: (b, off // HEAD_DIM + h))
    return pl.pallas_call(
        functools.partial(_dilated_kernel, seq=seq),
        out_shape=jax.ShapeDtypeStruct((batch * seq, MIX_B), BF16),
        grid=(batch, N_HEADS_DIL),
        in_specs=[pl.BlockSpec((seq, HEAD_DIM), col(E_QB)),
                  pl.BlockSpec((seq, HEAD_DIM), col(E_KB)),
                  pl.BlockSpec((seq, HEAD_DIM), col(E_VB)),
                  pl.BlockSpec((n_pat, 1, 2, BLOCK, 2 * BLOCK), lambda b, h: (0, h, 0, 0, 0))],
        out_specs=pl.BlockSpec((seq, HEAD_DIM), lambda b, h: (b, h)),
        scratch_shapes=[pltpu.VMEM((seq, HEAD_DIM), F32)] * 9,
        compiler_params=_cparams(("parallel", "parallel")),
        name="dilated_attn",
    )(proj, proj, proj, bias)


def _gate_out_kernel(oa_ref, ob_ref, gate_ref, x_ref, w_ref, gf_ref, o_ref, *, final_norm):
    half = oa_ref.shape[1]
    g = gate_ref[...].astype(F32)
    sg = g * jax.nn.sigmoid(g)
    mix_a = (oa_ref[...].astype(F32) * sg[:, :half]).astype(BF16)
    mix_b = (ob_ref[...].astype(F32) * sg[:, half:]).astype(BF16)
    y = jnp.dot(mix_a, w_ref[:half, :], preferred_element_type=F32)
    y = y + jnp.dot(mix_b, w_ref[half:, :], preferred_element_type=F32)
    y = x_ref[...] + y
    if final_norm:
        ms = jnp.mean(y * y, axis=-1, keepdims=True)
        y = y * lax.rsqrt(ms + EPS) * gf_ref[...]
    o_ref[...] = y


def _gate_out_proj(oa, oa_blk, ob, ob_blk, proj, gate_blk, x2d, w, gf, final_norm, tm):
    t, d = x2d.shape
    half = w.shape[0] // 2
    mix = w.shape[0]
    return pl.pallas_call(
        functools.partial(_gate_out_kernel, final_norm=final_norm),
        out_shape=jax.ShapeDtypeStruct((t, d), F32),
        grid=(t // tm,),
        in_specs=[pl.BlockSpec((tm, half), lambda i: (i, oa_blk)),
                  pl.BlockSpec((tm, half), lambda i: (i, ob_blk)),
                  pl.BlockSpec((tm, mix), lambda i: (i, gate_blk)),
                  pl.BlockSpec((tm, d), lambda i: (i, 0)),
                  pl.BlockSpec(w.shape, lambda i: (0, 0)),
                  pl.BlockSpec((1, d), lambda i: (0, 0))],
        out_specs=pl.BlockSpec((tm, d), lambda i: (i, 0)),
        compiler_params=_cparams(("parallel",)),
        name="gate_out_proj",
    )(oa, ob, proj, x2d, w, gf)


SB_TK = 256
SB_CHAINS = 4
SB_TQ = SB_CHAINS * SB_TK
SB_DEAD = -160.0


def _sb_attn_kernel(q_ref, k_ref, v_ref, u_ref, o_ref, acc_ref, carry_ref):
    def step(base, tiles):
        n = len(tiles)
        rows = [slice(c * SB_TK, (c + 1) * SB_TK) for c, _, _ in tiles]
        kv_at = [pl.ds(pl.multiple_of(kb * SB_TK, SB_TK), SB_TK) for _, kb, _ in tiles]
        nz = []
        for i, (c, _, _) in enumerate(tiles):
            q = q_ref[pl.ds(pl.multiple_of((base + c) * SB_TK, SB_TK), SB_TK), :]
            nz.append(lax.dot_general(q, k_ref[kv_at[i], :], (((1,), (1,)), ((), ())),
                                      preferred_element_type=F32))
        row = lax.broadcasted_iota(jnp.int32, (SB_TK, SB_TK), 0)
        col = lax.broadcasted_iota(jnp.int32, (SB_TK, SB_TK), 1)
        before = col < row
        hi_lo = []
        for i, (_, _, masked) in enumerate(tiles):
            neg_abs = lax.bitcast_convert_type(
                lax.bitcast_convert_type(nz[i], jnp.uint32) | jnp.uint32(SIGN_BIT), F32)
            log_rest = jnp.minimum(nz[i], 0.0) - jnp.log(1.0 + jnp.exp2(neg_abs)) * LOG2E
            if masked:
                log_rest = jnp.where(before, log_rest, 0.0)
            hi = log_rest.astype(BF16)
            lo = (log_rest - hi.astype(F32)).astype(BF16)
            hi_lo.append(jnp.concatenate([hi, lo], axis=1))
        cum_all = jnp.dot(jnp.concatenate(hi_lo, axis=0), u_ref[...], preferred_element_type=F32)
        weights, carries = [], []
        for i, (_, _, masked) in enumerate(tiles):
            cum = cum_all[i * SB_TK:(i + 1) * SB_TK]
            carry = carry_ref[rows[i], :]
            a = jnp.exp2(cum + carry - nz[i])
            if masked:
                a = jnp.where(before, a, 0.0)
            weights.append(a.astype(BF16))
            carries.append(carry + cum[:, 0:1])
        accs = [acc_ref[rows[i], :] + jnp.dot(weights[i], v_ref[kv_at[i], :], preferred_element_type=F32)
                for i in range(n)]
        for i in range(n):
            acc_ref[rows[i], :] = accs[i]
            carry_ref[rows[i], :] = carries[i]

    def alive(first_chain=0):
        return jnp.max(carry_ref[first_chain * SB_TK:, :]) > SB_DEAD

    def group(g, carry_unused):
        base = g * SB_CHAINS
        acc_ref[...] = jnp.zeros(acc_ref.shape, F32)
        carry_ref[...] = jnp.zeros(carry_ref.shape, F32)
        step(base, [(c, base + c, True) for c in range(SB_CHAINS)])

        def cond(state):
            t, go = state
            return jnp.logical_and(t <= base, go)

        def body(state):
            t, _ = state
            step(base, [(c, base + c - t, False) for c in range(SB_CHAINS)])
            return t + 1, alive()

        lax.while_loop(cond, body, (jnp.int32(1), alive()))
        for j in range(1, SB_CHAINS):
            @pl.when(alive(first_chain=j))
            def _():
                step(base, [(c, c - j, False) for c in range(j, SB_CHAINS)])

        o_ref[pl.ds(pl.multiple_of(base * SB_TK, SB_TQ), SB_TQ), :] = acc_ref[...].astype(o_ref.dtype)
        return carry_unused

    lax.fori_loop(0, q_ref.shape[0] // SB_TQ, group, 0)


def _sb_attn(proj, batch, seq):
    row = np.arange(SB_TK)[:, None]
    colv = np.arange(SB_TK)[None, :]
    tri = (row >= colv).astype(np.float32)
    u = jnp.asarray(np.concatenate([tri, tri], axis=0), BF16)
    hcol = lambda off: (lambda b, h: (b, off // HEAD_DIM + h))
    return pl.pallas_call(
        _sb_attn_kernel,
        out_shape=jax.ShapeDtypeStruct((batch * seq, MIX_ODD), BF16),
        grid=(batch, N_HEADS_SB),
        in_specs=[pl.BlockSpec((seq, HEAD_DIM), hcol(0)),
                  pl.BlockSpec((seq, HEAD_DIM), hcol(MIX_ODD)),
                  pl.BlockSpec((seq, HEAD_DIM), hcol(2 * MIX_ODD)),
                  pl.BlockSpec((2 * SB_TK, SB_TK), lambda b, h: (0, 0))],
        out_specs=pl.BlockSpec((seq, HEAD_DIM), lambda b, h: (b, h)),
        scratch_shapes=[pltpu.VMEM((SB_TQ, HEAD_DIM), F32), pltpu.VMEM((SB_TQ, 1), F32)],
        compiler_params=_cparams(("parallel", "parallel")),
        name="sb_attn",
    )(proj, proj, proj, u)


def _rotate_half_partner(w):
    half = QK_ROPE // 2
    return jnp.concatenate([-w[..., half:], w[..., :half]], axis=-1)


def _even_in_weight(w_in):
    c_q, c_kv, k_rope, q_b, k_b, v_b, gate = jnp.split(
        w_in, np.cumsum((Q_LORA, KV_LORA, QK_ROPE, MIX_B, MIX_B, MIX_B)).tolist(), axis=-1)
    pad = jnp.zeros((w_in.shape[0], E_WIDTH - E_END), w_in.dtype)
    w = jnp.concatenate([gate, q_b, k_b, v_b, c_q, c_kv, k_rope, _rotate_half_partner(k_rope), pad], axis=-1)
    return w.astype(BF16)


def _uq_weight(w_uq):
    w = w_uq.reshape(Q_LORA, N_HEADS_MLA, QK_NOPE + QK_ROPE)
    rot = w[..., QK_NOPE:]
    w = jnp.concatenate([w[..., :QK_NOPE], rot, _rotate_half_partner(rot)], axis=-1)
    return w.reshape(Q_LORA, N_HEADS_MLA * QK_PAD).astype(BF16)


def _col_scale(width, start, stop, scale):
    col = np.ones((1, width), np.float32)
    col[:, start:stop] = scale
    return jnp.asarray(col)


def _cos_sin_table(seq):
    half = QK_ROPE // 2
    inv = 1.0 / (ROPE_THETA ** (jnp.arange(half, dtype=F32) / half))
    ang = jnp.arange(seq).astype(F32)[:, None] * inv[None, :]
    cos, sin = jnp.cos(ang), jnp.sin(ang)
    return jnp.concatenate([cos, cos, sin, sin], axis=-1)


def kernel(x, norm_gain, w_in_even, q_norm_gain, kv_norm_gain, w_uq, w_ukv, w_out_even, rel_bias,
           w_in_odd, w_out_odd, final_norm_gain):
    batch, seq, d_model = x.shape
    t = batch * seq
    x2d = x.reshape(t, d_model)
    head_scale = HEAD_DIM ** -0.5

    proj0 = _rms_proj(x2d, norm_gain[0:1], _even_in_weight(w_in_even[0]),
                      _col_scale(E_WIDTH, E_QB, E_KB, head_scale * LOG2E), tm=1024, tn=PROJ_TN)
    q_a, k_a, v_a = _mla_prep(proj0, _cos_sin_table(seq), q_norm_gain[0:1], kv_norm_gain[0:1],
                              _uq_weight(w_uq[0]), w_ukv[0].astype(BF16), seq, tm=512)
    o_a = _mla_attn(q_a, k_a, v_a, batch, seq)
    o_b = _dilated_attn(proj0, _bias_tiles(rel_bias), batch, seq)
    x1 = _gate_out_proj(o_a, 0, o_b, 0, proj0, E_GATE // MIX_EVEN, x2d, w_out_even[0].astype(BF16),
                        final_norm_gain[None, :], final_norm=False, tm=512)

    proj1 = _rms_proj(x1, norm_gain[1:2], w_in_odd[0].astype(BF16),
                      _col_scale(4 * MIX_ODD, 0, MIX_ODD, -head_scale * LOG2E), tm=1024, tn=1024)
    o_c = _sb_attn(proj1, batch, seq)
    out = _gate_out_proj(o_c, 0, o_c, 1, proj1, 3, x1, w_out_odd[0].astype(BF16),
                         final_norm_gain[None, :], final_norm=True, tm=512)
    return out.reshape(batch, seq, d_model)
```

```python
import functools
import math

import numpy as np
import jax
import jax.numpy as jnp
from jax import lax
from jax.experimental import pallas as pl
from jax.experimental.pallas import tpu as pltpu

F32 = jnp.float32
BF16 = jnp.bfloat16

EPS = 1e-6
N_HEADS_MLA = 8
Q_LORA = 512
KV_LORA = 512
QK_NOPE = 128
QK_ROPE = 64
V_DIM = 128
ROPE_THETA = 10000.0
N_HEADS_DIL = 8
HEAD_DIM = 128
DIL_PATTERNS = ((128, 1), (512, 4), (2048, 16))
N_BUCKETS = 32
BUCKET_MAX_DIST = 2048
N_HEADS_SB = 16
BLOCK = 128
MASK_VALUE = -1e30
LOG2E = 1.4426950408889634
SIGN_BIT = 0x80000000

LANES = 128
VMEM_LIMIT = 48 * 1024 * 1024

MIX_A = N_HEADS_MLA * V_DIM
MIX_B = N_HEADS_DIL * HEAD_DIM
MIX_EVEN = MIX_A + MIX_B
MIX_ODD = N_HEADS_SB * HEAD_DIM
E_GATE = 0
E_QB = E_GATE + MIX_EVEN
E_KB = E_QB + MIX_B
E_VB = E_KB + MIX_B
E_CQ = E_VB + MIX_B
E_CKV = E_CQ + Q_LORA
E_KR = E_CKV + KV_LORA
E_END = E_KR + 2 * QK_ROPE
PROJ_TN = 1280
E_WIDTH = -(-E_END // PROJ_TN) * PROJ_TN
QK_PAD = 2 * LANES
V_PAD = 2 * LANES


def _cparams(sem):
    return pltpu.CompilerParams(dimension_semantics=sem, vmem_limit_bytes=VMEM_LIMIT)


def _bucket_tables():
    qi = np.arange(BLOCK)[:, None]
    kj = np.arange(2 * BLOCK)[None, :]
    rel = BLOCK + qi - kj
    max_exact = N_BUCKETS // 2
    tabs = []
    for window, dil in DIL_PATTERNS:
        span = window // dil
        dist = np.maximum(rel, 0) * dil
        d = np.maximum(dist.astype(np.float64), 1.0)
        frac = np.log(d / max_exact) / math.log(BUCKET_MAX_DIST / max_exact) * (N_BUCKETS - max_exact)
        large = np.minimum(max_exact + np.trunc(frac).astype(np.int64), N_BUCKETS - 1)
        bucket = np.where(dist < max_exact, dist, large)
        valid = (rel >= 0) & (rel <= span)
        tabs.append(np.where(valid, bucket, -1).astype(np.int32))
    return np.stack(tabs)


def _bias_tiles_kernel(bucket_ref, rel_bias_ref, o_ref):
    h = pl.program_id(1)
    bucket = bucket_ref[0]
    out = jnp.full(bucket.shape, MASK_VALUE, F32)
    for i in range(N_BUCKETS):
        out = jnp.where(bucket == i, rel_bias_ref[i, h] * LOG2E, out)
    o_ref[0, 0, 0] = out
    o_ref[0, 0, 1, :, :BLOCK] = out[:, BLOCK:]
    o_ref[0, 0, 1, :, BLOCK:] = jnp.full((BLOCK, BLOCK), MASK_VALUE, F32)


def _bias_tiles(rel_bias):
    n_pat = len(DIL_PATTERNS)
    buckets = jnp.asarray(_bucket_tables())
    return pl.pallas_call(
        _bias_tiles_kernel,
        out_shape=jax.ShapeDtypeStruct((n_pat, N_HEADS_DIL, 2, BLOCK, 2 * BLOCK), F32),
        grid=(n_pat, N_HEADS_DIL),
        in_specs=[pl.BlockSpec((1, BLOCK, 2 * BLOCK), lambda p, h: (p, 0, 0)),
                  pl.BlockSpec(memory_space=pltpu.SMEM)],
        out_specs=pl.BlockSpec((1, 1, 2, BLOCK, 2 * BLOCK), lambda p, h: (p, h, 0, 0, 0)),
        compiler_params=_cparams(("arbitrary", "arbitrary")),
        name="bias_tiles",
    )(buckets, rel_bias)


def _rms_proj_kernel(x_ref, g_ref, w_ref, cs_ref, o_ref, h_ref):
    @pl.when(pl.program_id(1) == 0)
    def _():
        x = x_ref[...]
        ms = jnp.mean(x * x, axis=-1, keepdims=True)
        h_ref[...] = (x * lax.rsqrt(ms + EPS) * g_ref[...]).astype(BF16)

    acc = jnp.dot(h_ref[...], w_ref[...], preferred_element_type=F32)
    o_ref[...] = (acc * cs_ref[...]).astype(o_ref.dtype)


def _rms_proj(x2d, gain, w, col_scale, tm, tn):
    t, d = x2d.shape
    n = w.shape[1]
    return pl.pallas_call(
        _rms_proj_kernel,
        out_shape=jax.ShapeDtypeStruct((t, n), BF16),
        grid=(t // tm, n // tn),
        in_specs=[pl.BlockSpec((tm, d), lambda i, j: (i, 0)),
                  pl.BlockSpec((1, d), lambda i, j: (0, 0)),
                  pl.BlockSpec((d, tn), lambda i, j: (0, j)),
                  pl.BlockSpec((1, tn), lambda i, j: (0, j))],
        out_specs=pl.BlockSpec((tm, tn), lambda i, j: (i, j)),
        scratch_shapes=[pltpu.VMEM((tm, d), BF16)],
        compiler_params=_cparams(("parallel", "arbitrary")),
        name="rms_proj",
    )(x2d, gain, w, col_scale)


def _rope_chunk(chunk, cs):
    r = chunk * cs
    r = r + pltpu.roll(r, QK_ROPE, axis=1)
    lane = lax.broadcasted_iota(jnp.int32, r.shape, 1)
    return jnp.where(lane < QK_ROPE, r, 0.0)


def _mla_prep_kernel(cq_ref, ckv_ref, kr_ref, cs_ref, gq_ref, gkv_ref, wuq_ref, wukv_ref,
                     q_ref, k_ref, v_ref):
    def latent_norm(c_ref, g_ref):
        c = c_ref[...].astype(F32)
        ms = jnp.mean(c * c, axis=-1, keepdims=True)
        return (c * lax.rsqrt(ms + EPS) * g_ref[...]).astype(BF16)

    cs = cs_ref[...]
    scale = LOG2E * (QK_NOPE + QK_ROPE) ** -0.5
    q = jnp.dot(latent_norm(cq_ref, gq_ref), wuq_ref[...], preferred_element_type=F32)
    for h in range(N_HEADS_MLA):
        lo = h * QK_PAD
        q_ref[:, lo:lo + LANES] = (q[:, lo:lo + LANES] * scale).astype(BF16)
        q_ref[:, lo + LANES:lo + QK_PAD] = (_rope_chunk(q[:, lo + LANES:lo + QK_PAD], cs) * scale).astype(BF16)

    k_rot = _rope_chunk(kr_ref[...].astype(F32), cs).astype(BF16)
    ones_col = jnp.ones((cs.shape[0], V_PAD - V_DIM), BF16)
    kv = jnp.dot(latent_norm(ckv_ref, gkv_ref), wukv_ref[...], preferred_element_type=F32)
    for h in range(N_HEADS_MLA):
        lo = h * (QK_NOPE + V_DIM)
        k_ref[:, h * QK_PAD:h * QK_PAD + LANES] = kv[:, lo:lo + QK_NOPE].astype(BF16)
        k_ref[:, h * QK_PAD + LANES:(h + 1) * QK_PAD] = k_rot
        v_ref[:, h * V_PAD:h * V_PAD + V_DIM] = kv[:, lo + QK_NOPE:lo + QK_NOPE + V_DIM].astype(BF16)
        v_ref[:, h * V_PAD + V_DIM:(h + 1) * V_PAD] = ones_col


def _mla_prep(proj, cos_sin, gq, gkv, wuq, wukv, seq, tm):
    t = proj.shape[0]
    n_seq_tiles = seq // tm
    const = lambda i: (0, 0)
    return pl.pallas_call(
        _mla_prep_kernel,
        out_shape=(jax.ShapeDtypeStruct((t, N_HEADS_MLA * QK_PAD), BF16),
                   jax.ShapeDtypeStruct((t, N_HEADS_MLA * QK_PAD), BF16),
                   jax.ShapeDtypeStruct((t, N_HEADS_MLA * V_PAD), BF16)),
        grid=(t // tm,),
        in_specs=[pl.BlockSpec((tm, Q_LORA), lambda i: (i, E_CQ // Q_LORA)),
                  pl.BlockSpec((tm, KV_LORA), lambda i: (i, E_CKV // KV_LORA)),
                  pl.BlockSpec((tm, LANES), lambda i: (i, E_KR // LANES)),
                  pl.BlockSpec((tm, LANES), lambda i: (i % n_seq_tiles, 0)),
                  pl.BlockSpec((1, Q_LORA), const),
                  pl.BlockSpec((1, KV_LORA), const),
                  pl.BlockSpec(wuq.shape, const),
                  pl.BlockSpec(wukv.shape, const)],
        out_specs=(pl.BlockSpec((tm, N_HEADS_MLA * QK_PAD), lambda i: (i, 0)),
                   pl.BlockSpec((tm, N_HEADS_MLA * QK_PAD), lambda i: (i, 0)),
                   pl.BlockSpec((tm, N_HEADS_MLA * V_PAD), lambda i: (i, 0))),
        compiler_params=_cparams(("parallel",)),
        name="mla_prep",
    )(proj, proj, proj, cos_sin, gq, gkv, wuq, wukv)


MLA_TQ = 1024
MLA_TK = 512
MLA_SLAB = 256


def _mla_attn_kernel(q_ref, k_ref, v_ref, o_ref, m_ref, acc_ref):
    n_sub = MLA_TQ // MLA_TK

    def step(sup, row_lo, kb, masked):
        start = pl.multiple_of(kb * MLA_TK, MLA_TK)
        k = k_ref[pl.ds(start, MLA_TK), :]
        v = v_ref[pl.ds(start, MLA_TK), :]
        slabs = [slice(r, r + MLA_SLAB) for r in range(row_lo, MLA_TQ, MLA_SLAB)]
        s = []
        for i, rows in enumerate(slabs):
            q = q_ref[pl.ds(pl.multiple_of(sup * MLA_TQ + rows.start, MLA_SLAB), MLA_SLAB), :]
            si = lax.dot_general(q, k, (((1,), (1,)), ((), ())), preferred_element_type=F32)
            if masked and rows.start - row_lo < MLA_TK:
                row = lax.broadcasted_iota(jnp.int32, si.shape, 0) + (rows.start - row_lo)
                col = lax.broadcasted_iota(jnp.int32, si.shape, 1)
                si = jnp.where(col <= row, si, MASK_VALUE)
            s.append(si)
        p, alpha, m_new = [], [], []
        for rows, si in zip(slabs, s):
            m_old = m_ref[rows, :]
            mi = jnp.maximum(m_old, jnp.max(si, axis=-1, keepdims=True))
            alpha.append(jnp.exp2(m_old - mi))
            p.append(jnp.exp2(si - jnp.concatenate([mi] * (MLA_TK // LANES), axis=1)).astype(BF16))
            m_new.append(mi)
        pv = [jnp.dot(pi, v, preferred_element_type=F32) for pi in p]
        for rows, ai, mi, pvi in zip(slabs, alpha, m_new, pv):
            acc_ref[rows, :] = jnp.concatenate([ai] * (V_PAD // LANES), axis=1) * acc_ref[rows, :] + pvi
            m_ref[rows, :] = mi

    def query_tile(sup, c):
        m_ref[...] = jnp.full(m_ref.shape, -jnp.inf, F32)
        acc_ref[...] = jnp.zeros(acc_ref.shape, F32)
        for b in range(n_sub):
            step(sup, b * MLA_TK, sup * n_sub + b, True)

        def body(j, carry):
            step(sup, 0, j, False)
            return carry

        lax.fori_loop(0, sup * n_sub, body, 0)
        acc = acc_ref[...]
        o_ref[pl.ds(pl.multiple_of(sup * MLA_TQ, MLA_TQ), MLA_TQ), :] = (
            acc[:, :V_DIM] / acc[:, V_DIM:]).astype(o_ref.dtype)
        return c

    lax.fori_loop(0, q_ref.shape[0] // MLA_TQ, query_tile, 0)


def _mla_attn(q, k, v, batch, seq):
    return pl.pallas_call(
        _mla_attn_kernel,
        out_shape=jax.ShapeDtypeStruct((batch * seq, MIX_A), BF16),
        grid=(batch, N_HEADS_MLA),
        in_specs=[pl.BlockSpec((seq, QK_PAD), lambda b, h: (b, h)),
                  pl.BlockSpec((seq, QK_PAD), lambda b, h: (b, h)),
                  pl.BlockSpec((seq, V_PAD), lambda b, h: (b, h))],
        out_specs=pl.BlockSpec((seq, V_DIM), lambda b, h: (b, h)),
        scratch_shapes=[pltpu.VMEM((MLA_TQ, LANES), F32), pltpu.VMEM((MLA_TQ, V_PAD), F32)],
        compiler_params=_cparams(("parallel", "parallel")),
        name="mla_attn",
    )(q, k, v)


DIL_GROUP = 4


def _dilated_kernel(q_ref, k_ref, v_ref, bias_ref, o_ref, nat_m, nat_l, nat_acc, q4_ref, k4_ref, v4_ref,
                    m_ref, l_ref, acc_ref, *, seq):
    quarter = seq // 4

    def by4(start, n):
        return pl.ds(start, n, stride=4)

    for src, tmp, dst in ((q_ref, nat_m, q4_ref), (k_ref, nat_l, k4_ref), (v_ref, nat_acc, v4_ref)):
        tmp[...] = src[...].astype(F32)
        for r4 in range(4):
            dst[r4 * quarter:(r4 + 1) * quarter, :] = tmp[by4(r4, quarter), :]

    def attend(blocks):
        s = [lax.dot_general(q.astype(BF16), k.astype(BF16), (((1,), (1,)), ((), ())),
                             preferred_element_type=F32) + bias for q, k, _, bias in blocks]
        m_p = [jnp.broadcast_to(jnp.max(si, axis=-1, keepdims=True), (BLOCK, LANES)) for si in s]
        p = [jnp.exp2(si - jnp.concatenate([mi, mi], axis=1)).astype(BF16) for si, mi in zip(s, m_p)]
        ones = jnp.ones((2 * BLOCK, LANES), BF16)
        num_l = [jnp.dot(pi, jnp.concatenate([v.astype(BF16), ones], axis=1), preferred_element_type=F32)
                 for pi, (_, _, v, _) in zip(p, blocks)]
        return [(mi, nl[:, HEAD_DIM:], nl[:, :HEAD_DIM]) for mi, nl in zip(m_p, num_l)]

    def merge(old, new):
        m_old, l_old, acc_old = old
        m_p, l_p, num_p = new
        m_new = jnp.maximum(m_old, m_p)
        a = jnp.exp2(m_old - m_new)
        b = jnp.exp2(m_p - m_new)
        return m_new, l_old * a + l_p * b, acc_old * a + num_p * b

    def bias_of(pat, n):
        return bias_ref[pat, 0, jnp.where(n == 0, 1, 0)]

    def prev_block(n):
        return jnp.maximum(n - 1, 0)

    def pass_natural(g, carry):
        where, blocks = [], []
        for u in range(DIL_GROUP):
            n = g * DIL_GROUP + u
            q_at = pl.ds(pl.multiple_of(n * BLOCK, BLOCK), BLOCK)
            k_at = pl.ds(pl.multiple_of(prev_block(n) * BLOCK, BLOCK), 2 * BLOCK)
            where.append(q_at)
            blocks.append((q_ref[q_at, :], k_ref[k_at, :], v_ref[k_at, :], bias_of(0, n)))
        for q_at, (m_p, l_p, num_p) in zip(where, attend(blocks)):
            nat_m[q_at, :] = m_p
            nat_l[q_at, :] = l_p
            nat_acc[q_at, :] = num_p
        return carry

    def pass_dil4(n, carry):
        where, blocks, old = [], [], []
        for r4 in range(4):
            q_at = pl.ds(pl.multiple_of(r4 * quarter + n * BLOCK, BLOCK), BLOCK)
            k_at = pl.ds(pl.multiple_of(r4 * quarter + prev_block(n) * BLOCK, BLOCK), 2 * BLOCK)
            nat_at = by4(r4 + n * (4 * BLOCK), BLOCK)
            where.append(q_at)
            blocks.append((q4_ref[q_at, :], k4_ref[k_at, :], v4_ref[k_at, :], bias_of(1, n)))
            old.append((nat_m[nat_at, :], nat_l[nat_at, :], nat_acc[nat_at, :]))
        for q_at, o, new in zip(where, old, attend(blocks)):
            m_ref[q_at, :], l_ref[q_at, :], acc_ref[q_at, :] = merge(o, new)
        return carry

    def pass_dil16(g, carry):
        where, blocks, old = [], [], []
        for u in range(DIL_GROUP):
            unit = g * DIL_GROUP + u
            r = lax.rem(unit, 16)
            n = lax.div(unit, 16)
            row0 = lax.rem(r, 4) * quarter + lax.div(r, 4)
            q_at = by4(row0 + n * (4 * BLOCK), BLOCK)
            k_at = by4(row0 + prev_block(n) * (4 * BLOCK), 2 * BLOCK)
            where.append(q_at)
            blocks.append((q4_ref[q_at, :], k4_ref[k_at, :], v4_ref[k_at, :], bias_of(2, n)))
            old.append((m_ref[q_at, :], l_ref[q_at, :], acc_ref[q_at, :]))
        for q_at, o, new in zip(where, old, attend(blocks)):
            m_ref[q_at, :], l_ref[q_at, :], acc_ref[q_at, :] = merge(o, new)
        return carry

    n_units = seq // BLOCK
    lax.fori_loop(0, n_units // DIL_GROUP, pass_natural, 0)
    lax.fori_loop(0, n_units // 4, pass_dil4, 0)
    lax.fori_loop(0, n_units // DIL_GROUP, pass_dil16, 0)

    for r4 in range(4):
        rows = slice(r4 * quarter, (r4 + 1) * quarter)
        nat_acc[by4(r4, quarter), :] = acc_ref[rows, :] / l_ref[rows, :]
    o_ref[...] = nat_acc[...].astype(o_ref.dtype)


def _dilated_attn(proj, bias, batch, seq):
    n_pat = len(DIL_PATTERNS)
    assert all(seq // dil // BLOCK >= 2 for _, dil in DIL_PATTERNS), "every residue class needs two key blocks"
    assert tuple(dil for _, dil in DIL_PATTERNS) == (1, 4, 16) and DIL_GROUP == 4
    col = lambda off: (lambda b, h: (b, off // HEAD_DIM + h))
    return pl.pallas_call(
        functools.partial(_dilated_kernel, seq=seq),
        out_shape=jax.ShapeDtypeStruct((batch * seq, MIX_B), BF16),
        grid=(batch, N_HEADS_DIL),
        in_specs=[pl.BlockSpec((seq, HEAD_DIM), col(E_QB)),
                  pl.BlockSpec((seq, HEAD_DIM), col(E_KB)),
                  pl.BlockSpec((seq, HEAD_DIM), col(E_VB)),
                  pl.BlockSpec((n_pat, 1, 2, BLOCK, 2 * BLOCK), lambda b, h: (0, h, 0, 0, 0))],
        out_specs=pl.BlockSpec((seq, HEAD_DIM), lambda b, h: (b, h)),
        scratch_shapes=[pltpu.VMEM((seq, HEAD_DIM), F32)] * 9,
        compiler_params=_cparams(("parallel", "parallel")),
        name="dilated_attn",
    )(proj, proj, proj, bias)


def _gate_out_kernel(oa_ref, ob_ref, gate_ref, x_ref, w_ref, gf_ref, o_ref, *, final_norm):
    half = oa_ref.shape[1]
    g = gate_ref[...].astype(F32)
    sg = g * jax.nn.sigmoid(g)
    mix_a = (oa_ref[...].astype(F32) * sg[:, :half]).astype(BF16)
    mix_b = (ob_ref[...].astype(F32) * sg[:, half:]).astype(BF16)
    y = jnp.dot(mix_a, w_ref[:half, :], preferred_element_type=F32)
    y = y + jnp.dot(mix_b, w_ref[half:, :], preferred_element_type=F32)
    y = x_ref[...] + y
    if final_norm:
        ms = jnp.mean(y * y, axis=-1, keepdims=True)
        y = y * lax.rsqrt(ms + EPS) * gf_ref[...]
    o_ref[...] = y


def _gate_out_proj(oa, oa_blk, ob, ob_blk, proj, gate_blk, x2d, w, gf, final_norm, tm):
    t, d = x2d.shape
    half = w.shape[0] // 2
    mix = w.shape[0]
    return pl.pallas_call(
        functools.partial(_gate_out_kernel, final_norm=final_norm),
        out_shape=jax.ShapeDtypeStruct((t, d), F32),
        grid=(t // tm,),
        in_specs=[pl.BlockSpec((tm, half), lambda i: (i, oa_blk)),
                  pl.BlockSpec((tm, half), lambda i: (i, ob_blk)),
                  pl.BlockSpec((tm, mix), lambda i: (i, gate_blk)),
                  pl.BlockSpec((tm, d), lambda i: (i, 0)),
                  pl.BlockSpec(w.shape, lambda i: (0, 0)),
                  pl.BlockSpec((1, d), lambda i: (0, 0))],
        out_specs=pl.BlockSpec((tm, d), lambda i: (i, 0)),
        compiler_params=_cparams(("parallel",)),
        name="gate_out_proj",
    )(oa, ob, proj, x2d, w, gf)


SB_TK = 256
SB_CHAINS = 4
SB_TQ = SB_CHAINS * SB_TK
SB_DEAD = -160.0


def _sb_attn_kernel(q_ref, k_ref, v_ref, u_ref, o_ref, acc_ref, carry_ref):
    def step(base, tiles):
        n = len(tiles)
        rows = [slice(c * SB_TK, (c + 1) * SB_TK) for c, _, _ in tiles]
        kv_at = [pl.ds(pl.multiple_of(kb * SB_TK, SB_TK), SB_TK) for _, kb, _ in tiles]
        nz = []
        for i, (c, _, _) in enumerate(tiles):
            q = q_ref[pl.ds(pl.multiple_of((base + c) * SB_TK, SB_TK), SB_TK), :]
            nz.append(lax.dot_general(q, k_ref[kv_at[i], :], (((1,), (1,)), ((), ())),
                                      preferred_element_type=F32))
        row = lax.broadcasted_iota(jnp.int32, (SB_TK, SB_TK), 0)
        col = lax.broadcasted_iota(jnp.int32, (SB_TK, SB_TK), 1)
        before = col < row
        hi_lo = []
        for i, (_, _, masked) in enumerate(tiles):
            neg_abs = lax.bitcast_convert_type(
                lax.bitcast_convert_type(nz[i], jnp.uint32) | jnp.uint32(SIGN_BIT), F32)
            log_rest = jnp.minimum(nz[i], 0.0) - jnp.log(1.0 + jnp.exp2(neg_abs)) * LOG2E
            if masked:
                log_rest = jnp.where(before, log_rest, 0.0)
            hi = log_rest.astype(BF16)
            lo = (log_rest - hi.astype(F32)).astype(BF16)
            hi_lo.append(jnp.concatenate([hi, lo], axis=1))
        cum_all = jnp.dot(jnp.concatenate(hi_lo, axis=0), u_ref[...], preferred_element_type=F32)
        weights, carries = [], []
        for i, (_, _, masked) in enumerate(tiles):
            cum = cum_all[i * SB_TK:(i + 1) * SB_TK]
            carry = carry_ref[rows[i], :]
            a = jnp.exp2(cum + carry - nz[i])
            if masked:
                a = jnp.where(before, a, 0.0)
            weights.append(a.astype(BF16))
            carries.append(carry + cum[:, 0:1])
        accs = [acc_ref[rows[i], :] + jnp.dot(weights[i], v_ref[kv_at[i], :], preferred_element_type=F32)
                for i in range(n)]
        for i in range(n):
            acc_ref[rows[i], :] = accs[i]
            carry_ref[rows[i], :] = carries[i]

    def alive(first_chain=0):
        return jnp.max(carry_ref[first_chain * SB_TK:, :]) > SB_DEAD

    def group(g, carry_unused):
        base = g * SB_CHAINS
        acc_ref[...] = jnp.zeros(acc_ref.shape, F32)
        carry_ref[...] = jnp.zeros(carry_ref.shape, F32)
        step(base, [(c, base + c, True) for c in range(SB_CHAINS)])

        def cond(state):
            t, go = state
            return jnp.logical_and(t <= base, go)

        def body(state):
            t, _ = state
            step(base, [(c, base + c - t, False) for c in range(SB_CHAINS)])
            return t + 1, alive()

        lax.while_loop(cond, body, (jnp.int32(1), alive()))
        for j in range(1, SB_CHAINS):
            @pl.when(alive(first_chain=j))
            def _():
                step(base, [(c, c - j, False) for c in range(j, SB_CHAINS)])

        o_ref[pl.ds(pl.multiple_of(base * SB_TK, SB_TQ), SB_TQ), :] = acc_ref[...].astype(o_ref.dtype)
        return carry_unused

    lax.fori_loop(0, q_ref.shape[0] // SB_TQ, group, 0)


def _sb_attn(proj, batch, seq):
    row = np.arange(SB_TK)[:, None]
    colv = np.arange(SB_TK)[None, :]
    tri = (row >= colv).astype(np.float32)
    u = jnp.asarray(np.concatenate([tri, tri], axis=0), BF16)
    hcol = lambda off: (lambda b, h: (b, off // HEAD_DIM + h))
    return pl.pallas_call(
        _sb_attn_kernel,
        out_shape=jax.ShapeDtypeStruct((batch * seq, MIX_ODD), BF16),
        grid=(batch, N_HEADS_SB),
        in_specs=[pl.BlockSpec((seq, HEAD_DIM), hcol(0)),
                  pl.BlockSpec((seq, HEAD_DIM), hcol(MIX_ODD)),
                  pl.BlockSpec((seq, HEAD_DIM), hcol(2 * MIX_ODD)),
                  pl.BlockSpec((2 * SB_TK, SB_TK), lambda b, h: (0, 0))],
        out_specs=pl.BlockSpec((seq, HEAD_DIM), lambda b, h: (b, h)),
        scratch_shapes=[pltpu.VMEM((SB_TQ, HEAD_DIM), F32), pltpu.VMEM((SB_TQ, 1), F32)],
        compiler_params=_cparams(("parallel", "parallel")),
        name="sb_attn",
    )(proj, proj, proj, u)


def _rotate_half_partner(w):
    half = QK_ROPE // 2
    return jnp.concatenate([-w[..., half:], w[..., :half]], axis=-1)


def _even_in_weight(w_in):
    c_q, c_kv, k_rope, q_b, k_b, v_b, gate = jnp.split(
        w_in, np.cumsum((Q_LORA, KV_LORA, QK_ROPE, MIX_B, MIX_B, MIX_B)).tolist(), axis=-1)
    pad = jnp.zeros((w_in.shape[0], E_WIDTH - E_END), w_in.dtype)
    w = jnp.concatenate([gate, q_b, k_b, v_b, c_q, c_kv, k_rope, _rotate_half_partner(k_rope), pad], axis=-1)
    return w.astype(BF16)


def _uq_weight(w_uq):
    w = w_uq.reshape(Q_LORA, N_HEADS_MLA, QK_NOPE + QK_ROPE)
    rot = w[..., QK_NOPE:]
    w = jnp.concatenate([w[..., :QK_NOPE], rot, _rotate_half_partner(rot)], axis=-1)
    return w.reshape(Q_LORA, N_HEADS_MLA * QK_PAD).astype(BF16)


def _col_scale(width, start, stop, scale):
    col = np.ones((1, width), np.float32)
    col[:, start:stop] = scale
    return jnp.asarray(col)


def _cos_sin_table(seq):
    half = QK_ROPE // 2
    inv = 1.0 / (ROPE_THETA ** (jnp.arange(half, dtype=F32) / half))
    ang = jnp.arange(seq).astype(F32)[:, None] * inv[None, :]
    cos, sin = jnp.cos(ang), jnp.sin(ang)
    return jnp.concatenate([cos, cos, sin, sin], axis=-1)


def kernel(x, norm_gain, w_in_even, q_norm_gain, kv_norm_gain, w_uq, w_ukv, w_out_even, rel_bias,
           w_in_odd, w_out_odd, final_norm_gain):
    batch, seq, d_model = x.shape
    t = batch * seq
    x2d = x.reshape(t, d_model)
    head_scale = HEAD_DIM ** -0.5

    proj0 = _rms_proj(x2d, norm_gain[0:1], _even_in_weight(w_in_even[0]),
                      _col_scale(E_WIDTH, E_QB, E_KB, head_scale * LOG2E), tm=1024, tn=PROJ_TN)
    q_a, k_a, v_a = _mla_prep(proj0, _cos_sin_table(seq), q_norm_gain[0:1], kv_norm_gain[0:1],
                              _uq_weight(w_uq[0]), w_ukv[0].astype(BF16), seq, tm=512)
    o_a = _mla_attn(q_a, k_a, v_a, batch, seq)
    o_b = _dilated_attn(proj0, _bias_tiles(rel_bias), batch, seq)
    x1 = _gate_out_proj(o_a, 0, o_b, 0, proj0, E_GATE // MIX_EVEN, x2d, w_out_even[0].astype(BF16),
                        final_norm_gain[None, :], final_norm=False, tm=512)

    proj1 = _rms_proj(x1, norm_gain[1:2], w_in_odd[0].astype(BF16),
                      _col_scale(4 * MIX_ODD, 0, MIX_ODD, -head_scale * LOG2E), tm=1024, tn=1024)
    o_c = _sb_attn(proj1, batch, seq)
    out = _gate_out_proj(o_c, 0, o_c, 1, proj1, 3, x1, w_out_odd[0].astype(BF16),
                         final_norm_gain[None, :], final_norm=True, tm=512)
    return out.reshape(batch, seq, d_model)
```

```python
import functools
import math

import numpy as np
import jax
import jax.numpy as jnp
from jax import lax
from jax.experimental import pallas as pl
from jax.experimental.pallas import tpu as pltpu

F32 = jnp.float32
BF16 = jnp.bfloat16

EPS = 1e-6
N_HEADS_MLA = 8
Q_LORA = 512
KV_LORA = 512
QK_NOPE = 128
QK_ROPE = 64
V_DIM = 128
ROPE_THETA = 10000.0
N_HEADS_DIL = 8
HEAD_DIM = 128
DIL_PATTERNS = ((128, 1), (512, 4), (2048, 16))
N_BUCKETS = 32
BUCKET_MAX_DIST = 2048
N_HEADS_SB = 16
BLOCK = 128
MASK_VALUE = -1e30
LOG2E = 1.4426950408889634
SIGN_BIT = 0x80000000

LANES = 128
VMEM_LIMIT = 48 * 1024 * 1024

MIX_A = N_HEADS_MLA * V_DIM
MIX_B = N_HEADS_DIL * HEAD_DIM
MIX_EVEN = MIX_A + MIX_B
MIX_ODD = N_HEADS_SB * HEAD_DIM
E_GATE = 0
E_QB = E_GATE + MIX_EVEN
E_KB = E_QB + MIX_B
E_VB = E_KB + MIX_B
E_CQ = E_VB + MIX_B
E_CKV = E_CQ + Q_LORA
E_KR = E_CKV + KV_LORA
E_END = E_KR + 2 * QK_ROPE
PROJ_TN = 1280
E_WIDTH = -(-E_END // PROJ_TN) * PROJ_TN
QK_PAD = 2 * LANES
V_PAD = 2 * LANES


def _cparams(sem):
    return pltpu.CompilerParams(dimension_semantics=sem, vmem_limit_bytes=VMEM_LIMIT)


def _bucket_tables():
    qi = np.arange(BLOCK)[:, None]
    kj = np.arange(2 * BLOCK)[None, :]
    rel = BLOCK + qi - kj
    max_exact = N_BUCKETS // 2
    tabs = []
    for window, dil in DIL_PATTERNS:
        span = window // dil
        dist = np.maximum(rel, 0) * dil
        d = np.maximum(dist.astype(np.float64), 1.0)
        frac = np.log(d / max_exact) / math.log(BUCKET_MAX_DIST / max_exact) * (N_BUCKETS - max_exact)
        large = np.minimum(max_exact + np.trunc(frac).astype(np.int64), N_BUCKETS - 1)
        bucket = np.where(dist < max_exact, dist, large)
        valid = (rel >= 0) & (rel <= span)
        tabs.append(np.where(valid, bucket, -1).astype(np.int32))
    return np.stack(tabs)


def _bias_tiles_kernel(bucket_ref, rel_bias_ref, o_ref):
    h = pl.program_id(1)
    bucket = bucket_ref[0]
    out = jnp.full(bucket.shape, MASK_VALUE, F32)
    for i in range(N_BUCKETS):
        out = jnp.where(bucket == i, rel_bias_ref[i, h] * LOG2E, out)
    o_ref[0, 0, 0] = out
    o_ref[0, 0, 1, :, :BLOCK] = out[:, BLOCK:]
    o_ref[0, 0, 1, :, BLOCK:] = jnp.full((BLOCK, BLOCK), MASK_VALUE, F32)


def _bias_tiles(rel_bias):
    n_pat = len(DIL_PATTERNS)
    buckets = jnp.asarray(_bucket_tables())
    return pl.pallas_call(
        _bias_tiles_kernel,
        out_shape=jax.ShapeDtypeStruct((n_pat, N_HEADS_DIL, 2, BLOCK, 2 * BLOCK), F32),
        grid=(n_pat, N_HEADS_DIL),
        in_specs=[pl.BlockSpec((1, BLOCK, 2 * BLOCK), lambda p, h: (p, 0, 0)),
                  pl.BlockSpec(memory_space=pltpu.SMEM)],
        out_specs=pl.BlockSpec((1, 1, 2, BLOCK, 2 * BLOCK), lambda p, h: (p, h, 0, 0, 0)),
        compiler_params=_cparams(("arbitrary", "arbitrary")),
        name="bias_tiles",
    )(buckets, rel_bias)


def _rms_proj_kernel(x_ref, g_ref, w_ref, cs_ref, o_ref, h_ref):
    @pl.when(pl.program_id(1) == 0)
    def _():
        x = x_ref[...]
        ms = jnp.mean(x * x, axis=-1, keepdims=True)
        h_ref[...] = (x * lax.rsqrt(ms + EPS) * g_ref[...]).astype(BF16)

    acc = jnp.dot(h_ref[...], w_ref[...], preferred_element_type=F32)
    o_ref[...] = (acc * cs_ref[...]).astype(o_ref.dtype)


def _rms_proj(x2d, gain, w, col_scale, tm, tn):
    t, d = x2d.shape
    n = w.shape[1]
    return pl.pallas_call(
        _rms_proj_kernel,
        out_shape=jax.ShapeDtypeStruct((t, n), BF16),
        grid=(t // tm, n // tn),
        in_specs=[pl.BlockSpec((tm, d), lambda i, j: (i, 0)),
                  pl.BlockSpec((1, d), lambda i, j: (0, 0)),
                  pl.BlockSpec((d, tn), lambda i, j: (0, j)),
                  pl.BlockSpec((1, tn), lambda i, j: (0, j))],
        out_specs=pl.BlockSpec((tm, tn), lambda i, j: (i, j)),
        scratch_shapes=[pltpu.VMEM((tm, d), BF16)],
        compiler_params=_cparams(("parallel", "arbitrary")),
        name="rms_proj",
    )(x2d, gain, w, col_scale)


def _rope_chunk(chunk, cs):
    r = chunk * cs
    r = r + pltpu.roll(r, QK_ROPE, axis=1)
    lane = lax.broadcasted_iota(jnp.int32, r.shape, 1)
    return jnp.where(lane < QK_ROPE, r, 0.0)


def _mla_prep_kernel(cq_ref, ckv_ref, kr_ref, cs_ref, gq_ref, gkv_ref, wuq_ref, wukv_ref,
                     q_ref, k_ref, v_ref):
    def latent_norm(c_ref, g_ref):
        c = c_ref[...].astype(F32)
        ms = jnp.mean(c * c, axis=-1, keepdims=True)
        return (c * lax.rsqrt(ms + EPS) * g_ref[...]).astype(BF16)

    cs = cs_ref[...]
    scale = LOG2E * (QK_NOPE + QK_ROPE) ** -0.5
    q = jnp.dot(latent_norm(cq_ref, gq_ref), wuq_ref[...], preferred_element_type=F32)
    for h in range(N_HEADS_MLA):
        lo = h * QK_PAD
        q_ref[:, lo:lo + LANES] = (q[:, lo:lo + LANES] * scale).astype(BF16)
        q_ref[:, lo + LANES:lo + QK_PAD] = (_rope_chunk(q[:, lo + LANES:lo + QK_PAD], cs) * scale).astype(BF16)

    k_rot = _rope_chunk(kr_ref[...].astype(F32), cs).astype(BF16)
    ones_col = jnp.ones((cs.shape[0], V_PAD - V_DIM), BF16)
    kv = jnp.dot(latent_norm(ckv_ref, gkv_ref), wukv_ref[...], preferred_element_type=F32)
    for h in range(N_HEADS_MLA):
        lo = h * (QK_NOPE + V_DIM)
        k_ref[:, h * QK_PAD:h * QK_PAD + LANES] = kv[:, lo:lo + QK_NOPE].astype(BF16)
        k_ref[:, h * QK_PAD + LANES:(h + 1) * QK_PAD] = k_rot
        v_ref[:, h * V_PAD:h * V_PAD + V_DIM] = kv[:, lo + QK_NOPE:lo + QK_NOPE + V_DIM].astype(BF16)
        v_ref[:, h * V_PAD + V_DIM:(h + 1) * V_PAD] = ones_col


def _mla_prep(proj, cos_sin, gq, gkv, wuq, wukv, seq, tm):
    t = proj.shape[0]
    n_seq_tiles = seq // tm
    const = lambda i: (0, 0)
    return pl.pallas_call(
        _mla_prep_kernel,
        out_shape=(jax.ShapeDtypeStruct((t, N_HEADS_MLA * QK_PAD), BF16),
                   jax.ShapeDtypeStruct((t, N_HEADS_MLA * QK_PAD), BF16),
                   jax.ShapeDtypeStruct((t, N_HEADS_MLA * V_PAD), BF16)),
        grid=(t // tm,),
        in_specs=[pl.BlockSpec((tm, Q_LORA), lambda i: (i, E_CQ // Q_LORA)),
                  pl.BlockSpec((tm, KV_LORA), lambda i: (i, E_CKV // KV_LORA)),
                  pl.BlockSpec((tm, LANES), lambda i: (i, E_KR // LANES)),
                  pl.BlockSpec((tm, LANES), lambda i: (i % n_seq_tiles, 0)),
                  pl.BlockSpec((1, Q_LORA), const),
                  pl.BlockSpec((1, KV_LORA), const),
                  pl.BlockSpec(wuq.shape, const),
                  pl.BlockSpec(wukv.shape, const)],
        out_specs=(pl.BlockSpec((tm, N_HEADS_MLA * QK_PAD), lambda i: (i, 0)),
                   pl.BlockSpec((tm, N_HEADS_MLA * QK_PAD), lambda i: (i, 0)),
                   pl.BlockSpec((tm, N_HEADS_MLA * V_PAD), lambda i: (i, 0))),
        compiler_params=_cparams(("parallel",)),
        name="mla_prep",
    )(proj, proj, proj, cos_sin, gq, gkv, wuq, wukv)


MLA_TQ = 1024
MLA_TK = 512
MLA_SLAB = 256


def _mla_attn_kernel(q_ref, k_ref, v_ref, o_ref, m_ref, acc_ref):
    n_sub = MLA_TQ // MLA_TK

    def step(sup, row_lo, kb, masked):
        start = pl.multiple_of(kb * MLA_TK, MLA_TK)
        k = k_ref[pl.ds(start, MLA_TK), :]
        v = v_ref[pl.ds(start, MLA_TK), :]
        slabs = [slice(r, r + MLA_SLAB) for r in range(row_lo, MLA_TQ, MLA_SLAB)]
        s = []
        for i, rows in enumerate(slabs):
            q = q_ref[pl.ds(pl.multiple_of(sup * MLA_TQ + rows.start, MLA_SLAB), MLA_SLAB), :]
            si = lax.dot_general(q, k, (((1,), (1,)), ((), ())), preferred_element_type=F32)
            if masked and rows.start - row_lo < MLA_TK:
                row = lax.broadcasted_iota(jnp.int32, si.shape, 0) + (rows.start - row_lo)
                col = lax.broadcasted_iota(jnp.int32, si.shape, 1)
                si = jnp.where(col <= row, si, MASK_VALUE)
            s.append(si)
        p, alpha, m_new = [], [], []
        for rows, si in zip(slabs, s):
            m_old = m_ref[rows, :]
            mi = jnp.maximum(m_old, jnp.max(si, axis=-1, keepdims=True))
            alpha.append(jnp.exp2(m_old - mi))
            p.append(jnp.exp2(si - jnp.concatenate([mi] * (MLA_TK // LANES), axis=1)).astype(BF16))
            m_new.append(mi)
        pv = [jnp.dot(pi, v, preferred_element_type=F32) for pi in p]
        for rows, ai, mi, pvi in zip(slabs, alpha, m_new, pv):
            acc_ref[rows, :] = jnp.concatenate([ai] * (V_PAD // LANES), axis=1) * acc_ref[rows, :] + pvi
            m_ref[rows, :] = mi

    def query_tile(sup, c):
        m_ref[...] = jnp.full(m_ref.shape, -jnp.inf, F32)
        acc_ref[...] = jnp.zeros(acc_ref.shape, F32)
        for b in range(n_sub):
            step(sup, b * MLA_TK, sup * n_sub + b, True)

        def body(j, carry):
            step(sup, 0, j, False)
            return carry

        lax.fori_loop(0, sup * n_sub, body, 0)
        acc = acc_ref[...]
        o_ref[pl.ds(pl.multiple_of(sup * MLA_TQ, MLA_TQ), MLA_TQ), :] = (
            acc[:, :V_DIM] / acc[:, V_DIM:]).astype(o_ref.dtype)
        return c

    lax.fori_loop(0, q_ref.shape[0] // MLA_TQ, query_tile, 0)


def _mla_attn(q, k, v, batch, seq):
    return pl.pallas_call(
        _mla_attn_kernel,
        out_shape=jax.ShapeDtypeStruct((batch * seq, MIX_A), BF16),
        grid=(batch, N_HEADS_MLA),
        in_specs=[pl.BlockSpec((seq, QK_PAD), lambda b, h: (b, h)),
                  pl.BlockSpec((seq, QK_PAD), lambda b, h: (b, h)),
                  pl.BlockSpec((seq, V_PAD), lambda b, h: (b, h))],
        out_specs=pl.BlockSpec((seq, V_DIM), lambda b, h: (b, h)),
        scratch_shapes=[pltpu.VMEM((MLA_TQ, LANES), F32), pltpu.VMEM((MLA_TQ, V_PAD), F32)],
        compiler_params=_cparams(("parallel", "parallel")),
        name="mla_attn",
    )(q, k, v)


DIL_GROUP = 8


def _dilated_kernel(q_ref, k_ref, v_ref, bias_ref, o_ref, nat_m, nat_l, nat_acc, q4_ref, k4_ref, v4_ref,
                    m_ref, l_ref, acc_ref, *, seq):
    quarter = seq // 4

    def by4(start, n):
        return pl.ds(start, n, stride=4)

    for src, tmp, dst in ((q_ref, nat_m, q4_ref), (k_ref, nat_l, k4_ref), (v_ref, nat_acc, v4_ref)):
        tmp[...] = src[...].astype(F32)
        for r4 in range(4):
            dst[r4 * quarter:(r4 + 1) * quarter, :] = tmp[by4(r4, quarter), :]

    def attend(blocks):
        s = [lax.dot_general(q.astype(BF16), k.astype(BF16), (((1,), (1,)), ((), ())),
                             preferred_element_type=F32) + bias for q, k, _, bias in blocks]
        m_p = [jnp.broadcast_to(jnp.max(si, axis=-1, keepdims=True), (BLOCK, LANES)) for si in s]
        p = [jnp.exp2(si - jnp.concatenate([mi, mi], axis=1)).astype(BF16) for si, mi in zip(s, m_p)]
        ones = jnp.ones((2 * BLOCK, LANES), BF16)
        num_l = [jnp.dot(pi, jnp.concatenate([v.astype(BF16), ones], axis=1), preferred_element_type=F32)
                 for pi, (_, _, v, _) in zip(p, blocks)]
        return [(mi, nl[:, HEAD_DIM:], nl[:, :HEAD_DIM]) for mi, nl in zip(m_p, num_l)]

    def merge(old, new):
        m_old, l_old, acc_old = old
        m_p, l_p, num_p = new
        m_new = jnp.maximum(m_old, m_p)
        a = jnp.exp2(m_old - m_new)
        b = jnp.exp2(m_p - m_new)
        return m_new, l_old * a + l_p * b, acc_old * a + num_p * b

    def bias_of(pat, n):
        return bias_ref[pat, 0, jnp.where(n == 0, 1, 0)]

    def prev_block(n):
        return jnp.maximum(n - 1, 0)

    def pass_natural(g, carry):
        where, blocks = [], []
        for u in range(DIL_GROUP):
            n = g * DIL_GROUP + u
            q_at = pl.ds(pl.multiple_of(n * BLOCK, BLOCK), BLOCK)
            k_at = pl.ds(pl.multiple_of(prev_block(n) * BLOCK, BLOCK), 2 * BLOCK)
            where.append(q_at)
            blocks.append((q_ref[q_at, :], k_ref[k_at, :], v_ref[k_at, :], bias_of(0, n)))
        for q_at, (m_p, l_p, num_p) in zip(where, attend(blocks)):
            nat_m[q_at, :] = m_p
            nat_l[q_at, :] = l_p
            nat_acc[q_at, :] = num_p
        return carry

    def pass_dil4(g, carry):
        where, blocks, old = [], [], []
        for u in range(DIL_GROUP):
            r4 = u % 4
            n = g * (DIL_GROUP // 4) + u // 4
            q_at = pl.ds(pl.multiple_of(r4 * quarter + n * BLOCK, BLOCK), BLOCK)
            k_at = pl.ds(pl.multiple_of(r4 * quarter + prev_block(n) * BLOCK, BLOCK), 2 * BLOCK)
            nat_at = by4(r4 + n * (4 * BLOCK), BLOCK)
            where.append(q_at)
            blocks.append((q4_ref[q_at, :], k4_ref[k_at, :], v4_ref[k_at, :], bias_of(1, n)))
            old.append((nat_m[nat_at, :], nat_l[nat_at, :], nat_acc[nat_at, :]))
        for q_at, o, new in zip(where, old, attend(blocks)):
            m_ref[q_at, :], l_ref[q_at, :], acc_ref[q_at, :] = merge(o, new)
        return carry

    def pass_dil16(g, carry):
        where, blocks, old = [], [], []
        for u in range(DIL_GROUP):
            unit = g * DIL_GROUP + u
            r = lax.rem(unit, 16)
            n = lax.div(unit, 16)
            row0 = lax.rem(r, 4) * quarter + lax.div(r, 4)
            q_at = by4(row0 + n * (4 * BLOCK), BLOCK)
            k_at = by4(row0 + prev_block(n) * (4 * BLOCK), 2 * BLOCK)
            where.append(q_at)
            blocks.append((q4_ref[q_at, :], k4_ref[k_at, :], v4_ref[k_at, :], bias_of(2, n)))
            old.append((m_ref[q_at, :], l_ref[q_at, :], acc_ref[q_at, :]))
        for q_at, o, new in zip(where, old, attend(blocks)):
            m_ref[q_at, :], l_ref[q_at, :], acc_ref[q_at, :] = merge(o, new)
        return carry

    n_units = seq // BLOCK
    lax.fori_loop(0, n_units // DIL_GROUP, pass_natural, 0)
    lax.fori_loop(0, n_units // DIL_GROUP, pass_dil4, 0)
    lax.fori_loop(0, n_units // DIL_GROUP, pass_dil16, 0)

    for r4 in range(4):
        rows = slice(r4 * quarter, (r4 + 1) * quarter)
        nat_acc[by4(r4, quarter), :] = acc_ref[rows, :] / l_ref[rows, :]
    o_ref[...] = nat_acc[...].astype(o_ref.dtype)


def _dilated_attn(proj, bias, batch, seq):
    n_pat = len(DIL_PATTERNS)
    assert all(seq // dil // BLOCK >= 2 for _, dil in DIL_PATTERNS), "every residue class needs two key blocks"
    assert tuple(dil for _, dil in DIL_PATTERNS) == (1, 4, 16) and DIL_GROUP % 4 == 0
    col = lambda off: (lambda b, h: (b, off // HEAD_DIM + h))
    return pl.pallas_call(
        functools.partial(_dilated_kernel, seq=seq),
        out_shape=jax.ShapeDtypeStruct((batch * seq, MIX_B), BF16),
        grid=(batch, N_HEADS_DIL),
        in_specs=[pl.BlockSpec((seq, HEAD_DIM), col(E_QB)),
                  pl.BlockSpec((seq, HEAD_DIM), col(E_KB)),
                  pl.BlockSpec((seq, HEAD_DIM), col(E_VB)),
                  pl.BlockSpec((n_pat, 1, 2, BLOCK, 2 * BLOCK), lambda b, h: (0, h, 0, 0, 0))],
        out_specs=pl.BlockSpec((seq, HEAD_DIM), lambda b, h: (b, h)),
        scratch_shapes=[pltpu.VMEM((seq, HEAD_DIM), F32)] * 9,
        compiler_params=_cparams(("parallel", "parallel")),
        name="dilated_attn",
    )(proj, proj, proj, bias)


def _gate_out_kernel(oa_ref, ob_ref, gate_ref, x_ref, w_ref, gf_ref, o_ref, *, final_norm):
    half = oa_ref.shape[1]
    g = gate_ref[...].astype(F32)
    sg = g * jax.nn.sigmoid(g)
    mix_a = (oa_ref[...].astype(F32) * sg[:, :half]).astype(BF16)
    mix_b = (ob_ref[...].astype(F32) * sg[:, half:]).astype(BF16)
    y = jnp.dot(mix_a, w_ref[:half, :], preferred_element_type=F32)
    y = y + jnp.dot(mix_b, w_ref[half:, :], preferred_element_type=F32)
    y = x_ref[...] + y
    if final_norm:
        ms = jnp.mean(y * y, axis=-1, keepdims=True)
        y = y * lax.rsqrt(ms + EPS) * gf_ref[...]
    o_ref[...] = y


def _gate_out_proj(oa, oa_blk, ob, ob_blk, proj, gate_blk, x2d, w, gf, final_norm, tm):
    t, d = x2d.shape
    half = w.shape[0] // 2
    mix = w.shape[0]
    return pl.pallas_call(
        functools.partial(_gate_out_kernel, final_norm=final_norm),
        out_shape=jax.ShapeDtypeStruct((t, d), F32),
        grid=(t // tm,),
        in_specs=[pl.BlockSpec((tm, half), lambda i: (i, oa_blk)),
                  pl.BlockSpec((tm, half), lambda i: (i, ob_blk)),
                  pl.BlockSpec((tm, mix), lambda i: (i, gate_blk)),
                  pl.BlockSpec((tm, d), lambda i: (i, 0)),
                  pl.BlockSpec(w.shape, lambda i: (0, 0)),
                  pl.BlockSpec((1, d), lambda i: (0, 0))],
        out_specs=pl.BlockSpec((tm, d), lambda i: (i, 0)),
        compiler_params=_cparams(("parallel",)),
        name="gate_out_proj",
    )(oa, ob, proj, x2d, w, gf)


SB_TK = 256
SB_CHAINS = 4
SB_TQ = SB_CHAINS * SB_TK
SB_DEAD = -160.0


def _sb_attn_kernel(q_ref, k_ref, v_ref, u_ref, o_ref, acc_ref, carry_ref):
    def step(base, tiles):
        n = len(tiles)
        rows = [slice(c * SB_TK, (c + 1) * SB_TK) for c, _, _ in tiles]
        kv_at = [pl.ds(pl.multiple_of(kb * SB_TK, SB_TK), SB_TK) for _, kb, _ in tiles]
        nz = []
        for i, (c, _, _) in enumerate(tiles):
            q = q_ref[pl.ds(pl.multiple_of((base + c) * SB_TK, SB_TK), SB_TK), :]
            nz.append(lax.dot_general(q, k_ref[kv_at[i], :], (((1,), (1,)), ((), ())),
                                      preferred_element_type=F32))
        row = lax.broadcasted_iota(jnp.int32, (SB_TK, SB_TK), 0)
        col = lax.broadcasted_iota(jnp.int32, (SB_TK, SB_TK), 1)
        before = col < row
        hi_lo = []
        for i, (_, _, masked) in enumerate(tiles):
            neg_abs = lax.bitcast_convert_type(
                lax.bitcast_convert_type(nz[i], jnp.uint32) | jnp.uint32(SIGN_BIT), F32)
            log_rest = jnp.minimum(nz[i], 0.0) - jnp.log(1.0 + jnp.exp2(neg_abs)) * LOG2E
            if masked:
                log_rest = jnp.where(before, log_rest, 0.0)
            hi = log_rest.astype(BF16)
            lo = (log_rest - hi.astype(F32)).astype(BF16)
            hi_lo.append(jnp.concatenate([hi, lo], axis=1))
        cum_all = jnp.dot(jnp.concatenate(hi_lo, axis=0), u_ref[...], preferred_element_type=F32)
        weights, carries = [], []
        for i, (_, _, masked) in enumerate(tiles):
            cum = cum_all[i * SB_TK:(i + 1) * SB_TK]
            carry = carry_ref[rows[i], :]
            a = jnp.exp2(cum + carry - nz[i])
            if masked:
                a = jnp.where(before, a, 0.0)
            weights.append(a.astype(BF16))
            carries.append(carry + cum[:, 0:1])
        accs = [acc_ref[rows[i], :] + jnp.dot(weights[i], v_ref[kv_at[i], :], preferred_element_type=F32)
                for i in range(n)]
        for i in range(n):
            acc_ref[rows[i], :] = accs[i]
            carry_ref[rows[i], :] = carries[i]

    def alive(first_chain=0):
        return jnp.max(carry_ref[first_chain * SB_TK:, :]) > SB_DEAD

    def group(g, carry_unused):
        base = g * SB_CHAINS
        acc_ref[...] = jnp.zeros(acc_ref.shape, F32)
        carry_ref[...] = jnp.zeros(carry_ref.shape, F32)
        step(base, [(c, base + c, True) for c in range(SB_CHAINS)])

        def cond(state):
            t, go = state
            return jnp.logical_and(t <= base, go)

        def body(state):
            t, _ = state
            step(base, [(c, base + c - t, False) for c in range(SB_CHAINS)])
            return t + 1, alive()

        lax.while_loop(cond, body, (jnp.int32(1), alive()))
        for j in range(1, SB_CHAINS):
            @pl.when(alive(first_chain=j))
            def _():
                step(base, [(c, c - j, False) for c in range(j, SB_CHAINS)])

        o_ref[pl.ds(pl.multiple_of(base * SB_TK, SB_TQ), SB_TQ), :] = acc_ref[...].astype(o_ref.dtype)
        return carry_unused

    lax.fori_loop(0, q_ref.shape[0] // SB_TQ, group, 0)


def _sb_attn(proj, batch, seq):
    row = np.arange(SB_TK)[:, None]
    colv = np.arange(SB_TK)[None, :]
    tri = (row >= colv).astype(np.float32)
    u = jnp.asarray(np.concatenate([tri, tri], axis=0), BF16)
    hcol = lambda off: (lambda b, h: (b, off // HEAD_DIM + h))
    return pl.pallas_call(
        _sb_attn_kernel,
        out_shape=jax.ShapeDtypeStruct((batch * seq, MIX_ODD), BF16),
        grid=(batch, N_HEADS_SB),
        in_specs=[pl.BlockSpec((seq, HEAD_DIM), hcol(0)),
                  pl.BlockSpec((seq, HEAD_DIM), hcol(MIX_ODD)),
                  pl.BlockSpec((seq, HEAD_DIM), hcol(2 * MIX_ODD)),
                  pl.BlockSpec((2 * SB_TK, SB_TK), lambda b, h: (0, 0))],
        out_specs=pl.BlockSpec((seq, HEAD_DIM), lambda b, h: (b, h)),
        scratch_shapes=[pltpu.VMEM((SB_TQ, HEAD_DIM), F32), pltpu.VMEM((SB_TQ, 1), F32)],
        compiler_params=_cparams(("parallel", "parallel")),
        name="sb_attn",
    )(proj, proj, proj, u)


def _rotate_half_partner(w):
    half = QK_ROPE // 2
    return jnp.concatenate([-w[..., half:], w[..., :half]], axis=-1)


def _even_in_weight(w_in):
    w_in = w_in.astype(BF16)
    c_q, c_kv, k_rope, q_b, k_b, v_b, gate = jnp.split(
        w_in, np.cumsum((Q_LORA, KV_LORA, QK_ROPE, MIX_B, MIX_B, MIX_B)).tolist(), axis=-1)
    pad = jnp.zeros((w_in.shape[0], E_WIDTH - E_END), w_in.dtype)
    return jnp.concatenate([gate, q_b, k_b, v_b, c_q, c_kv, k_rope, _rotate_half_partner(k_rope), pad], axis=-1)


def _uq_weight(w_uq):
    w = w_uq.astype(BF16).reshape(Q_LORA, N_HEADS_MLA, QK_NOPE + QK_ROPE)
    rot = w[..., QK_NOPE:]
    w = jnp.concatenate([w[..., :QK_NOPE], rot, _rotate_half_partner(rot)], axis=-1)
    return w.reshape(Q_LORA, N_HEADS_MLA * QK_PAD)


def _col_scale(width, start, stop, scale):
    col = np.ones((1, width), np.float32)
    col[:, start:stop] = scale
    return jnp.asarray(col)


def _cos_sin_table(seq):
    half = QK_ROPE // 2
    inv = 1.0 / (ROPE_THETA ** (jnp.arange(half, dtype=F32) / half))
    ang = jnp.arange(seq).astype(F32)[:, None] * inv[None, :]
    cos, sin = jnp.cos(ang), jnp.sin(ang)
    return jnp.concatenate([cos, cos, sin, sin], axis=-1)


def kernel(x, norm_gain, w_in_even, q_norm_gain, kv_norm_gain, w_uq, w_ukv, w_out_even, rel_bias,
           w_in_odd, w_out_odd, final_norm_gain):
    batch, seq, d_model = x.shape
    t = batch * seq
    x2d = x.reshape(t, d_model)
    head_scale = HEAD_DIM ** -0.5

    proj0 = _rms_proj(x2d, norm_gain[0:1], _even_in_weight(w_in_even[0]),
                      _col_scale(E_WIDTH, E_QB, E_KB, head_scale * LOG2E), tm=1024, tn=PROJ_TN)
    q_a, k_a, v_a = _mla_prep(proj0, _cos_sin_table(seq), q_norm_gain[0:1], kv_norm_gain[0:1],
                              _uq_weight(w_uq[0]), w_ukv[0].astype(BF16), seq, tm=512)
    o_a = _mla_attn(q_a, k_a, v_a, batch, seq)
    o_b = _dilated_attn(proj0, _bias_tiles(rel_bias), batch, seq)
    x1 = _gate_out_proj(o_a, 0, o_b, 0, proj0, E_GATE // MIX_EVEN, x2d, w_out_even[0].astype(BF16),
                        final_norm_gain[None, :], final_norm=False, tm=512)

    proj1 = _rms_proj(x1, norm_gain[1:2], w_in_odd[0].astype(BF16),
                      _col_scale(4 * MIX_ODD, 0, MIX_ODD, -head_scale * LOG2E), tm=1024, tn=1024)
    o_c = _sb_attn(proj1, batch, seq)
    out = _gate_out_proj(o_c, 0, o_c, 1, proj1, 3, x1, w_out_odd[0].astype(BF16),
                         final_norm_gain[None, :], final_norm=True, tm=512)
    return out.reshape(batch, seq, d_model)
```

```python
import functools
import math

import numpy as np
import jax
import jax.numpy as jnp
from jax import lax
from jax.experimental import pallas as pl
from jax.experimental.pallas import tpu as pltpu

F32 = jnp.float32
BF16 = jnp.bfloat16

EPS = 1e-6
N_HEADS_MLA = 8
Q_LORA = 512
KV_LORA = 512
QK_NOPE = 128
QK_ROPE = 64
V_DIM = 128
ROPE_THETA = 10000.0
N_HEADS_DIL = 8
HEAD_DIM = 128
DIL_PATTERNS = ((128, 1), (512, 4), (2048, 16))
N_BUCKETS = 32
BUCKET_MAX_DIST = 2048
N_HEADS_SB = 16
BLOCK = 128
MASK_VALUE = -1e30
LOG2E = 1.4426950408889634
SIGN_BIT = 0x80000000

LANES = 128
VMEM_LIMIT = 48 * 1024 * 1024

MIX_A = N_HEADS_MLA * V_DIM
MIX_B = N_HEADS_DIL * HEAD_DIM
MIX_EVEN = MIX_A + MIX_B
MIX_ODD = N_HEADS_SB * HEAD_DIM
E_GATE = 0
E_QB = E_GATE + MIX_EVEN
E_KB = E_QB + MIX_B
E_VB = E_KB + MIX_B
E_CQ = E_VB + MIX_B
E_CKV = E_CQ + Q_LORA
E_KR = E_CKV + KV_LORA
E_END = E_KR + 2 * QK_ROPE
PROJ_TN = 1280
E_WIDTH = -(-E_END // PROJ_TN) * PROJ_TN
QK_PAD = 2 * LANES
V_PAD = 2 * LANES


def _cparams(sem):
    return pltpu.CompilerParams(dimension_semantics=sem, vmem_limit_bytes=VMEM_LIMIT)


def _bucket_tables():
    qi = np.arange(BLOCK)[:, None]
    kj = np.arange(2 * BLOCK)[None, :]
    rel = BLOCK + qi - kj
    max_exact = N_BUCKETS // 2
    tabs = []
    for window, dil in DIL_PATTERNS:
        span = window // dil
        dist = np.maximum(rel, 0) * dil
        d = np.maximum(dist.astype(np.float64), 1.0)
        frac = np.log(d / max_exact) / math.log(BUCKET_MAX_DIST / max_exact) * (N_BUCKETS - max_exact)
        large = np.minimum(max_exact + np.trunc(frac).astype(np.int64), N_BUCKETS - 1)
        bucket = np.where(dist < max_exact, dist, large)
        valid = (rel >= 0) & (rel <= span)
        tabs.append(np.where(valid, bucket, -1).astype(np.int32))
    return np.stack(tabs)


def _bias_tiles_kernel(bucket_ref, rel_bias_ref, o_ref):
    h = pl.program_id(1)
    bucket = bucket_ref[0]
    out = jnp.full(bucket.shape, MASK_VALUE, F32)
    for i in range(N_BUCKETS):
        out = jnp.where(bucket == i, rel_bias_ref[i, h] * LOG2E, out)
    o_ref[0, 0, 0] = out
    o_ref[0, 0, 1, :, :BLOCK] = out[:, BLOCK:]
    o_ref[0, 0, 1, :, BLOCK:] = jnp.full((BLOCK, BLOCK), MASK_VALUE, F32)


def _bias_tiles(rel_bias):
    n_pat = len(DIL_PATTERNS)
    buckets = jnp.asarray(_bucket_tables())
    return pl.pallas_call(
        _bias_tiles_kernel,
        out_shape=jax.ShapeDtypeStruct((n_pat, N_HEADS_DIL, 2, BLOCK, 2 * BLOCK), F32),
        grid=(n_pat, N_HEADS_DIL),
        in_specs=[pl.BlockSpec((1, BLOCK, 2 * BLOCK), lambda p, h: (p, 0, 0)),
                  pl.BlockSpec(memory_space=pltpu.SMEM)],
        out_specs=pl.BlockSpec((1, 1, 2, BLOCK, 2 * BLOCK), lambda p, h: (p, h, 0, 0, 0)),
        compiler_params=_cparams(("arbitrary", "arbitrary")),
        name="bias_tiles",
    )(buckets, rel_bias)


def _rms_proj_kernel(x_ref, g_ref, w_ref, cs_ref, o_ref, h_ref):
    @pl.when(pl.program_id(1) == 0)
    def _():
        x = x_ref[...]
        ms = jnp.mean(x * x, axis=-1, keepdims=True)
        h_ref[...] = (x * lax.rsqrt(ms + EPS) * g_ref[...]).astype(BF16)

    acc = jnp.dot(h_ref[...], w_ref[...], preferred_element_type=F32)
    o_ref[...] = (acc * cs_ref[...]).astype(o_ref.dtype)


def _rms_proj(x2d, gain, w, col_scale, tm, tn):
    t, d = x2d.shape
    n = w.shape[1]
    return pl.pallas_call(
        _rms_proj_kernel,
        out_shape=jax.ShapeDtypeStruct((t, n), BF16),
        grid=(t // tm, n // tn),
        in_specs=[pl.BlockSpec((tm, d), lambda i, j: (i, 0)),
                  pl.BlockSpec((1, d), lambda i, j: (0, 0)),
                  pl.BlockSpec((d, tn), lambda i, j: (0, j)),
                  pl.BlockSpec((1, tn), lambda i, j: (0, j))],
        out_specs=pl.BlockSpec((tm, tn), lambda i, j: (i, j)),
        scratch_shapes=[pltpu.VMEM((tm, d), BF16)],
        compiler_params=_cparams(("parallel", "arbitrary")),
        name="rms_proj",
    )(x2d, gain, w, col_scale)


def _rope_chunk(chunk, cs):
    r = chunk * cs
    r = r + pltpu.roll(r, QK_ROPE, axis=1)
    lane = lax.broadcasted_iota(jnp.int32, r.shape, 1)
    return jnp.where(lane < QK_ROPE, r, 0.0)


def _mla_prep_kernel(cq_ref, ckv_ref, kr_ref, cs_ref, gq_ref, gkv_ref, wuq_ref, wukv_ref,
                     q_ref, k_ref, v_ref):
    def latent_norm(c_ref, g_ref):
        c = c_ref[...].astype(F32)
        ms = jnp.mean(c * c, axis=-1, keepdims=True)
        return (c * lax.rsqrt(ms + EPS) * g_ref[...]).astype(BF16)

    cs = cs_ref[...]
    scale = LOG2E * (QK_NOPE + QK_ROPE) ** -0.5
    q = jnp.dot(latent_norm(cq_ref, gq_ref), wuq_ref[...], preferred_element_type=F32)
    for h in range(N_HEADS_MLA):
        lo = h * QK_PAD
        q_ref[:, lo:lo + LANES] = (q[:, lo:lo + LANES] * scale).astype(BF16)
        q_ref[:, lo + LANES:lo + QK_PAD] = (_rope_chunk(q[:, lo + LANES:lo + QK_PAD], cs) * scale).astype(BF16)

    k_rot = _rope_chunk(kr_ref[...].astype(F32), cs).astype(BF16)
    ones_col = jnp.ones((cs.shape[0], V_PAD - V_DIM), BF16)
    kv = jnp.dot(latent_norm(ckv_ref, gkv_ref), wukv_ref[...], preferred_element_type=F32)
    for h in range(N_HEADS_MLA):
        lo = h * (QK_NOPE + V_DIM)
        k_ref[:, h * QK_PAD:h * QK_PAD + LANES] = kv[:, lo:lo + QK_NOPE].astype(BF16)
        k_ref[:, h * QK_PAD + LANES:(h + 1) * QK_PAD] = k_rot
        v_ref[:, h * V_PAD:h * V_PAD + V_DIM] = kv[:, lo + QK_NOPE:lo + QK_NOPE + V_DIM].astype(BF16)
        v_ref[:, h * V_PAD + V_DIM:(h + 1) * V_PAD] = ones_col


def _mla_prep(proj, cos_sin, gq, gkv, wuq, wukv, seq, tm):
    t = proj.shape[0]
    n_seq_tiles = seq // tm
    const = lambda i: (0, 0)
    return pl.pallas_call(
        _mla_prep_kernel,
        out_shape=(jax.ShapeDtypeStruct((t, N_HEADS_MLA * QK_PAD), BF16),
                   jax.ShapeDtypeStruct((t, N_HEADS_MLA * QK_PAD), BF16),
                   jax.ShapeDtypeStruct((t, N_HEADS_MLA * V_PAD), BF16)),
        grid=(t // tm,),
        in_specs=[pl.BlockSpec((tm, Q_LORA), lambda i: (i, E_CQ // Q_LORA)),
                  pl.BlockSpec((tm, KV_LORA), lambda i: (i, E_CKV // KV_LORA)),
                  pl.BlockSpec((tm, LANES), lambda i: (i, E_KR // LANES)),
                  pl.BlockSpec((tm, LANES), lambda i: (i % n_seq_tiles, 0)),
                  pl.BlockSpec((1, Q_LORA), const),
                  pl.BlockSpec((1, KV_LORA), const),
                  pl.BlockSpec(wuq.shape, const),
                  pl.BlockSpec(wukv.shape, const)],
        out_specs=(pl.BlockSpec((tm, N_HEADS_MLA * QK_PAD), lambda i: (i, 0)),
                   pl.BlockSpec((tm, N_HEADS_MLA * QK_PAD), lambda i: (i, 0)),
                   pl.BlockSpec((tm, N_HEADS_MLA * V_PAD), lambda i: (i, 0))),
        compiler_params=_cparams(("parallel",)),
        name="mla_prep",
    )(proj, proj, proj, cos_sin, gq, gkv, wuq, wukv)


MLA_TQ = 1024
MLA_TK = 1024
MLA_SLAB = 256


def _mla_attn_kernel(q_ref, k_ref, v_ref, o_ref, m_ref, acc_ref):
    n_sub = MLA_TQ // MLA_TK

    def step(sup, row_lo, kb, masked):
        start = pl.multiple_of(kb * MLA_TK, MLA_TK)
        k = k_ref[pl.ds(start, MLA_TK), :]
        v = v_ref[pl.ds(start, MLA_TK), :]
        slabs = [slice(r, r + MLA_SLAB) for r in range(row_lo, MLA_TQ, MLA_SLAB)]
        n_keys = [min(rows.stop - row_lo, MLA_TK) if masked else MLA_TK for rows in slabs]
        s = []
        for rows, nk in zip(slabs, n_keys):
            q = q_ref[pl.ds(pl.multiple_of(sup * MLA_TQ + rows.start, MLA_SLAB), MLA_SLAB), :]
            si = lax.dot_general(q, k[:nk], (((1,), (1,)), ((), ())), preferred_element_type=F32)
            if masked and rows.start - row_lo < MLA_TK:
                row = lax.broadcasted_iota(jnp.int32, si.shape, 0) + (rows.start - row_lo)
                col = lax.broadcasted_iota(jnp.int32, si.shape, 1)
                si = jnp.where(col <= row, si, MASK_VALUE)
            s.append(si)
        p, alpha, m_new = [], [], []
        for rows, si, nk in zip(slabs, s, n_keys):
            m_old = m_ref[rows, :]
            mi = jnp.maximum(m_old, jnp.max(si, axis=-1, keepdims=True))
            alpha.append(jnp.exp2(m_old - mi))
            p.append(jnp.exp2(si - jnp.concatenate([mi] * (nk // LANES), axis=1)).astype(BF16))
            m_new.append(mi)
        pv = [jnp.dot(pi, v[:nk], preferred_element_type=F32) for pi, nk in zip(p, n_keys)]
        for rows, ai, mi, pvi in zip(slabs, alpha, m_new, pv):
            acc_ref[rows, :] = jnp.concatenate([ai] * (V_PAD // LANES), axis=1) * acc_ref[rows, :] + pvi
            m_ref[rows, :] = mi

    def query_tile(sup, c):
        m_ref[...] = jnp.full(m_ref.shape, -jnp.inf, F32)
        acc_ref[...] = jnp.zeros(acc_ref.shape, F32)
        for b in range(n_sub):
            step(sup, b * MLA_TK, sup * n_sub + b, True)

        def body(j, carry):
            step(sup, 0, j, False)
            return carry

        lax.fori_loop(0, sup * n_sub, body, 0)
        acc = acc_ref[...]
        o_ref[pl.ds(pl.multiple_of(sup * MLA_TQ, MLA_TQ), MLA_TQ), :] = (
            acc[:, :V_DIM] / acc[:, V_DIM:]).astype(o_ref.dtype)
        return c

    lax.fori_loop(0, q_ref.shape[0] // MLA_TQ, query_tile, 0)


def _mla_attn(q, k, v, batch, seq):
    return pl.pallas_call(
        _mla_attn_kernel,
        out_shape=jax.ShapeDtypeStruct((batch * seq, MIX_A), BF16),
        grid=(batch, N_HEADS_MLA),
        in_specs=[pl.BlockSpec((seq, QK_PAD), lambda b, h: (b, h)),
                  pl.BlockSpec((seq, QK_PAD), lambda b, h: (b, h)),
                  pl.BlockSpec((seq, V_PAD), lambda b, h: (b, h))],
        out_specs=pl.BlockSpec((seq, V_DIM), lambda b, h: (b, h)),
        scratch_shapes=[pltpu.VMEM((MLA_TQ, LANES), F32), pltpu.VMEM((MLA_TQ, V_PAD), F32)],
        compiler_params=_cparams(("parallel", "parallel")),
        name="mla_attn",
    )(q, k, v)


DIL_GROUP = 8


def _dilated_kernel(q_ref, k_ref, v_ref, bias_ref, o_ref, nat_m, nat_l, nat_acc, q4_ref, k4_ref, v4_ref,
                    m_ref, l_ref, acc_ref, *, seq):
    quarter = seq // 4

    def by4(start, n):
        return pl.ds(start, n, stride=4)

    for src, tmp, dst in ((q_ref, nat_m, q4_ref), (k_ref, nat_l, k4_ref), (v_ref, nat_acc, v4_ref)):
        tmp[...] = src[...].astype(F32)
        for r4 in range(4):
            dst[r4 * quarter:(r4 + 1) * quarter, :] = tmp[by4(r4, quarter), :]

    def attend(blocks):
        s = [lax.dot_general(q.astype(BF16), k.astype(BF16), (((1,), (1,)), ((), ())),
                             preferred_element_type=F32) + bias for q, k, _, bias in blocks]
        m_p = [jnp.broadcast_to(jnp.max(si, axis=-1, keepdims=True), (BLOCK, LANES)) for si in s]
        p = [jnp.exp2(si - jnp.concatenate([mi, mi], axis=1)).astype(BF16) for si, mi in zip(s, m_p)]
        ones = jnp.ones((2 * BLOCK, LANES), BF16)
        num_l = [jnp.dot(pi, jnp.concatenate([v.astype(BF16), ones], axis=1), preferred_element_type=F32)
                 for pi, (_, _, v, _) in zip(p, blocks)]
        return [(mi, nl[:, HEAD_DIM:], nl[:, :HEAD_DIM]) for mi, nl in zip(m_p, num_l)]

    def merge(old, new):
        m_old, l_old, acc_old = old
        m_p, l_p, num_p = new
        m_new = jnp.maximum(m_old, m_p)
        a = jnp.exp2(m_old - m_new)
        b = jnp.exp2(m_p - m_new)
        return m_new, l_old * a + l_p * b, acc_old * a + num_p * b

    def bias_of(pat, n):
        return bias_ref[pat, 0, jnp.where(n == 0, 1, 0)]

    def prev_block(n):
        return jnp.maximum(n - 1, 0)

    def pass_natural(g, carry):
        where, blocks = [], []
        for u in range(DIL_GROUP):
            n = g * DIL_GROUP + u
            q_at = pl.ds(pl.multiple_of(n * BLOCK, BLOCK), BLOCK)
            k_at = pl.ds(pl.multiple_of(prev_block(n) * BLOCK, BLOCK), 2 * BLOCK)
            where.append(q_at)
            blocks.append((q_ref[q_at, :], k_ref[k_at, :], v_ref[k_at, :], bias_of(0, n)))
        for q_at, (m_p, l_p, num_p) in zip(where, attend(blocks)):
            nat_m[q_at, :] = m_p
            nat_l[q_at, :] = l_p
            nat_acc[q_at, :] = num_p
        return carry

    def pass_dil4(g, carry):
        where, blocks, old = [], [], []
        for u in range(DIL_GROUP):
            r4 = u % 4
            n = g * (DIL_GROUP // 4) + u // 4
            q_at = pl.ds(pl.multiple_of(r4 * quarter + n * BLOCK, BLOCK), BLOCK)
            k_at = pl.ds(pl.multiple_of(r4 * quarter + prev_block(n) * BLOCK, BLOCK), 2 * BLOCK)
            nat_at = by4(r4 + n * (4 * BLOCK), BLOCK)
            where.append(q_at)
            blocks.append((q4_ref[q_at, :], k4_ref[k_at, :], v4_ref[k_at, :], bias_of(1, n)))
            old.append((nat_m[nat_at, :], nat_l[nat_at, :], nat_acc[nat_at, :]))
        for q_at, o, new in zip(where, old, attend(blocks)):
            m_ref[q_at, :], l_ref[q_at, :], acc_ref[q_at, :] = merge(o, new)
        return carry

    def pass_dil16(g, carry):
        where, blocks, old = [], [], []
        for u in range(DIL_GROUP):
            unit = g * DIL_GROUP + u
            r = lax.rem(unit, 16)
            n = lax.div(unit, 16)
            row0 = lax.rem(r, 4) * quarter + lax.div(r, 4)
            q_at = by4(row0 + n * (4 * BLOCK), BLOCK)
            k_at = by4(row0 + prev_block(n) * (4 * BLOCK), 2 * BLOCK)
            where.append(q_at)
            blocks.append((q4_ref[q_at, :], k4_ref[k_at, :], v4_ref[k_at, :], bias_of(2, n)))
            old.append((m_ref[q_at, :], l_ref[q_at, :], acc_ref[q_at, :]))
        for q_at, o, new in zip(where, old, attend(blocks)):
            m_ref[q_at, :], l_ref[q_at, :], acc_ref[q_at, :] = merge(o, new)
        return carry

    n_units = seq // BLOCK
    lax.fori_loop(0, n_units // DIL_GROUP, pass_natural, 0)
    lax.fori_loop(0, n_units // DIL_GROUP, pass_dil4, 0)
    lax.fori_loop(0, n_units // DIL_GROUP, pass_dil16, 0)

    for r4 in range(4):
        rows = slice(r4 * quarter, (r4 + 1) * quarter)
        nat_acc[by4(r4, quarter), :] = acc_ref[rows, :] / l_ref[rows, :]
    o_ref[...] = nat_acc[...].astype(o_ref.dtype)


def _dilated_attn(proj, bias, batch, seq):
    n_pat = len(DIL_PATTERNS)
    assert all(seq // dil // BLOCK >= 2 for _, dil in DIL_PATTERNS), "every residue class needs two key blocks"
    assert tuple(dil for _, dil in DIL_PATTERNS) == (1, 4, 16) and DIL_GROUP % 4 == 0
    col = lambda off: (lambda b, h: (b, off // HEAD_DIM + h))
    return pl.pallas_call(
        functools.partial(_dilated_kernel, seq=seq),
        out_shape=jax.ShapeDtypeStruct((batch * seq, MIX_B), BF16),
        grid=(batch, N_HEADS_DIL),
        in_specs=[pl.BlockSpec((seq, HEAD_DIM), col(E_QB)),
                  pl.BlockSpec((seq, HEAD_DIM), col(E_KB)),
                  pl.BlockSpec((seq, HEAD_DIM), col(E_VB)),
                  pl.BlockSpec((n_pat, 1, 2, BLOCK, 2 * BLOCK), lambda b, h: (0, h, 0, 0, 0))],
        out_specs=pl.BlockSpec((seq, HEAD_DIM), lambda b, h: (b, h)),
        scratch_shapes=[pltpu.VMEM((seq, HEAD_DIM), F32)] * 9,
        compiler_params=_cparams(("parallel", "parallel")),
        name="dilated_attn",
    )(proj, proj, proj, bias)


def _gate_out_kernel(oa_ref, ob_ref, gate_ref, x_ref, w_ref, gf_ref, o_ref, *, final_norm):
    half = oa_ref.shape[1]
    g = gate_ref[...].astype(F32)
    sg = g * jax.nn.sigmoid(g)
    mix_a = (oa_ref[...].astype(F32) * sg[:, :half]).astype(BF16)
    mix_b = (ob_ref[...].astype(F32) * sg[:, half:]).astype(BF16)
    y = jnp.dot(mix_a, w_ref[:half, :], preferred_element_type=F32)
    y = y + jnp.dot(mix_b, w_ref[half:, :], preferred_element_type=F32)
    y = x_ref[...] + y
    if final_norm:
        ms = jnp.mean(y * y, axis=-1, keepdims=True)
        y = y * lax.rsqrt(ms + EPS) * gf_ref[...]
    o_ref[...] = y


def _gate_out_proj(oa, oa_blk, ob, ob_blk, proj, gate_blk, x2d, w, gf, final_norm, tm):
    t, d = x2d.shape
    half = w.shape[0] // 2
    mix = w.shape[0]
    return pl.pallas_call(
        functools.partial(_gate_out_kernel, final_norm=final_norm),
        out_shape=jax.ShapeDtypeStruct((t, d), F32),
        grid=(t // tm,),
        in_specs=[pl.BlockSpec((tm, half), lambda i: (i, oa_blk)),
                  pl.BlockSpec((tm, half), lambda i: (i, ob_blk)),
                  pl.BlockSpec((tm, mix), lambda i: (i, gate_blk)),
                  pl.BlockSpec((tm, d), lambda i: (i, 0)),
                  pl.BlockSpec(w.shape, lambda i: (0, 0)),
                  pl.BlockSpec((1, d), lambda i: (0, 0))],
        out_specs=pl.BlockSpec((tm, d), lambda i: (i, 0)),
        compiler_params=_cparams(("parallel",)),
        name="gate_out_proj",
    )(oa, ob, proj, x2d, w, gf)


SB_TK = 256
SB_CHAINS = 4
SB_TQ = SB_CHAINS * SB_TK
SB_DEAD = -160.0


def _sb_attn_kernel(q_ref, k_ref, v_ref, u_ref, o_ref, acc_ref, carry_ref):
    def step(base, tiles):
        n = len(tiles)
        rows = [slice(c * SB_TK, (c + 1) * SB_TK) for c, _, _ in tiles]
        kv_at = [pl.ds(pl.multiple_of(kb * SB_TK, SB_TK), SB_TK) for _, kb, _ in tiles]
        nz = []
        for i, (c, _, _) in enumerate(tiles):
            q = q_ref[pl.ds(pl.multiple_of((base + c) * SB_TK, SB_TK), SB_TK), :]
            nz.append(lax.dot_general(q, k_ref[kv_at[i], :], (((1,), (1,)), ((), ())),
                                      preferred_element_type=F32))
        row = lax.broadcasted_iota(jnp.int32, (SB_TK, SB_TK), 0)
        col = lax.broadcasted_iota(jnp.int32, (SB_TK, SB_TK), 1)
        before = col < row
        hi_lo = []
        for i, (_, _, masked) in enumerate(tiles):
            neg_abs = lax.bitcast_convert_type(
                lax.bitcast_convert_type(nz[i], jnp.uint32) | jnp.uint32(SIGN_BIT), F32)
            log_rest = jnp.minimum(nz[i], 0.0) - jnp.log(1.0 + jnp.exp2(neg_abs)) * LOG2E
            if masked:
                log_rest = jnp.where(before, log_rest, 0.0)
            hi = log_rest.astype(BF16)
            lo = (log_rest - hi.astype(F32)).astype(BF16)
            hi_lo.append(jnp.concatenate([hi, lo], axis=1))
        cum_all = jnp.dot(jnp.concatenate(hi_lo, axis=0), u_ref[...], preferred_element_type=F32)
        weights, carries = [], []
        for i, (_, _, masked) in enumerate(tiles):
            cum = cum_all[i * SB_TK:(i + 1) * SB_TK]
            carry = carry_ref[rows[i], :]
            a = jnp.exp2(cum + carry - nz[i])
            if masked:
                a = jnp.where(before, a, 0.0)
            weights.append(a.astype(BF16))
            carries.append(carry + cum[:, 0:1])
        accs = [acc_ref[rows[i], :] + jnp.dot(weights[i], v_ref[kv_at[i], :], preferred_element_type=F32)
                for i in range(n)]
        for i in range(n):
            acc_ref[rows[i], :] = accs[i]
            carry_ref[rows[i], :] = carries[i]

    def alive(first_chain=0):
        return jnp.max(carry_ref[first_chain * SB_TK:, :]) > SB_DEAD

    def group(g, carry_unused):
        base = g * SB_CHAINS
        acc_ref[...] = jnp.zeros(acc_ref.shape, F32)
        carry_ref[...] = jnp.zeros(carry_ref.shape, F32)
        step(base, [(c, base + c, True) for c in range(SB_CHAINS)])

        def cond(state):
            t, go = state
            return jnp.logical_and(t <= base, go)

        def body(state):
            t, _ = state
            step(base, [(c, base + c - t, False) for c in range(SB_CHAINS)])
            return t + 1, alive()

        lax.while_loop(cond, body, (jnp.int32(1), alive()))
        for j in range(1, SB_CHAINS):
            @pl.when(alive(first_chain=j))
            def _():
                step(base, [(c, c - j, False) for c in range(j, SB_CHAINS)])

        o_ref[pl.ds(pl.multiple_of(base * SB_TK, SB_TQ), SB_TQ), :] = acc_ref[...].astype(o_ref.dtype)
        return carry_unused

    lax.fori_loop(0, q_ref.shape[0] // SB_TQ, group, 0)


def _sb_attn(proj, batch, seq):
    row = np.arange(SB_TK)[:, None]
    colv = np.arange(SB_TK)[None, :]
    tri = (row >= colv).astype(np.float32)
    u = jnp.asarray(np.concatenate([tri, tri], axis=0), BF16)
    hcol = lambda off: (lambda b, h: (b, off // HEAD_DIM + h))
    return pl.pallas_call(
        _sb_attn_kernel,
        out_shape=jax.ShapeDtypeStruct((batch * seq, MIX_ODD), BF16),
        grid=(batch, N_HEADS_SB),
        in_specs=[pl.BlockSpec((seq, HEAD_DIM), hcol(0)),
                  pl.BlockSpec((seq, HEAD_DIM), hcol(MIX_ODD)),
                  pl.BlockSpec((seq, HEAD_DIM), hcol(2 * MIX_ODD)),
                  pl.BlockSpec((2 * SB_TK, SB_TK), lambda b, h: (0, 0))],
        out_specs=pl.BlockSpec((seq, HEAD_DIM), lambda b, h: (b, h)),
        scratch_shapes=[pltpu.VMEM((SB_TQ, HEAD_DIM), F32), pltpu.VMEM((SB_TQ, 1), F32)],
        compiler_params=_cparams(("parallel", "parallel")),
        name="sb_attn",
    )(proj, proj, proj, u)


def _rotate_half_partner(w):
    half = QK_ROPE // 2
    return jnp.concatenate([-w[..., half:], w[..., :half]], axis=-1)


def _even_in_weight(w_in):
    w_in = w_in.astype(BF16)
    c_q, c_kv, k_rope, q_b, k_b, v_b, gate = jnp.split(
        w_in, np.cumsum((Q_LORA, KV_LORA, QK_ROPE, MIX_B, MIX_B, MIX_B)).tolist(), axis=-1)
    pad = jnp.zeros((w_in.shape[0], E_WIDTH - E_END), w_in.dtype)
    return jnp.concatenate([gate, q_b, k_b, v_b, c_q, c_kv, k_rope, _rotate_half_partner(k_rope), pad], axis=-1)


def _uq_weight(w_uq):
    w = w_uq.astype(BF16).reshape(Q_LORA, N_HEADS_MLA, QK_NOPE + QK_ROPE)
    rot = w[..., QK_NOPE:]
    w = jnp.concatenate([w[..., :QK_NOPE], rot, _rotate_half_partner(rot)], axis=-1)
    return w.reshape(Q_LORA, N_HEADS_MLA * QK_PAD)


def _col_scale(width, start, stop, scale):
    col = np.ones((1, width), np.float32)
    col[:, start:stop] = scale
    return jnp.asarray(col)


def _cos_sin_table(seq):
    half = QK_ROPE // 2
    inv = 1.0 / (ROPE_THETA ** (jnp.arange(half, dtype=F32) / half))
    ang = jnp.arange(seq).astype(F32)[:, None] * inv[None, :]
    cos, sin = jnp.cos(ang), jnp.sin(ang)
    return jnp.concatenate([cos, cos, sin, sin], axis=-1)


def kernel(x, norm_gain, w_in_even, q_norm_gain, kv_norm_gain, w_uq, w_ukv, w_out_even, rel_bias,
           w_in_odd, w_out_odd, final_norm_gain):
    batch, seq, d_model = x.shape
    t = batch * seq
    x2d = x.reshape(t, d_model)
    head_scale = HEAD_DIM ** -0.5

    proj0 = _rms_proj(x2d, norm_gain[0:1], _even_in_weight(w_in_even[0]),
                      _col_scale(E_WIDTH, E_QB, E_KB, head_scale * LOG2E), tm=1024, tn=PROJ_TN)
    q_a, k_a, v_a = _mla_prep(proj0, _cos_sin_table(seq), q_norm_gain[0:1], kv_norm_gain[0:1],
                              _uq_weight(w_uq[0]), w_ukv[0].astype(BF16), seq, tm=512)
    o_a = _mla_attn(q_a, k_a, v_a, batch, seq)
    o_b = _dilated_attn(proj0, _bias_tiles(rel_bias), batch, seq)
    x1 = _gate_out_proj(o_a, 0, o_b, 0, proj0, E_GATE // MIX_EVEN, x2d, w_out_even[0].astype(BF16),
                        final_norm_gain[None, :], final_norm=False, tm=512)

    proj1 = _rms_proj(x1, norm_gain[1:2], w_in_odd[0].astype(BF16),
                      _col_scale(4 * MIX_ODD, 0, MIX_ODD, -head_scale * LOG2E), tm=1024, tn=1024)
    o_c = _sb_attn(proj1, batch, seq)
    out = _gate_out_proj(o_c, 0, o_c, 1, proj1, 3, x1, w_out_odd[0].astype(BF16),
                         final_norm_gain[None, :], final_norm=True, tm=512)
    return out.reshape(batch, seq, d_model)
```

```python
import functools
import math

import numpy as np
import jax
import jax.numpy as jnp
from jax import lax
from jax.experimental import pallas as pl
from jax.experimental.pallas import tpu as pltpu

F32 = jnp.float32
BF16 = jnp.bfloat16

EPS = 1e-6
N_HEADS_MLA = 8
Q_LORA = 512
KV_LORA = 512
QK_NOPE = 128
QK_ROPE = 64
V_DIM = 128
ROPE_THETA = 10000.0
N_HEADS_DIL = 8
HEAD_DIM = 128
DIL_PATTERNS = ((128, 1), (512, 4), (2048, 16))
N_BUCKETS = 32
BUCKET_MAX_DIST = 2048
N_HEADS_SB = 16
BLOCK = 128
MASK_VALUE = -1e30
LOG2E = 1.4426950408889634
SIGN_BIT = 0x80000000

LANES = 128
VMEM_LIMIT = 48 * 1024 * 1024

MIX_A = N_HEADS_MLA * V_DIM
MIX_B = N_HEADS_DIL * HEAD_DIM
MIX_EVEN = MIX_A + MIX_B
MIX_ODD = N_HEADS_SB * HEAD_DIM
E_GATE = 0
E_QB = E_GATE + MIX_EVEN
E_KB = E_QB + MIX_B
E_VB = E_KB + MIX_B
E_CQ = E_VB + MIX_B
E_CKV = E_CQ + Q_LORA
E_KR = E_CKV + KV_LORA
E_END = E_KR + 2 * QK_ROPE
PROJ_TM = 1024
PROJ_TN = 1280
PROJ_TN_ODD = 1024
PREP_TM = 512
OUT_TM = 512
E_WIDTH = -(-E_END // PROJ_TN) * PROJ_TN
QK_PAD = 2 * LANES
V_PAD = 2 * LANES


def _cparams(sem):
    return pltpu.CompilerParams(dimension_semantics=sem, vmem_limit_bytes=VMEM_LIMIT)


def _bucket_tables():
    qi = np.arange(BLOCK)[:, None]
    kj = np.arange(2 * BLOCK)[None, :]
    rel = BLOCK + qi - kj
    max_exact = N_BUCKETS // 2
    tabs = []
    for window, dil in DIL_PATTERNS:
        span = window // dil
        dist = np.maximum(rel, 0) * dil
        d = np.maximum(dist.astype(np.float64), 1.0)
        frac = np.log(d / max_exact) / math.log(BUCKET_MAX_DIST / max_exact) * (N_BUCKETS - max_exact)
        large = np.minimum(max_exact + np.trunc(frac).astype(np.int64), N_BUCKETS - 1)
        bucket = np.where(dist < max_exact, dist, large)
        valid = (rel >= 0) & (rel <= span)
        tabs.append(np.where(valid, bucket, -1).astype(np.int32))
    return np.stack(tabs)


def _bias_tiles_kernel(bucket_ref, rel_bias_ref, o_ref):
    h = pl.program_id(1)
    bucket = bucket_ref[0]
    out = jnp.full(bucket.shape, MASK_VALUE, F32)
    for i in range(N_BUCKETS):
        out = jnp.where(bucket == i, rel_bias_ref[i, h] * LOG2E, out)
    o_ref[0, 0, 0] = out
    o_ref[0, 0, 1, :, :BLOCK] = out[:, BLOCK:]
    o_ref[0, 0, 1, :, BLOCK:] = jnp.full((BLOCK, BLOCK), MASK_VALUE, F32)


def _bias_tiles(rel_bias):
    n_pat = len(DIL_PATTERNS)
    buckets = jnp.asarray(_bucket_tables())
    return pl.pallas_call(
        _bias_tiles_kernel,
        out_shape=jax.ShapeDtypeStruct((n_pat, N_HEADS_DIL, 2, BLOCK, 2 * BLOCK), F32),
        grid=(n_pat, N_HEADS_DIL),
        in_specs=[pl.BlockSpec((1, BLOCK, 2 * BLOCK), lambda p, h: (p, 0, 0)),
                  pl.BlockSpec(memory_space=pltpu.SMEM)],
        out_specs=pl.BlockSpec((1, 1, 2, BLOCK, 2 * BLOCK), lambda p, h: (p, h, 0, 0, 0)),
        compiler_params=_cparams(("arbitrary", "arbitrary")),
        name="bias_tiles",
    )(buckets, rel_bias)


def _rms_proj_kernel(x_ref, g_ref, w_ref, cs_ref, o_ref, h_ref):
    @pl.when(pl.program_id(1) == 0)
    def _():
        x = x_ref[...]
        ms = jnp.mean(x * x, axis=-1, keepdims=True)
        h_ref[...] = (x * lax.rsqrt(ms + EPS) * g_ref[...]).astype(BF16)

    acc = jnp.dot(h_ref[...], w_ref[...], preferred_element_type=F32)
    o_ref[...] = (acc * cs_ref[...]).astype(o_ref.dtype)


def _rms_proj(x2d, gain, w, col_scale, tm, tn):
    t, d = x2d.shape
    n = w.shape[1]
    return pl.pallas_call(
        _rms_proj_kernel,
        out_shape=jax.ShapeDtypeStruct((t, n), BF16),
        grid=(t // tm, n // tn),
        in_specs=[pl.BlockSpec((tm, d), lambda i, j: (i, 0)),
                  pl.BlockSpec((1, d), lambda i, j: (0, 0)),
                  pl.BlockSpec((d, tn), lambda i, j: (0, j)),
                  pl.BlockSpec((1, tn), lambda i, j: (0, j))],
        out_specs=pl.BlockSpec((tm, tn), lambda i, j: (i, j)),
        scratch_shapes=[pltpu.VMEM((tm, d), BF16)],
        compiler_params=_cparams(("parallel", "arbitrary")),
        name="rms_proj",
    )(x2d, gain, w, col_scale)


def _rope_chunk(chunk, cs):
    r = chunk * cs
    r = r + pltpu.roll(r, QK_ROPE, axis=1)
    lane = lax.broadcasted_iota(jnp.int32, r.shape, 1)
    return jnp.where(lane < QK_ROPE, r, 0.0)


def _mla_prep_kernel(cq_ref, ckv_ref, kr_ref, cs_ref, gq_ref, gkv_ref, wuq_ref, wukv_ref,
                     q_ref, k_ref, v_ref):
    def latent_norm(c_ref, g_ref):
        c = c_ref[...].astype(F32)
        ms = jnp.mean(c * c, axis=-1, keepdims=True)
        return (c * lax.rsqrt(ms + EPS) * g_ref[...]).astype(BF16)

    cs = cs_ref[...]
    scale = LOG2E * (QK_NOPE + QK_ROPE) ** -0.5
    q = jnp.dot(latent_norm(cq_ref, gq_ref), wuq_ref[...], preferred_element_type=F32)
    for h in range(N_HEADS_MLA):
        lo = h * QK_PAD
        q_ref[:, lo:lo + LANES] = (q[:, lo:lo + LANES] * scale).astype(BF16)
        q_ref[:, lo + LANES:lo + QK_PAD] = (_rope_chunk(q[:, lo + LANES:lo + QK_PAD], cs) * scale).astype(BF16)

    k_rot = _rope_chunk(kr_ref[...].astype(F32), cs).astype(BF16)
    ones_col = jnp.ones((cs.shape[0], V_PAD - V_DIM), BF16)
    kv = jnp.dot(latent_norm(ckv_ref, gkv_ref), wukv_ref[...], preferred_element_type=F32)
    for h in range(N_HEADS_MLA):
        lo = h * (QK_NOPE + V_DIM)
        k_ref[:, h * QK_PAD:h * QK_PAD + LANES] = kv[:, lo:lo + QK_NOPE].astype(BF16)
        k_ref[:, h * QK_PAD + LANES:(h + 1) * QK_PAD] = k_rot
        v_ref[:, h * V_PAD:h * V_PAD + V_DIM] = kv[:, lo + QK_NOPE:lo + QK_NOPE + V_DIM].astype(BF16)
        v_ref[:, h * V_PAD + V_DIM:(h + 1) * V_PAD] = ones_col


def _mla_prep(proj, cos_sin, gq, gkv, wuq, wukv, seq, tm):
    t = proj.shape[0]
    n_seq_tiles = seq // tm
    const = lambda i: (0, 0)
    return pl.pallas_call(
        _mla_prep_kernel,
        out_shape=(jax.ShapeDtypeStruct((t, N_HEADS_MLA * QK_PAD), BF16),
                   jax.ShapeDtypeStruct((t, N_HEADS_MLA * QK_PAD), BF16),
                   jax.ShapeDtypeStruct((t, N_HEADS_MLA * V_PAD), BF16)),
        grid=(t // tm,),
        in_specs=[pl.BlockSpec((tm, Q_LORA), lambda i: (i, E_CQ // Q_LORA)),
                  pl.BlockSpec((tm, KV_LORA), lambda i: (i, E_CKV // KV_LORA)),
                  pl.BlockSpec((tm, LANES), lambda i: (i, E_KR // LANES)),
                  pl.BlockSpec((tm, LANES), lambda i: (i % n_seq_tiles, 0)),
                  pl.BlockSpec((1, Q_LORA), const),
                  pl.BlockSpec((1, KV_LORA), const),
                  pl.BlockSpec(wuq.shape, const),
                  pl.BlockSpec(wukv.shape, const)],
        out_specs=(pl.BlockSpec((tm, N_HEADS_MLA * QK_PAD), lambda i: (i, 0)),
                   pl.BlockSpec((tm, N_HEADS_MLA * QK_PAD), lambda i: (i, 0)),
                   pl.BlockSpec((tm, N_HEADS_MLA * V_PAD), lambda i: (i, 0))),
        compiler_params=_cparams(("parallel",)),
        name="mla_prep",
    )(proj, proj, proj, cos_sin, gq, gkv, wuq, wukv)


MLA_TQ = 1024
MLA_TK = 1024
MLA_SLAB = 256


def _mla_attn_kernel(q_ref, k_ref, v_ref, o_ref, m_ref, acc_ref):
    n_sub = MLA_TQ // MLA_TK

    def step(sup, row_lo, kb, masked):
        start = pl.multiple_of(kb * MLA_TK, MLA_TK)
        k = k_ref[pl.ds(start, MLA_TK), :]
        v = v_ref[pl.ds(start, MLA_TK), :]
        slabs = [slice(r, r + MLA_SLAB) for r in range(row_lo, MLA_TQ, MLA_SLAB)]
        n_keys = [min(rows.stop - row_lo, MLA_TK) if masked else MLA_TK for rows in slabs]
        s = []
        for rows, nk in zip(slabs, n_keys):
            q = q_ref[pl.ds(pl.multiple_of(sup * MLA_TQ + rows.start, MLA_SLAB), MLA_SLAB), :]
            si = lax.dot_general(q, k[:nk], (((1,), (1,)), ((), ())), preferred_element_type=F32)
            if masked and rows.start - row_lo < MLA_TK:
                row = lax.broadcasted_iota(jnp.int32, si.shape, 0) + (rows.start - row_lo)
                col = lax.broadcasted_iota(jnp.int32, si.shape, 1)
                si = jnp.where(col <= row, si, MASK_VALUE)
            s.append(si)
        p, alpha, m_new = [], [], []
        for rows, si, nk in zip(slabs, s, n_keys):
            m_old = m_ref[rows, :]
            mi = jnp.maximum(m_old, jnp.max(si, axis=-1, keepdims=True))
            alpha.append(jnp.exp2(m_old - mi))
            p.append(jnp.exp2(si - jnp.concatenate([mi] * (nk // LANES), axis=1)).astype(BF16))
            m_new.append(mi)
        pv = [jnp.dot(pi, v[:nk], preferred_element_type=F32) for pi, nk in zip(p, n_keys)]
        for rows, ai, mi, pvi in zip(slabs, alpha, m_new, pv):
            acc_ref[rows, :] = jnp.concatenate([ai] * (V_PAD // LANES), axis=1) * acc_ref[rows, :] + pvi
            m_ref[rows, :] = mi

    def query_tile(sup, c):
        m_ref[...] = jnp.full(m_ref.shape, -jnp.inf, F32)
        acc_ref[...] = jnp.zeros(acc_ref.shape, F32)
        for b in range(n_sub):
            step(sup, b * MLA_TK, sup * n_sub + b, True)

        def body(j, carry):
            step(sup, 0, j, False)
            return carry

        lax.fori_loop(0, sup * n_sub, body, 0)
        acc = acc_ref[...]
        o_ref[pl.ds(pl.multiple_of(sup * MLA_TQ, MLA_TQ), MLA_TQ), :] = (
            acc[:, :V_DIM] / acc[:, V_DIM:]).astype(o_ref.dtype)
        return c

    lax.fori_loop(0, q_ref.shape[0] // MLA_TQ, query_tile, 0)


def _mla_attn(q, k, v, batch, seq):
    return pl.pallas_call(
        _mla_attn_kernel,
        out_shape=jax.ShapeDtypeStruct((batch * seq, MIX_A), BF16),
        grid=(batch, N_HEADS_MLA),
        in_specs=[pl.BlockSpec((seq, QK_PAD), lambda b, h: (b, h)),
                  pl.BlockSpec((seq, QK_PAD), lambda b, h: (b, h)),
                  pl.BlockSpec((seq, V_PAD), lambda b, h: (b, h))],
        out_specs=pl.BlockSpec((seq, V_DIM), lambda b, h: (b, h)),
        scratch_shapes=[pltpu.VMEM((MLA_TQ, LANES), F32), pltpu.VMEM((MLA_TQ, V_PAD), F32)],
        compiler_params=_cparams(("parallel", "parallel")),
        name="mla_attn",
    )(q, k, v)


DIL_GROUP = 8


def _dilated_kernel(q_ref, k_ref, v_ref, bias_ref, o_ref, nat_m, nat_l, nat_acc, q4_ref, k4_ref, v4_ref,
                    m_ref, l_ref, acc_ref, *, seq):
    quarter = seq // 4

    def by4(start, n):
        return pl.ds(start, n, stride=4)

    for src, tmp, dst in ((q_ref, nat_m, q4_ref), (k_ref, nat_l, k4_ref), (v_ref, nat_acc, v4_ref)):
        tmp[...] = src[...].astype(F32)
        for r4 in range(4):
            dst[r4 * quarter:(r4 + 1) * quarter, :] = tmp[by4(r4, quarter), :]

    def attend(blocks):
        s = [lax.dot_general(q.astype(BF16), k.astype(BF16), (((1,), (1,)), ((), ())),
                             preferred_element_type=F32) + bias for q, k, _, bias in blocks]
        m_p = [jnp.broadcast_to(jnp.max(si, axis=-1, keepdims=True), (BLOCK, LANES)) for si in s]
        p = [jnp.exp2(si - jnp.concatenate([mi, mi], axis=1)).astype(BF16) for si, mi in zip(s, m_p)]
        ones = jnp.ones((2 * BLOCK, LANES), BF16)
        num_l = [jnp.dot(pi, jnp.concatenate([v.astype(BF16), ones], axis=1), preferred_element_type=F32)
                 for pi, (_, _, v, _) in zip(p, blocks)]
        return [(mi, nl[:, HEAD_DIM:], nl[:, :HEAD_DIM]) for mi, nl in zip(m_p, num_l)]

    def merge(old, new):
        m_old, l_old, acc_old = old
        m_p, l_p, num_p = new
        m_new = jnp.maximum(m_old, m_p)
        a = jnp.exp2(m_old - m_new)
        b = jnp.exp2(m_p - m_new)
        return m_new, l_old * a + l_p * b, acc_old * a + num_p * b

    def bias_of(pat, n):
        return bias_ref[pat, 0, jnp.where(n == 0, 1, 0)]

    def prev_block(n):
        return jnp.maximum(n - 1, 0)

    def pass_natural(g, carry):
        where, blocks = [], []
        for u in range(DIL_GROUP):
            n = g * DIL_GROUP + u
            q_at = pl.ds(pl.multiple_of(n * BLOCK, BLOCK), BLOCK)
            k_at = pl.ds(pl.multiple_of(prev_block(n) * BLOCK, BLOCK), 2 * BLOCK)
            where.append(q_at)
            blocks.append((q_ref[q_at, :], k_ref[k_at, :], v_ref[k_at, :], bias_of(0, n)))
        for q_at, (m_p, l_p, num_p) in zip(where, attend(blocks)):
            nat_m[q_at, :] = m_p
            nat_l[q_at, :] = l_p
            nat_acc[q_at, :] = num_p
        return carry

    def pass_dil4(g, carry):
        where, blocks, old = [], [], []
        for u in range(DIL_GROUP):
            r4 = u % 4
            n = g * (DIL_GROUP // 4) + u // 4
            q_at = pl.ds(pl.multiple_of(r4 * quarter + n * BLOCK, BLOCK), BLOCK)
            k_at = pl.ds(pl.multiple_of(r4 * quarter + prev_block(n) * BLOCK, BLOCK), 2 * BLOCK)
            nat_at = by4(r4 + n * (4 * BLOCK), BLOCK)
            where.append(q_at)
            blocks.append((q4_ref[q_at, :], k4_ref[k_at, :], v4_ref[k_at, :], bias_of(1, n)))
            old.append((nat_m[nat_at, :], nat_l[nat_at, :], nat_acc[nat_at, :]))
        for q_at, o, new in zip(where, old, attend(blocks)):
            m_ref[q_at, :], l_ref[q_at, :], acc_ref[q_at, :] = merge(o, new)
        return carry

    def pass_dil16(g, carry):
        where, blocks, old = [], [], []
        for u in range(DIL_GROUP):
            unit = g * DIL_GROUP + u
            r = lax.rem(unit, 16)
            n = lax.div(unit, 16)
            row0 = lax.rem(r, 4) * quarter + lax.div(r, 4)
            q_at = by4(row0 + n * (4 * BLOCK), BLOCK)
            k_at = by4(row0 + prev_block(n) * (4 * BLOCK), 2 * BLOCK)
            where.append(q_at)
            blocks.append((q4_ref[q_at, :], k4_ref[k_at, :], v4_ref[k_at, :], bias_of(2, n)))
            old.append((m_ref[q_at, :], l_ref[q_at, :], acc_ref[q_at, :]))
        for q_at, o, new in zip(where, old, attend(blocks)):
            m_ref[q_at, :], l_ref[q_at, :], acc_ref[q_at, :] = merge(o, new)
        return carry

    n_units = seq // BLOCK
    lax.fori_loop(0, n_units // DIL_GROUP, pass_natural, 0)
    lax.fori_loop(0, n_units // DIL_GROUP, pass_dil4, 0)
    lax.fori_loop(0, n_units // DIL_GROUP, pass_dil16, 0)

    for r4 in range(4):
        rows = slice(r4 * quarter, (r4 + 1) * quarter)
        nat_acc[by4(r4, quarter), :] = acc_ref[rows, :] / l_ref[rows, :]
    o_ref[...] = nat_acc[...].astype(o_ref.dtype)


def _dilated_attn(proj, bias, batch, seq):
    n_pat = len(DIL_PATTERNS)
    assert all(seq // dil // BLOCK >= 2 for _, dil in DIL_PATTERNS), "every residue class needs two key blocks"
    assert tuple(dil for _, dil in DIL_PATTERNS) == (1, 4, 16) and DIL_GROUP % 4 == 0
    col = lambda off: (lambda b, h: (b, off // HEAD_DIM + h))
    return pl.pallas_call(
        functools.partial(_dilated_kernel, seq=seq),
        out_shape=jax.ShapeDtypeStruct((batch * seq, MIX_B), BF16),
        grid=(batch, N_HEADS_DIL),
        in_specs=[pl.BlockSpec((seq, HEAD_DIM), col(E_QB)),
                  pl.BlockSpec((seq, HEAD_DIM), col(E_KB)),
                  pl.BlockSpec((seq, HEAD_DIM), col(E_VB)),
                  pl.BlockSpec((n_pat, 1, 2, BLOCK, 2 * BLOCK), lambda b, h: (0, h, 0, 0, 0))],
        out_specs=pl.BlockSpec((seq, HEAD_DIM), lambda b, h: (b, h)),
        scratch_shapes=[pltpu.VMEM((seq, HEAD_DIM), F32)] * 9,
        compiler_params=_cparams(("parallel", "parallel")),
        name="dilated_attn",
    )(proj, proj, proj, bias)


def _gate_out_kernel(*refs, n_mixers, final_norm):
    mixer_refs = refs[:n_mixers]
    gate_ref, x_ref, w_ref, gf_ref, o_ref = refs[n_mixers:]
    g = gate_ref[...].astype(F32)
    sg = g * jax.nn.sigmoid(g)
    y = x_ref[...]
    lo = 0
    for m_ref in mixer_refs:
        hi = lo + m_ref.shape[1]
        mix = (m_ref[...].astype(F32) * sg[:, lo:hi]).astype(BF16)
        y = y + jnp.dot(mix, w_ref[lo:hi, :], preferred_element_type=F32)
        lo = hi
    if final_norm:
        ms = jnp.mean(y * y, axis=-1, keepdims=True)
        y = y * lax.rsqrt(ms + EPS) * gf_ref[...]
    o_ref[...] = y


def _gate_out_proj(mixers, proj, gate_blk, x2d, w, gf, final_norm, tm):
    t, d = x2d.shape
    mix = w.shape[0]
    assert sum(m.shape[1] for m in mixers) == mix
    return pl.pallas_call(
        functools.partial(_gate_out_kernel, n_mixers=len(mixers), final_norm=final_norm),
        out_shape=jax.ShapeDtypeStruct((t, d), F32),
        grid=(t // tm,),
        in_specs=[pl.BlockSpec((tm, m.shape[1]), lambda i: (i, 0)) for m in mixers] + [
            pl.BlockSpec((tm, mix), lambda i: (i, gate_blk)),
            pl.BlockSpec((tm, d), lambda i: (i, 0)),
            pl.BlockSpec(w.shape, lambda i: (0, 0)),
            pl.BlockSpec((1, d), lambda i: (0, 0))],
        out_specs=pl.BlockSpec((tm, d), lambda i: (i, 0)),
        compiler_params=_cparams(("parallel",)),
        name="gate_out_proj",
    )(*mixers, proj, x2d, w, gf)


SB_TK = 256
SB_CHAINS = 4
SB_TQ = SB_CHAINS * SB_TK
SB_DEAD = -160.0


def _sb_attn_kernel(q_ref, k_ref, v_ref, u_ref, o_ref, acc_ref, carry_ref):
    def step(base, tiles):
        n = len(tiles)
        rows = [slice(c * SB_TK, (c + 1) * SB_TK) for c, _, _ in tiles]
        kv_at = [pl.ds(pl.multiple_of(kb * SB_TK, SB_TK), SB_TK) for _, kb, _ in tiles]
        nz = []
        for i, (c, _, _) in enumerate(tiles):
            q = q_ref[pl.ds(pl.multiple_of((base + c) * SB_TK, SB_TK), SB_TK), :]
            nz.append(lax.dot_general(q, k_ref[kv_at[i], :], (((1,), (1,)), ((), ())),
                                      preferred_element_type=F32))
        row = lax.broadcasted_iota(jnp.int32, (SB_TK, SB_TK), 0)
        col = lax.broadcasted_iota(jnp.int32, (SB_TK, SB_TK), 1)
        before = col < row
        hi_lo = []
        for i, (_, _, masked) in enumerate(tiles):
            neg_abs = lax.bitcast_convert_type(
                lax.bitcast_convert_type(nz[i], jnp.uint32) | jnp.uint32(SIGN_BIT), F32)
            log_rest = jnp.minimum(nz[i], 0.0) - jnp.log(1.0 + jnp.exp2(neg_abs)) * LOG2E
            if masked:
                log_rest = jnp.where(before, log_rest, 0.0)
            hi = log_rest.astype(BF16)
            lo = (log_rest - hi.astype(F32)).astype(BF16)
            hi_lo.append(jnp.concatenate([hi, lo], axis=1))
        cum_all = jnp.dot(jnp.concatenate(hi_lo, axis=0), u_ref[...], preferred_element_type=F32)
        weights, carries = [], []
        for i, (_, _, masked) in enumerate(tiles):
            cum = cum_all[i * SB_TK:(i + 1) * SB_TK]
            carry = carry_ref[rows[i], :]
            a = jnp.exp2(cum + carry - nz[i])
            if masked:
                a = jnp.where(before, a, 0.0)
            weights.append(a.astype(BF16))
            carries.append(carry + cum[:, 0:1])
        accs = [acc_ref[rows[i], :] + jnp.dot(weights[i], v_ref[kv_at[i], :], preferred_element_type=F32)
                for i in range(n)]
        for i in range(n):
            acc_ref[rows[i], :] = accs[i]
            carry_ref[rows[i], :] = carries[i]

    def alive(first_chain=0):
        return jnp.max(carry_ref[first_chain * SB_TK:, :]) > SB_DEAD

    def group(g, carry_unused):
        base = g * SB_CHAINS
        acc_ref[...] = jnp.zeros(acc_ref.shape, F32)
        carry_ref[...] = jnp.zeros(carry_ref.shape, F32)
        step(base, [(c, base + c, True) for c in range(SB_CHAINS)])

        def cond(state):
            t, go = state
            return jnp.logical_and(t <= base, go)

        def body(state):
            t, _ = state
            step(base, [(c, base + c - t, False) for c in range(SB_CHAINS)])
            return t + 1, alive()

        lax.while_loop(cond, body, (jnp.int32(1), alive()))
        for j in range(1, SB_CHAINS):
            @pl.when(alive(first_chain=j))
            def _():
                step(base, [(c, c - j, False) for c in range(j, SB_CHAINS)])

        o_ref[pl.ds(pl.multiple_of(base * SB_TK, SB_TQ), SB_TQ), :] = acc_ref[...].astype(o_ref.dtype)
        return carry_unused

    lax.fori_loop(0, q_ref.shape[0] // SB_TQ, group, 0)


def _sb_attn(proj, batch, seq):
    row = np.arange(SB_TK)[:, None]
    colv = np.arange(SB_TK)[None, :]
    tri = (row >= colv).astype(np.float32)
    u = jnp.asarray(np.concatenate([tri, tri], axis=0), BF16)
    hcol = lambda off: (lambda b, h: (b, off // HEAD_DIM + h))
    return pl.pallas_call(
        _sb_attn_kernel,
        out_shape=jax.ShapeDtypeStruct((batch * seq, MIX_ODD), BF16),
        grid=(batch, N_HEADS_SB),
        in_specs=[pl.BlockSpec((seq, HEAD_DIM), hcol(0)),
                  pl.BlockSpec((seq, HEAD_DIM), hcol(MIX_ODD)),
                  pl.BlockSpec((seq, HEAD_DIM), hcol(2 * MIX_ODD)),
                  pl.BlockSpec((2 * SB_TK, SB_TK), lambda b, h: (0, 0))],
        out_specs=pl.BlockSpec((seq, HEAD_DIM), lambda b, h: (b, h)),
        scratch_shapes=[pltpu.VMEM((SB_TQ, HEAD_DIM), F32), pltpu.VMEM((SB_TQ, 1), F32)],
        compiler_params=_cparams(("parallel", "parallel")),
        name="sb_attn",
    )(proj, proj, proj, u)


def _rotate_half_partner(w):
    half = QK_ROPE // 2
    return jnp.concatenate([-w[..., half:], w[..., :half]], axis=-1)


def _even_in_weight(w_in):
    w_in = w_in.astype(BF16)
    c_q, c_kv, k_rope, q_b, k_b, v_b, gate = jnp.split(
        w_in, np.cumsum((Q_LORA, KV_LORA, QK_ROPE, MIX_B, MIX_B, MIX_B)).tolist(), axis=-1)
    pad = jnp.zeros((w_in.shape[0], E_WIDTH - E_END), w_in.dtype)
    return jnp.concatenate([gate, q_b, k_b, v_b, c_q, c_kv, k_rope, _rotate_half_partner(k_rope), pad], axis=-1)


def _uq_weight(w_uq):
    w = w_uq.astype(BF16).reshape(Q_LORA, N_HEADS_MLA, QK_NOPE + QK_ROPE)
    rot = w[..., QK_NOPE:]
    w = jnp.concatenate([w[..., :QK_NOPE], rot, _rotate_half_partner(rot)], axis=-1)
    return w.reshape(Q_LORA, N_HEADS_MLA * QK_PAD)


def _col_scale(width, start, stop, scale):
    col = np.ones((1, width), np.float32)
    col[:, start:stop] = scale
    return jnp.asarray(col)


def _cos_sin_table(seq):
    half = QK_ROPE // 2
    inv = 1.0 / (ROPE_THETA ** (jnp.arange(half, dtype=F32) / half))
    ang = jnp.arange(seq).astype(F32)[:, None] * inv[None, :]
    cos, sin = jnp.cos(ang), jnp.sin(ang)
    return jnp.concatenate([cos, cos, sin, sin], axis=-1)


def kernel(x, norm_gain, w_in_even, q_norm_gain, kv_norm_gain, w_uq, w_ukv, w_out_even, rel_bias,
           w_in_odd, w_out_odd, final_norm_gain):
    batch, seq, d_model = x.shape
    t = batch * seq
    x2d = x.reshape(t, d_model)
    head_scale = HEAD_DIM ** -0.5

    proj0 = _rms_proj(x2d, norm_gain[0:1], _even_in_weight(w_in_even[0]),
                      _col_scale(E_WIDTH, E_QB, E_KB, head_scale * LOG2E), tm=PROJ_TM, tn=PROJ_TN)
    q_a, k_a, v_a = _mla_prep(proj0, _cos_sin_table(seq), q_norm_gain[0:1], kv_norm_gain[0:1],
                              _uq_weight(w_uq[0]), w_ukv[0].astype(BF16), seq, tm=PREP_TM)
    o_a = _mla_attn(q_a, k_a, v_a, batch, seq)
    o_b = _dilated_attn(proj0, _bias_tiles(rel_bias), batch, seq)
    x1 = _gate_out_proj([o_a, o_b], proj0, E_GATE // MIX_EVEN, x2d, w_out_even[0].astype(BF16),
                        final_norm_gain[None, :], final_norm=False, tm=OUT_TM)

    proj1 = _rms_proj(x1, norm_gain[1:2], w_in_odd[0].astype(BF16),
                      _col_scale(4 * MIX_ODD, 0, MIX_ODD, -head_scale * LOG2E), tm=PROJ_TM, tn=PROJ_TN_ODD)
    o_c = _sb_attn(proj1, batch, seq)
    out = _gate_out_proj([o_c], proj1, 3, x1, w_out_odd[0].astype(BF16),
                         final_norm_gain[None, :], final_norm=True, tm=OUT_TM)
    return out.reshape(batch, seq, d_model)
```

```python
import functools
import math

import numpy as np
import jax
import jax.numpy as jnp
from jax import lax
from jax.experimental import pallas as pl
from jax.experimental.pallas import tpu as pltpu

F32 = jnp.float32
BF16 = jnp.bfloat16

EPS = 1e-6
N_HEADS_MLA = 8
Q_LORA = 512
KV_LORA = 512
QK_NOPE = 128
QK_ROPE = 64
V_DIM = 128
ROPE_THETA = 10000.0
N_HEADS_DIL = 8
HEAD_DIM = 128
DIL_PATTERNS = ((128, 1), (512, 4), (2048, 16))
N_BUCKETS = 32
BUCKET_MAX_DIST = 2048
N_HEADS_SB = 16
BLOCK = 128
MASK_VALUE = -1e30
LOG2E = 1.4426950408889634
SIGN_BIT = 0x80000000

LANES = 128
VMEM_LIMIT = 48 * 1024 * 1024

MIX_A = N_HEADS_MLA * V_DIM
MIX_B = N_HEADS_DIL * HEAD_DIM
MIX_EVEN = MIX_A + MIX_B
MIX_ODD = N_HEADS_SB * HEAD_DIM
E_GATE = 0
E_QB = E_GATE + MIX_EVEN
E_KB = E_QB + MIX_B
E_VB = E_KB + MIX_B
E_CQ = E_VB + MIX_B
E_CKV = E_CQ + Q_LORA
E_KR = E_CKV + KV_LORA
E_END = E_KR + 2 * QK_ROPE
PROJ_TM = 1024
PROJ_TN = 1280
PROJ_TN_ODD = 1024
PREP_TM = 512
OUT_TM = 512
E_WIDTH = -(-E_END // PROJ_TN) * PROJ_TN
QK_PAD = 2 * LANES
V_PAD = 2 * LANES


def _cparams(sem):
    return pltpu.CompilerParams(dimension_semantics=sem, vmem_limit_bytes=VMEM_LIMIT)


def _bucket_tables():
    qi = np.arange(BLOCK)[:, None]
    kj = np.arange(2 * BLOCK)[None, :]
    rel = BLOCK + qi - kj
    max_exact = N_BUCKETS // 2
    tabs = []
    for window, dil in DIL_PATTERNS:
        span = window // dil
        dist = np.maximum(rel, 0) * dil
        d = np.maximum(dist.astype(np.float64), 1.0)
        frac = np.log(d / max_exact) / math.log(BUCKET_MAX_DIST / max_exact) * (N_BUCKETS - max_exact)
        large = np.minimum(max_exact + np.trunc(frac).astype(np.int64), N_BUCKETS - 1)
        bucket = np.where(dist < max_exact, dist, large)
        valid = (rel >= 0) & (rel <= span)
        tabs.append(np.where(valid, bucket, -1).astype(np.int32))
    return np.stack(tabs)


def _bias_tiles_kernel(bucket_ref, rel_bias_ref, o_ref):
    h = pl.program_id(1)
    bucket = bucket_ref[0]
    out = jnp.full(bucket.shape, MASK_VALUE, F32)
    for i in range(N_BUCKETS):
        out = jnp.where(bucket == i, rel_bias_ref[i, h] * LOG2E, out)
    o_ref[0, 0, 0] = out
    o_ref[0, 0, 1, :, :BLOCK] = out[:, BLOCK:]
    o_ref[0, 0, 1, :, BLOCK:] = jnp.full((BLOCK, BLOCK), MASK_VALUE, F32)


def _bias_tiles(rel_bias):
    n_pat = len(DIL_PATTERNS)
    buckets = jnp.asarray(_bucket_tables())
    return pl.pallas_call(
        _bias_tiles_kernel,
        out_shape=jax.ShapeDtypeStruct((n_pat, N_HEADS_DIL, 2, BLOCK, 2 * BLOCK), F32),
        grid=(n_pat, N_HEADS_DIL),
        in_specs=[pl.BlockSpec((1, BLOCK, 2 * BLOCK), lambda p, h: (p, 0, 0)),
                  pl.BlockSpec(memory_space=pltpu.SMEM)],
        out_specs=pl.BlockSpec((1, 1, 2, BLOCK, 2 * BLOCK), lambda p, h: (p, h, 0, 0, 0)),
        compiler_params=_cparams(("arbitrary", "arbitrary")),
        name="bias_tiles",
    )(buckets, rel_bias)


def _rms_proj_kernel(x_ref, g_ref, w_ref, cs_ref, o_ref, h_ref):
    @pl.when(pl.program_id(1) == 0)
    def _():
        x = x_ref[...]
        ms = jnp.mean(x * x, axis=-1, keepdims=True)
        h_ref[...] = (x * lax.rsqrt(ms + EPS) * g_ref[...]).astype(BF16)

    acc = jnp.dot(h_ref[...], w_ref[...], preferred_element_type=F32)
    o_ref[...] = (acc * cs_ref[...]).astype(o_ref.dtype)


def _rms_proj(x2d, gain, w, col_scale, tm, tn):
    t, d = x2d.shape
    n = w.shape[1]
    return pl.pallas_call(
        _rms_proj_kernel,
        out_shape=jax.ShapeDtypeStruct((t, n), BF16),
        grid=(t // tm, n // tn),
        in_specs=[pl.BlockSpec((tm, d), lambda i, j: (i, 0)),
                  pl.BlockSpec((1, d), lambda i, j: (0, 0)),
                  pl.BlockSpec((d, tn), lambda i, j: (0, j)),
                  pl.BlockSpec((1, tn), lambda i, j: (0, j))],
        out_specs=pl.BlockSpec((tm, tn), lambda i, j: (i, j)),
        scratch_shapes=[pltpu.VMEM((tm, d), BF16)],
        compiler_params=_cparams(("parallel", "arbitrary")),
        name="rms_proj",
    )(x2d, gain, w, col_scale)


def _rope_chunk(chunk, cs):
    r = chunk * cs
    r = r + pltpu.roll(r, QK_ROPE, axis=1)
    lane = lax.broadcasted_iota(jnp.int32, r.shape, 1)
    return jnp.where(lane < QK_ROPE, r, 0.0)


def _mla_prep_kernel(cq_ref, ckv_ref, kr_ref, cs_ref, gq_ref, gkv_ref, wuq_ref, wukv_ref,
                     q_ref, k_ref, v_ref):
    def latent_norm(c_ref, g_ref):
        c = c_ref[...].astype(F32)
        ms = jnp.mean(c * c, axis=-1, keepdims=True)
        return (c * lax.rsqrt(ms + EPS) * g_ref[...]).astype(BF16)

    cs = cs_ref[...]
    scale = LOG2E * (QK_NOPE + QK_ROPE) ** -0.5
    q = jnp.dot(latent_norm(cq_ref, gq_ref), wuq_ref[...], preferred_element_type=F32)
    for h in range(N_HEADS_MLA):
        lo = h * QK_PAD
        q_ref[:, lo:lo + LANES] = (q[:, lo:lo + LANES] * scale).astype(BF16)
        q_ref[:, lo + LANES:lo + QK_PAD] = (_rope_chunk(q[:, lo + LANES:lo + QK_PAD], cs) * scale).astype(BF16)

    k_rot = _rope_chunk(kr_ref[...].astype(F32), cs).astype(BF16)
    ones_col = jnp.ones((cs.shape[0], V_PAD - V_DIM), BF16)
    kv = jnp.dot(latent_norm(ckv_ref, gkv_ref), wukv_ref[...], preferred_element_type=F32)
    for h in range(N_HEADS_MLA):
        lo = h * (QK_NOPE + V_DIM)
        k_ref[:, h * QK_PAD:h * QK_PAD + LANES] = kv[:, lo:lo + QK_NOPE].astype(BF16)
        k_ref[:, h * QK_PAD + LANES:(h + 1) * QK_PAD] = k_rot
        v_ref[:, h * V_PAD:h * V_PAD + V_DIM] = kv[:, lo + QK_NOPE:lo + QK_NOPE + V_DIM].astype(BF16)
        v_ref[:, h * V_PAD + V_DIM:(h + 1) * V_PAD] = ones_col


def _mla_prep(proj, cos_sin, gq, gkv, wuq, wukv, seq, tm):
    t = proj.shape[0]
    n_seq_tiles = seq // tm
    const = lambda i: (0, 0)
    return pl.pallas_call(
        _mla_prep_kernel,
        out_shape=(jax.ShapeDtypeStruct((t, N_HEADS_MLA * QK_PAD), BF16),
                   jax.ShapeDtypeStruct((t, N_HEADS_MLA * QK_PAD), BF16),
                   jax.ShapeDtypeStruct((t, N_HEADS_MLA * V_PAD), BF16)),
        grid=(t // tm,),
        in_specs=[pl.BlockSpec((tm, Q_LORA), lambda i: (i, E_CQ // Q_LORA)),
                  pl.BlockSpec((tm, KV_LORA), lambda i: (i, E_CKV // KV_LORA)),
                  pl.BlockSpec((tm, LANES), lambda i: (i, E_KR // LANES)),
                  pl.BlockSpec((tm, LANES), lambda i: (i % n_seq_tiles, 0)),
                  pl.BlockSpec((1, Q_LORA), const),
                  pl.BlockSpec((1, KV_LORA), const),
                  pl.BlockSpec(wuq.shape, const),
                  pl.BlockSpec(wukv.shape, const)],
        out_specs=(pl.BlockSpec((tm, N_HEADS_MLA * QK_PAD), lambda i: (i, 0)),
                   pl.BlockSpec((tm, N_HEADS_MLA * QK_PAD), lambda i: (i, 0)),
                   pl.BlockSpec((tm, N_HEADS_MLA * V_PAD), lambda i: (i, 0))),
        compiler_params=_cparams(("parallel",)),
        name="mla_prep",
    )(proj, proj, proj, cos_sin, gq, gkv, wuq, wukv)


MLA_TQ = 1024
MLA_TK = 1024
MLA_SLAB = 256


def _mla_attn_kernel(q_ref, k_ref, v_ref, o_ref, m_ref, acc_ref):
    n_sub = MLA_TQ // MLA_TK

    def step(sup, row_lo, kb, masked):
        start = pl.multiple_of(kb * MLA_TK, MLA_TK)
        k = k_ref[pl.ds(start, MLA_TK), :]
        v = v_ref[pl.ds(start, MLA_TK), :]
        slabs = [slice(r, r + MLA_SLAB) for r in range(row_lo, MLA_TQ, MLA_SLAB)]
        n_keys = [min(rows.stop - row_lo, MLA_TK) if masked else MLA_TK for rows in slabs]
        s = []
        for rows, nk in zip(slabs, n_keys):
            q = q_ref[pl.ds(pl.multiple_of(sup * MLA_TQ + rows.start, MLA_SLAB), MLA_SLAB), :]
            si = lax.dot_general(q, k[:nk], (((1,), (1,)), ((), ())), preferred_element_type=F32)
            if masked and rows.start - row_lo < MLA_TK:
                row = lax.broadcasted_iota(jnp.int32, si.shape, 0) + (rows.start - row_lo)
                col = lax.broadcasted_iota(jnp.int32, si.shape, 1)
                si = jnp.where(col <= row, si, MASK_VALUE)
            s.append(si)
        p, alpha, m_new = [], [], []
        for rows, si, nk in zip(slabs, s, n_keys):
            m_old = m_ref[rows, :]
            mi = jnp.maximum(m_old, jnp.max(si, axis=-1, keepdims=True))
            alpha.append(jnp.exp2(m_old - mi))
            p.append(jnp.exp2(si - jnp.concatenate([mi] * (nk // LANES), axis=1)).astype(BF16))
            m_new.append(mi)
        pv = [jnp.dot(pi, v[:nk], preferred_element_type=F32) for pi, nk in zip(p, n_keys)]
        for rows, ai, mi, pvi in zip(slabs, alpha, m_new, pv):
            acc_ref[rows, :] = jnp.concatenate([ai] * (V_PAD // LANES), axis=1) * acc_ref[rows, :] + pvi
            m_ref[rows, :] = mi

    def query_tile(sup, c):
        m_ref[...] = jnp.full(m_ref.shape, -jnp.inf, F32)
        acc_ref[...] = jnp.zeros(acc_ref.shape, F32)
        for b in range(n_sub):
            step(sup, b * MLA_TK, sup * n_sub + b, True)

        def body(j, carry):
            step(sup, 0, j, False)
            return carry

        lax.fori_loop(0, sup * n_sub, body, 0)
        acc = acc_ref[...]
        o_ref[pl.ds(pl.multiple_of(sup * MLA_TQ, MLA_TQ), MLA_TQ), :] = (
            acc[:, :V_DIM] / acc[:, V_DIM:]).astype(o_ref.dtype)
        return c

    lax.fori_loop(0, q_ref.shape[0] // MLA_TQ, query_tile, 0)


def _mla_attn(q, k, v, batch, seq):
    return pl.pallas_call(
        _mla_attn_kernel,
        out_shape=jax.ShapeDtypeStruct((batch * seq, MIX_A), BF16),
        grid=(batch, N_HEADS_MLA),
        in_specs=[pl.BlockSpec((seq, QK_PAD), lambda b, h: (b, h)),
                  pl.BlockSpec((seq, QK_PAD), lambda b, h: (b, h)),
                  pl.BlockSpec((seq, V_PAD), lambda b, h: (b, h))],
        out_specs=pl.BlockSpec((seq, V_DIM), lambda b, h: (b, h)),
        scratch_shapes=[pltpu.VMEM((MLA_TQ, LANES), F32), pltpu.VMEM((MLA_TQ, V_PAD), F32)],
        compiler_params=_cparams(("parallel", "parallel")),
        name="mla_attn",
    )(q, k, v)


DIL_GROUP = 8


def _dilated_kernel(q_ref, k_ref, v_ref, bias_ref, o_ref, nat_m, nat_l, nat_acc, q4_ref, k4_ref, v4_ref,
                    m_ref, l_ref, acc_ref, *, seq):
    quarter = seq // 4

    def by4(start, n):
        return pl.ds(start, n, stride=4)

    for src, tmp, dst in ((q_ref, nat_m, q4_ref), (k_ref, nat_l, k4_ref), (v_ref, nat_acc, v4_ref)):
        tmp[...] = src[...].astype(F32)
        for r4 in range(4):
            dst[r4 * quarter:(r4 + 1) * quarter, :] = tmp[by4(r4, quarter), :]

    def attend(blocks):
        s = [lax.dot_general(q.astype(BF16), k.astype(BF16), (((1,), (1,)), ((), ())),
                             preferred_element_type=F32) + bias for q, k, _, bias in blocks]
        m_p = [jnp.broadcast_to(jnp.max(si, axis=-1, keepdims=True), (BLOCK, LANES)) for si in s]
        p = [jnp.exp2(si - jnp.concatenate([mi, mi], axis=1)).astype(BF16) for si, mi in zip(s, m_p)]
        ones = jnp.ones((2 * BLOCK, LANES), BF16)
        num_l = [jnp.dot(pi, jnp.concatenate([v.astype(BF16), ones], axis=1), preferred_element_type=F32)
                 for pi, (_, _, v, _) in zip(p, blocks)]
        return [(mi, nl[:, HEAD_DIM:], nl[:, :HEAD_DIM]) for mi, nl in zip(m_p, num_l)]

    def merge(old, new):
        m_old, l_old, acc_old = old
        m_p, l_p, num_p = new
        m_new = jnp.maximum(m_old, m_p)
        a = jnp.exp2(m_old - m_new)
        b = jnp.exp2(m_p - m_new)
        return m_new, l_old * a + l_p * b, acc_old * a + num_p * b

    def bias_of(pat, n):
        return bias_ref[pat, 0, jnp.where(n == 0, 1, 0)]

    def prev_block(n):
        return jnp.maximum(n - 1, 0)

    def pass_natural(g, carry):
        where, blocks = [], []
        for u in range(DIL_GROUP):
            n = g * DIL_GROUP + u
            q_at = pl.ds(pl.multiple_of(n * BLOCK, BLOCK), BLOCK)
            k_at = pl.ds(pl.multiple_of(prev_block(n) * BLOCK, BLOCK), 2 * BLOCK)
            where.append(q_at)
            blocks.append((q_ref[q_at, :], k_ref[k_at, :], v_ref[k_at, :], bias_of(0, n)))
        for q_at, (m_p, l_p, num_p) in zip(where, attend(blocks)):
            nat_m[q_at, :] = m_p
            nat_l[q_at, :] = l_p
            nat_acc[q_at, :] = num_p
        return carry

    def pass_dil4(g, carry):
        where, blocks, old = [], [], []
        for u in range(DIL_GROUP):
            r4 = u % 4
            n = g * (DIL_GROUP // 4) + u // 4
            q_at = pl.ds(pl.multiple_of(r4 * quarter + n * BLOCK, BLOCK), BLOCK)
            k_at = pl.ds(pl.multiple_of(r4 * quarter + prev_block(n) * BLOCK, BLOCK), 2 * BLOCK)
            nat_at = by4(r4 + n * (4 * BLOCK), BLOCK)
            where.append(q_at)
            blocks.append((q4_ref[q_at, :], k4_ref[k_at, :], v4_ref[k_at, :], bias_of(1, n)))
            old.append((nat_m[nat_at, :], nat_l[nat_at, :], nat_acc[nat_at, :]))
        for q_at, o, new in zip(where, old, attend(blocks)):
            m_ref[q_at, :], l_ref[q_at, :], acc_ref[q_at, :] = merge(o, new)
        return carry

    def pass_dil16(g, carry):
        where, blocks, old = [], [], []
        for u in range(DIL_GROUP):
            unit = g * DIL_GROUP + u
            r = lax.rem(unit, 16)
            n = lax.div(unit, 16)
            row0 = lax.rem(r, 4) * quarter + lax.div(r, 4)
            q_at = by4(row0 + n * (4 * BLOCK), BLOCK)
            k_at = by4(row0 + prev_block(n) * (4 * BLOCK), 2 * BLOCK)
            where.append(q_at)
            blocks.append((q4_ref[q_at, :], k4_ref[k_at, :], v4_ref[k_at, :], bias_of(2, n)))
            old.append((m_ref[q_at, :], l_ref[q_at, :], acc_ref[q_at, :]))
        for q_at, o, new in zip(where, old, attend(blocks)):
            m_ref[q_at, :], l_ref[q_at, :], acc_ref[q_at, :] = merge(o, new)
        return carry

    n_units = seq // BLOCK
    lax.fori_loop(0, n_units // DIL_GROUP, pass_natural, 0)
    lax.fori_loop(0, n_units // DIL_GROUP, pass_dil4, 0)
    lax.fori_loop(0, n_units // DIL_GROUP, pass_dil16, 0)

    for r4 in range(4):
        rows = slice(r4 * quarter, (r4 + 1) * quarter)
        nat_acc[by4(r4, quarter), :] = acc_ref[rows, :] / l_ref[rows, :]
    o_ref[...] = nat_acc[...].astype(o_ref.dtype)


def _dilated_attn(proj, bias, batch, seq):
    n_pat = len(DIL_PATTERNS)
    assert all(seq // dil // BLOCK >= 2 for _, dil in DIL_PATTERNS), "every residue class needs two key blocks"
    assert tuple(dil for _, dil in DIL_PATTERNS) == (1, 4, 16) and DIL_GROUP % 4 == 0
    col = lambda off: (lambda b, h: (b, off // HEAD_DIM + h))
    return pl.pallas_call(
        functools.partial(_dilated_kernel, seq=seq),
        out_shape=jax.ShapeDtypeStruct((batch * seq, MIX_B), BF16),
        grid=(batch, N_HEADS_DIL),
        in_specs=[pl.BlockSpec((seq, HEAD_DIM), col(E_QB)),
                  pl.BlockSpec((seq, HEAD_DIM), col(E_KB)),
                  pl.BlockSpec((seq, HEAD_DIM), col(E_VB)),
                  pl.BlockSpec((n_pat, 1, 2, BLOCK, 2 * BLOCK), lambda b, h: (0, h, 0, 0, 0))],
        out_specs=pl.BlockSpec((seq, HEAD_DIM), lambda b, h: (b, h)),
        scratch_shapes=[pltpu.VMEM((seq, HEAD_DIM), F32)] * 9,
        compiler_params=_cparams(("parallel", "parallel")),
        name="dilated_attn",
    )(proj, proj, proj, bias)


def _gate_out_kernel(*refs, n_mixers, final_norm):
    mixer_refs = refs[:n_mixers]
    gate_ref, x_ref, w_ref, gf_ref, o_ref = refs[n_mixers:]
    g = gate_ref[...].astype(F32)
    sg = g * jax.nn.sigmoid(g)
    y = x_ref[...]
    lo = 0
    for m_ref in mixer_refs:
        hi = lo + m_ref.shape[1]
        mix = (m_ref[...].astype(F32) * sg[:, lo:hi]).astype(BF16)
        y = y + jnp.dot(mix, w_ref[lo:hi, :], preferred_element_type=F32)
        lo = hi
    if final_norm:
        ms = jnp.mean(y * y, axis=-1, keepdims=True)
        y = y * lax.rsqrt(ms + EPS) * gf_ref[...]
    o_ref[...] = y


def _gate_out_proj(mixers, proj, gate_blk, x2d, w, gf, final_norm, tm):
    t, d = x2d.shape
    mix = w.shape[0]
    assert sum(m.shape[1] for m in mixers) == mix
    return pl.pallas_call(
        functools.partial(_gate_out_kernel, n_mixers=len(mixers), final_norm=final_norm),
        out_shape=jax.ShapeDtypeStruct((t, d), F32),
        grid=(t // tm,),
        in_specs=[pl.BlockSpec((tm, m.shape[1]), lambda i: (i, 0)) for m in mixers] + [
            pl.BlockSpec((tm, mix), lambda i: (i, gate_blk)),
            pl.BlockSpec((tm, d), lambda i: (i, 0)),
            pl.BlockSpec(w.shape, lambda i: (0, 0)),
            pl.BlockSpec((1, d), lambda i: (0, 0))],
        out_specs=pl.BlockSpec((tm, d), lambda i: (i, 0)),
        compiler_params=_cparams(("parallel",)),
        name="gate_out_proj",
    )(*mixers, proj, x2d, w, gf)


SB_TK = 256
SB_CHAINS = 4
SB_TQ = SB_CHAINS * SB_TK
SB_DEAD = -160.0


def _sb_attn_kernel(q_ref, k_ref, v_ref, u_ref, o_ref, acc_ref, carry_ref):
    def step(base, tiles):
        n = len(tiles)
        rows = [slice(c * SB_TK, (c + 1) * SB_TK) for c, _, _ in tiles]
        kv_at = [pl.ds(pl.multiple_of(kb * SB_TK, SB_TK), SB_TK) for _, kb, _ in tiles]
        nz = []
        for i, (c, _, _) in enumerate(tiles):
            q = q_ref[pl.ds(pl.multiple_of((base + c) * SB_TK, SB_TK), SB_TK), :]
            nz.append(lax.dot_general(q, k_ref[kv_at[i], :], (((1,), (1,)), ((), ())),
                                      preferred_element_type=F32))
        row = lax.broadcasted_iota(jnp.int32, (SB_TK, SB_TK), 0)
        col = lax.broadcasted_iota(jnp.int32, (SB_TK, SB_TK), 1)
        before = col < row
        hi_lo = []
        for i, (_, _, masked) in enumerate(tiles):
            neg_abs = lax.bitcast_convert_type(
                lax.bitcast_convert_type(nz[i], jnp.uint32) | jnp.uint32(SIGN_BIT), F32)
            log_rest = jnp.minimum(nz[i], 0.0) - jnp.log(1.0 + jnp.exp2(neg_abs)) * LOG2E
            if masked:
                log_rest = jnp.where(before, log_rest, 0.0)
            hi = log_rest.astype(BF16)
            lo = (log_rest - hi.astype(F32)).astype(BF16)
            hi_lo.append(jnp.concatenate([hi, lo], axis=1))
        cum_all = jnp.dot(jnp.concatenate(hi_lo, axis=0), u_ref[...], preferred_element_type=F32)
        weights, carries = [], []
        for i, (_, _, masked) in enumerate(tiles):
            cum = cum_all[i * SB_TK:(i + 1) * SB_TK]
            carry = carry_ref[rows[i], :]
            a = jnp.exp2(cum + carry - nz[i])
            if masked:
                a = jnp.where(before, a, 0.0)
            weights.append(a.astype(BF16))
            carries.append(carry + cum[:, 0:1])
        accs = [acc_ref[rows[i], :] + jnp.dot(weights[i], v_ref[kv_at[i], :], preferred_element_type=F32)
                for i in range(n)]
        for i in range(n):
            acc_ref[rows[i], :] = accs[i]
            carry_ref[rows[i], :] = carries[i]

    def alive(first_chain=0):
        return jnp.max(carry_ref[first_chain * SB_TK:, :]) > SB_DEAD

    def group(g, carry_unused):
        base = g * SB_CHAINS
        acc_ref[...] = jnp.zeros(acc_ref.shape, F32)
        carry_ref[...] = jnp.zeros(carry_ref.shape, F32)
        step(base, [(c, base + c, True) for c in range(SB_CHAINS)])

        def cond(state):
            t, go = state
            return jnp.logical_and(t <= base, go)

        def body(state):
            t, _ = state
            step(base, [(c, base + c - t, False) for c in range(SB_CHAINS)])
            return t + 1, alive()

        lax.while_loop(cond, body, (jnp.int32(1), alive()))
        for j in range(1, SB_CHAINS):
            @pl.when(alive(first_chain=j))
            def _():
                step(base, [(c, c - j, False) for c in range(j, SB_CHAINS)])

        o_ref[pl.ds(pl.multiple_of(base * SB_TK, SB_TQ), SB_TQ), :] = acc_ref[...].astype(o_ref.dtype)
        return carry_unused

    lax.fori_loop(0, q_ref.shape[0] // SB_TQ, group, 0)


def _sb_attn(proj, batch, seq):
    row = np.arange(SB_TK)[:, None]
    colv = np.arange(SB_TK)[None, :]
    tri = (row >= colv).astype(np.float32)
    u = jnp.asarray(np.concatenate([tri, tri], axis=0), BF16)
    hcol = lambda off: (lambda b, h: (b, off // HEAD_DIM + h))
    return pl.pallas_call(
        _sb_attn_kernel,
        out_shape=jax.ShapeDtypeStruct((batch * seq, MIX_ODD), BF16),
        grid=(batch, N_HEADS_SB),
        in_specs=[pl.BlockSpec((seq, HEAD_DIM), hcol(0)),
                  pl.BlockSpec((seq, HEAD_DIM), hcol(MIX_ODD)),
                  pl.BlockSpec((seq, HEAD_DIM), hcol(2 * MIX_ODD)),
                  pl.BlockSpec((2 * SB_TK, SB_TK), lambda b, h: (0, 0))],
        out_specs=pl.BlockSpec((seq, HEAD_DIM), lambda b, h: (b, h)),
        scratch_shapes=[pltpu.VMEM((SB_TQ, HEAD_DIM), F32), pltpu.VMEM((SB_TQ, 1), F32)],
        compiler_params=_cparams(("parallel", "parallel")),
        name="sb_attn",
    )(proj, proj, proj, u)


def _rotate_half_partner(w):
    half = QK_ROPE // 2
    return jnp.concatenate([-w[..., half:], w[..., :half]], axis=-1)


def _even_weight_plan():
    half = LANES // 2
    src = np.cumsum((0, Q_LORA, KV_LORA, QK_ROPE, MIX_B, MIX_B, MIX_B))
    a_idx, b_idx, mode = [], [], []
    for name, width in (("gate", MIX_EVEN), ("q_b", MIX_B), ("k_b", MIX_B), ("v_b", MIX_B),
                        ("c_q", Q_LORA), ("c_kv", KV_LORA), ("k_rope", 2 * QK_ROPE)):
        start = int(src[("c_q", "c_kv", "k_rope", "q_b", "k_b", "v_b", "gate").index(name)])
        for col in range(start, start + width, LANES):
            if name == "k_rope":
                a, b, m = col // LANES, col // LANES, 2
            elif col % LANES == 0:
                a, b, m = col // LANES, col // LANES, 1
            else:
                assert col % LANES == half
                a, b, m = col // LANES, col // LANES + 1, 0
            a_idx.append(a), b_idx.append(b), mode.append(m)
    n_pad = (E_WIDTH - E_END) // LANES
    a_idx += [0] * n_pad
    b_idx += [0] * n_pad
    mode += [3] * n_pad
    return tuple(np.asarray(t, np.int32) for t in (a_idx, b_idx, mode))


def _even_weight_kernel(a_idx, b_idx, mode, a_ref, b_ref, o_ref):
    del a_idx, b_idx
    half = LANES // 2
    quarter = QK_ROPE // 2
    md = mode[pl.program_id(0)]
    lane = lax.broadcasted_iota(jnp.int32, a_ref.shape, 1)

    @pl.when(md == 0)
    def _():
        o_ref[...] = jnp.where(lane < half, pltpu.roll(a_ref[...], half, axis=1),
                               pltpu.roll(b_ref[...], half, axis=1)).astype(o_ref.dtype)

    @pl.when(md == 1)
    def _():
        o_ref[...] = a_ref[...].astype(o_ref.dtype)

    @pl.when(md == 2)
    def _():
        a = a_ref[...]
        partner = jnp.where(lane < half + quarter, -pltpu.roll(a, quarter, axis=1),
                            pltpu.roll(a, half + quarter, axis=1))
        o_ref[...] = jnp.where(lane < half, a, partner).astype(o_ref.dtype)

    @pl.when(md == 3)
    def _():
        o_ref[...] = jnp.zeros(o_ref.shape, o_ref.dtype)


def _even_in_weight(w_in):
    d = w_in.shape[0]
    assert QK_ROPE == LANES // 2
    a_idx, b_idx, mode = _even_weight_plan()
    return pl.pallas_call(
        _even_weight_kernel,
        out_shape=jax.ShapeDtypeStruct((d, E_WIDTH), BF16),
        grid_spec=pltpu.PrefetchScalarGridSpec(
            num_scalar_prefetch=3,
            grid=(E_WIDTH // LANES,),
            in_specs=[pl.BlockSpec((d, LANES), lambda j, a, b, m: (0, a[j])),
                      pl.BlockSpec((d, LANES), lambda j, a, b, m: (0, b[j]))],
            out_specs=pl.BlockSpec((d, LANES), lambda j, a, b, m: (0, j))),
        compiler_params=_cparams(("arbitrary",)),
        name="even_weight",
    )(jnp.asarray(a_idx), jnp.asarray(b_idx), jnp.asarray(mode), w_in, w_in)


def _uq_weight(w_uq):
    w = w_uq.astype(BF16).reshape(Q_LORA, N_HEADS_MLA, QK_NOPE + QK_ROPE)
    rot = w[..., QK_NOPE:]
    w = jnp.concatenate([w[..., :QK_NOPE], rot, _rotate_half_partner(rot)], axis=-1)
    return w.reshape(Q_LORA, N_HEADS_MLA * QK_PAD)


def _col_scale(width, start, stop, scale):
    col = np.ones((1, width), np.float32)
    col[:, start:stop] = scale
    return jnp.asarray(col)


def _cos_sin_table(seq):
    half = QK_ROPE // 2
    inv = 1.0 / (ROPE_THETA ** (jnp.arange(half, dtype=F32) / half))
    ang = jnp.arange(seq).astype(F32)[:, None] * inv[None, :]
    cos, sin = jnp.cos(ang), jnp.sin(ang)
    return jnp.concatenate([cos, cos, sin, sin], axis=-1)


def kernel(x, norm_gain, w_in_even, q_norm_gain, kv_norm_gain, w_uq, w_ukv, w_out_even, rel_bias,
           w_in_odd, w_out_odd, final_norm_gain):
    batch, seq, d_model = x.shape
    t = batch * seq
    x2d = x.reshape(t, d_model)
    head_scale = HEAD_DIM ** -0.5

    proj0 = _rms_proj(x2d, norm_gain[0:1], _even_in_weight(w_in_even[0]),
                      _col_scale(E_WIDTH, E_QB, E_KB, head_scale * LOG2E), tm=PROJ_TM, tn=PROJ_TN)
    q_a, k_a, v_a = _mla_prep(proj0, _cos_sin_table(seq), q_norm_gain[0:1], kv_norm_gain[0:1],
                              _uq_weight(w_uq[0]), w_ukv[0].astype(BF16), seq, tm=PREP_TM)
    o_a = _mla_attn(q_a, k_a, v_a, batch, seq)
    o_b = _dilated_attn(proj0, _bias_tiles(rel_bias), batch, seq)
    x1 = _gate_out_proj([o_a, o_b], proj0, E_GATE // MIX_EVEN, x2d, w_out_even[0].astype(BF16),
                        final_norm_gain[None, :], final_norm=False, tm=OUT_TM)

    proj1 = _rms_proj(x1, norm_gain[1:2], w_in_odd[0].astype(BF16),
                      _col_scale(4 * MIX_ODD, 0, MIX_ODD, -head_scale * LOG2E), tm=PROJ_TM, tn=PROJ_TN_ODD)
    o_c = _sb_attn(proj1, batch, seq)
    out = _gate_out_proj([o_c], proj1, 3, x1, w_out_odd[0].astype(BF16),
                         final_norm_gain[None, :], final_norm=True, tm=OUT_TM)
    return out.reshape(batch, seq, d_model)
```

```python
import functools
import math

import numpy as np
import jax
import jax.numpy as jnp
from jax import lax
from jax.experimental import pallas as pl
from jax.experimental.pallas import tpu as pltpu

F32 = jnp.float32
BF16 = jnp.bfloat16

EPS = 1e-6
N_HEADS_MLA = 8
Q_LORA = 512
KV_LORA = 512
QK_NOPE = 128
QK_ROPE = 64
V_DIM = 128
ROPE_THETA = 10000.0
N_HEADS_DIL = 8
HEAD_DIM = 128
DIL_PATTERNS = ((128, 1), (512, 4), (2048, 16))
N_BUCKETS = 32
BUCKET_MAX_DIST = 2048
N_HEADS_SB = 16
BLOCK = 128
MASK_VALUE = -1e30
LOG2E = 1.4426950408889634
SIGN_BIT = 0x80000000

LANES = 128
VMEM_LIMIT = 48 * 1024 * 1024

MIX_A = N_HEADS_MLA * V_DIM
MIX_B = N_HEADS_DIL * HEAD_DIM
MIX_EVEN = MIX_A + MIX_B
MIX_ODD = N_HEADS_SB * HEAD_DIM
E_GATE = 0
E_QB = E_GATE + MIX_EVEN
E_KB = E_QB + MIX_B
E_VB = E_KB + MIX_B
E_CQ = E_VB + MIX_B
E_CKV = E_CQ + Q_LORA
E_KR = E_CKV + KV_LORA
E_END = E_KR + 2 * QK_ROPE
PROJ_TM = 1024
PROJ_TN = 1280
PROJ_TN_ODD = 1024
PREP_TM = 512
OUT_TM = 512
E_WIDTH = -(-E_END // PROJ_TN) * PROJ_TN
QK_PAD = 2 * LANES
V_PAD = 2 * LANES


def _cparams(sem):
    return pltpu.CompilerParams(dimension_semantics=sem, vmem_limit_bytes=VMEM_LIMIT)


def _bucket_tables():
    qi = np.arange(BLOCK)[:, None]
    kj = np.arange(2 * BLOCK)[None, :]
    rel = BLOCK + qi - kj
    max_exact = N_BUCKETS // 2
    tabs = []
    for window, dil in DIL_PATTERNS:
        span = window // dil
        dist = np.maximum(rel, 0) * dil
        d = np.maximum(dist.astype(np.float64), 1.0)
        frac = np.log(d / max_exact) / math.log(BUCKET_MAX_DIST / max_exact) * (N_BUCKETS - max_exact)
        large = np.minimum(max_exact + np.trunc(frac).astype(np.int64), N_BUCKETS - 1)
        bucket = np.where(dist < max_exact, dist, large)
        valid = (rel >= 0) & (rel <= span)
        tabs.append(np.where(valid, bucket, -1).astype(np.int32))
    return np.stack(tabs)


def _bias_tiles_kernel(bucket_ref, rel_bias_ref, o_ref):
    h = pl.program_id(1)
    bucket = bucket_ref[0]
    out = jnp.full(bucket.shape, MASK_VALUE, F32)
    for i in range(N_BUCKETS):
        out = jnp.where(bucket == i, rel_bias_ref[i, h] * LOG2E, out)
    o_ref[0, 0, 0] = out
    o_ref[0, 0, 1, :, :BLOCK] = out[:, BLOCK:]
    o_ref[0, 0, 1, :, BLOCK:] = jnp.full((BLOCK, BLOCK), MASK_VALUE, F32)


def _bias_tiles(rel_bias):
    n_pat = len(DIL_PATTERNS)
    buckets = jnp.asarray(_bucket_tables())
    return pl.pallas_call(
        _bias_tiles_kernel,
        out_shape=jax.ShapeDtypeStruct((n_pat, N_HEADS_DIL, 2, BLOCK, 2 * BLOCK), F32),
        grid=(n_pat, N_HEADS_DIL),
        in_specs=[pl.BlockSpec((1, BLOCK, 2 * BLOCK), lambda p, h: (p, 0, 0)),
                  pl.BlockSpec(memory_space=pltpu.SMEM)],
        out_specs=pl.BlockSpec((1, 1, 2, BLOCK, 2 * BLOCK), lambda p, h: (p, h, 0, 0, 0)),
        compiler_params=_cparams(("arbitrary", "arbitrary")),
        name="bias_tiles",
    )(buckets, rel_bias)


def _rms_proj_kernel(x_ref, g_ref, w_ref, cs_ref, o_ref, h_ref):
    @pl.when(pl.program_id(1) == 0)
    def _():
        x = x_ref[...]
        ms = jnp.mean(x * x, axis=-1, keepdims=True)
        h_ref[...] = (x * lax.rsqrt(ms + EPS) * g_ref[...]).astype(BF16)

    acc = jnp.dot(h_ref[...], w_ref[...], preferred_element_type=F32)
    o_ref[...] = (acc * cs_ref[...]).astype(o_ref.dtype)


def _rms_proj(x2d, gain, w, col_scale, tm, tn):
    t, d = x2d.shape
    n = w.shape[1]
    return pl.pallas_call(
        _rms_proj_kernel,
        out_shape=jax.ShapeDtypeStruct((t, n), BF16),
        grid=(t // tm, n // tn),
        in_specs=[pl.BlockSpec((tm, d), lambda i, j: (i, 0)),
                  pl.BlockSpec((1, d), lambda i, j: (0, 0)),
                  pl.BlockSpec((d, tn), lambda i, j: (0, j)),
                  pl.BlockSpec((1, tn), lambda i, j: (0, j))],
        out_specs=pl.BlockSpec((tm, tn), lambda i, j: (i, j)),
        scratch_shapes=[pltpu.VMEM((tm, d), BF16)],
        compiler_params=_cparams(("parallel", "arbitrary")),
        name="rms_proj",
    )(x2d, gain, w, col_scale)


def _rope_chunk(chunk, cs):
    r = chunk * cs
    r = r + pltpu.roll(r, QK_ROPE, axis=1)
    lane = lax.broadcasted_iota(jnp.int32, r.shape, 1)
    return jnp.where(lane < QK_ROPE, r, 0.0)


def _mla_prep_kernel(cq_ref, ckv_ref, kr_ref, cs_ref, gq_ref, gkv_ref, wuq_ref, wukv_ref,
                     q_ref, k_ref, v_ref):
    def latent_norm(c_ref, g_ref):
        c = c_ref[...].astype(F32)
        ms = jnp.mean(c * c, axis=-1, keepdims=True)
        return (c * lax.rsqrt(ms + EPS) * g_ref[...]).astype(BF16)

    cs = cs_ref[...]
    scale = LOG2E * (QK_NOPE + QK_ROPE) ** -0.5
    q = jnp.dot(latent_norm(cq_ref, gq_ref), wuq_ref[...], preferred_element_type=F32)
    for h in range(N_HEADS_MLA):
        lo = h * QK_PAD
        q_ref[:, lo:lo + LANES] = (q[:, lo:lo + LANES] * scale).astype(BF16)
        q_ref[:, lo + LANES:lo + QK_PAD] = (_rope_chunk(q[:, lo + LANES:lo + QK_PAD], cs) * scale).astype(BF16)

    k_rot = _rope_chunk(kr_ref[...].astype(F32), cs).astype(BF16)
    ones_col = jnp.ones((cs.shape[0], V_PAD - V_DIM), BF16)
    kv = jnp.dot(latent_norm(ckv_ref, gkv_ref), wukv_ref[...], preferred_element_type=F32)
    for h in range(N_HEADS_MLA):
        lo = h * (QK_NOPE + V_DIM)
        k_ref[:, h * QK_PAD:h * QK_PAD + LANES] = kv[:, lo:lo + QK_NOPE].astype(BF16)
        k_ref[:, h * QK_PAD + LANES:(h + 1) * QK_PAD] = k_rot
        v_ref[:, h * V_PAD:h * V_PAD + V_DIM] = kv[:, lo + QK_NOPE:lo + QK_NOPE + V_DIM].astype(BF16)
        v_ref[:, h * V_PAD + V_DIM:(h + 1) * V_PAD] = ones_col


def _mla_prep(proj, cos_sin, gq, gkv, wuq, wukv, seq, tm):
    t = proj.shape[0]
    n_seq_tiles = seq // tm
    const = lambda i: (0, 0)
    return pl.pallas_call(
        _mla_prep_kernel,
        out_shape=(jax.ShapeDtypeStruct((t, N_HEADS_MLA * QK_PAD), BF16),
                   jax.ShapeDtypeStruct((t, N_HEADS_MLA * QK_PAD), BF16),
                   jax.ShapeDtypeStruct((t, N_HEADS_MLA * V_PAD), BF16)),
        grid=(t // tm,),
        in_specs=[pl.BlockSpec((tm, Q_LORA), lambda i: (i, E_CQ // Q_LORA)),
                  pl.BlockSpec((tm, KV_LORA), lambda i: (i, E_CKV // KV_LORA)),
                  pl.BlockSpec((tm, LANES), lambda i: (i, E_KR // LANES)),
                  pl.BlockSpec((tm, LANES), lambda i: (i % n_seq_tiles, 0)),
                  pl.BlockSpec((1, Q_LORA), const),
                  pl.BlockSpec((1, KV_LORA), const),
                  pl.BlockSpec(wuq.shape, const),
                  pl.BlockSpec(wukv.shape, const)],
        out_specs=(pl.BlockSpec((tm, N_HEADS_MLA * QK_PAD), lambda i: (i, 0)),
                   pl.BlockSpec((tm, N_HEADS_MLA * QK_PAD), lambda i: (i, 0)),
                   pl.BlockSpec((tm, N_HEADS_MLA * V_PAD), lambda i: (i, 0))),
        compiler_params=_cparams(("parallel",)),
        name="mla_prep",
    )(proj, proj, proj, cos_sin, gq, gkv, wuq, wukv)


MLA_TQ = 1024
MLA_TK = 1024
MLA_SLAB = 256


def _mla_attn_kernel(q_ref, k_ref, v_ref, o_ref, m_ref, acc_ref):
    n_sub = MLA_TQ // MLA_TK

    def step(sup, row_lo, kb, masked):
        start = pl.multiple_of(kb * MLA_TK, MLA_TK)
        k = k_ref[pl.ds(start, MLA_TK), :]
        v = v_ref[pl.ds(start, MLA_TK), :]
        slabs = [slice(r, r + MLA_SLAB) for r in range(row_lo, MLA_TQ, MLA_SLAB)]
        n_keys = [min(rows.stop - row_lo, MLA_TK) if masked else MLA_TK for rows in slabs]
        s = []
        for rows, nk in zip(slabs, n_keys):
            q = q_ref[pl.ds(pl.multiple_of(sup * MLA_TQ + rows.start, MLA_SLAB), MLA_SLAB), :]
            si = lax.dot_general(q, k[:nk], (((1,), (1,)), ((), ())), preferred_element_type=F32)
            if masked and rows.start - row_lo < MLA_TK:
                row = lax.broadcasted_iota(jnp.int32, si.shape, 0) + (rows.start - row_lo)
                col = lax.broadcasted_iota(jnp.int32, si.shape, 1)
                si = jnp.where(col <= row, si, MASK_VALUE)
            s.append(si)
        p, alpha, m_new = [], [], []
        for rows, si, nk in zip(slabs, s, n_keys):
            m_old = m_ref[rows, :]
            mi = jnp.maximum(m_old, jnp.max(si, axis=-1, keepdims=True))
            alpha.append(jnp.exp2(m_old - mi))
            p.append(jnp.exp2(si - jnp.concatenate([mi] * (nk // LANES), axis=1)).astype(BF16))
            m_new.append(mi)
        pv = [jnp.dot(pi, v[:nk], preferred_element_type=F32) for pi, nk in zip(p, n_keys)]
        for rows, ai, mi, pvi in zip(slabs, alpha, m_new, pv):
            acc_ref[rows, :] = jnp.concatenate([ai] * (V_PAD // LANES), axis=1) * acc_ref[rows, :] + pvi
            m_ref[rows, :] = mi

    def query_tile(sup, c):
        m_ref[...] = jnp.full(m_ref.shape, -jnp.inf, F32)
        acc_ref[...] = jnp.zeros(acc_ref.shape, F32)
        for b in range(n_sub):
            step(sup, b * MLA_TK, sup * n_sub + b, True)

        def body(j, carry):
            step(sup, 0, j, False)
            return carry

        lax.fori_loop(0, sup * n_sub, body, 0)
        acc = acc_ref[...]
        o_ref[pl.ds(pl.multiple_of(sup * MLA_TQ, MLA_TQ), MLA_TQ), :] = (
            acc[:, :V_DIM] / acc[:, V_DIM:]).astype(o_ref.dtype)
        return c

    lax.fori_loop(0, q_ref.shape[0] // MLA_TQ, query_tile, 0)


def _mla_attn(q, k, v, batch, seq):
    return pl.pallas_call(
        _mla_attn_kernel,
        out_shape=jax.ShapeDtypeStruct((batch * seq, MIX_A), BF16),
        grid=(batch, N_HEADS_MLA),
        in_specs=[pl.BlockSpec((seq, QK_PAD), lambda b, h: (b, h)),
                  pl.BlockSpec((seq, QK_PAD), lambda b, h: (b, h)),
                  pl.BlockSpec((seq, V_PAD), lambda b, h: (b, h))],
        out_specs=pl.BlockSpec((seq, V_DIM), lambda b, h: (b, h)),
        scratch_shapes=[pltpu.VMEM((MLA_TQ, LANES), F32), pltpu.VMEM((MLA_TQ, V_PAD), F32)],
        compiler_params=_cparams(("parallel", "parallel")),
        name="mla_attn",
    )(q, k, v)


DIL_GROUP = 8


def _dilated_kernel(q_ref, k_ref, v_ref, bias_ref, o_ref, nat_m, nat_l, nat_acc, q4_ref, k4_ref, v4_ref,
                    m_ref, l_ref, acc_ref, *, seq):
    quarter = seq // 4

    def by4(start, n):
        return pl.ds(start, n, stride=4)

    for src, tmp, dst in ((q_ref, nat_m, q4_ref), (k_ref, nat_l, k4_ref), (v_ref, nat_acc, v4_ref)):
        tmp[...] = src[...].astype(F32)
        for r4 in range(4):
            dst[r4 * quarter:(r4 + 1) * quarter, :] = tmp[by4(r4, quarter), :]

    def attend(blocks):
        s = [lax.dot_general(q.astype(BF16), k.astype(BF16), (((1,), (1,)), ((), ())),
                             preferred_element_type=F32) + bias for q, k, _, bias in blocks]
        m_p = [jnp.broadcast_to(jnp.max(si, axis=-1, keepdims=True), (BLOCK, LANES)) for si in s]
        p = [jnp.exp2(si - jnp.concatenate([mi, mi], axis=1)).astype(BF16) for si, mi in zip(s, m_p)]
        ones = jnp.ones((2 * BLOCK, LANES), BF16)
        num_l = [jnp.dot(pi, jnp.concatenate([v.astype(BF16), ones], axis=1), preferred_element_type=F32)
                 for pi, (_, _, v, _) in zip(p, blocks)]
        return [(mi, nl[:, HEAD_DIM:], nl[:, :HEAD_DIM]) for mi, nl in zip(m_p, num_l)]

    def merge(old, new):
        m_old, l_old, acc_old = old
        m_p, l_p, num_p = new
        m_new = jnp.maximum(m_old, m_p)
        a = jnp.exp2(m_old - m_new)
        b = jnp.exp2(m_p - m_new)
        return m_new, l_old * a + l_p * b, acc_old * a + num_p * b

    def bias_of(pat, n):
        return bias_ref[pat, 0, jnp.where(n == 0, 1, 0)]

    def prev_block(n):
        return jnp.maximum(n - 1, 0)

    def pass_natural(g, carry):
        where, blocks = [], []
        for u in range(DIL_GROUP):
            n = g * DIL_GROUP + u
            q_at = pl.ds(pl.multiple_of(n * BLOCK, BLOCK), BLOCK)
            k_at = pl.ds(pl.multiple_of(prev_block(n) * BLOCK, BLOCK), 2 * BLOCK)
            where.append(q_at)
            blocks.append((q_ref[q_at, :], k_ref[k_at, :], v_ref[k_at, :], bias_of(0, n)))
        for q_at, (m_p, l_p, num_p) in zip(where, attend(blocks)):
            nat_m[q_at, :] = m_p
            nat_l[q_at, :] = l_p
            nat_acc[q_at, :] = num_p
        return carry

    def pass_dil4(g, carry):
        where, blocks, old = [], [], []
        for u in range(DIL_GROUP):
            r4 = u % 4
            n = g * (DIL_GROUP // 4) + u // 4
            q_at = pl.ds(pl.multiple_of(r4 * quarter + n * BLOCK, BLOCK), BLOCK)
            k_at = pl.ds(pl.multiple_of(r4 * quarter + prev_block(n) * BLOCK, BLOCK), 2 * BLOCK)
            nat_at = by4(r4 + n * (4 * BLOCK), BLOCK)
            where.append(q_at)
            blocks.append((q4_ref[q_at, :], k4_ref[k_at, :], v4_ref[k_at, :], bias_of(1, n)))
            old.append((nat_m[nat_at, :], nat_l[nat_at, :], nat_acc[nat_at, :]))
        for q_at, o, new in zip(where, old, attend(blocks)):
            m_ref[q_at, :], l_ref[q_at, :], acc_ref[q_at, :] = merge(o, new)
        return carry

    def pass_dil16(g, carry):
        where, blocks, old = [], [], []
        for u in range(DIL_GROUP):
            unit = g * DIL_GROUP + u
            r = lax.rem(unit, 16)
            n = lax.div(unit, 16)
            row0 = lax.rem(r, 4) * quarter + lax.div(r, 4)
            q_at = by4(row0 + n * (4 * BLOCK), BLOCK)
            k_at = by4(row0 + prev_block(n) * (4 * BLOCK), 2 * BLOCK)
            where.append(q_at)
            blocks.append((q4_ref[q_at, :], k4_ref[k_at, :], v4_ref[k_at, :], bias_of(2, n)))
            old.append((m_ref[q_at, :], l_ref[q_at, :], acc_ref[q_at, :]))
        for q_at, o, new in zip(where, old, attend(blocks)):
            m_ref[q_at, :], l_ref[q_at, :], acc_ref[q_at, :] = merge(o, new)
        return carry

    n_units = seq // BLOCK
    lax.fori_loop(0, n_units // DIL_GROUP, pass_natural, 0)
    lax.fori_loop(0, n_units // DIL_GROUP, pass_dil4, 0)
    lax.fori_loop(0, n_units // DIL_GROUP, pass_dil16, 0)

    for r4 in range(4):
        rows = slice(r4 * quarter, (r4 + 1) * quarter)
        nat_acc[by4(r4, quarter), :] = acc_ref[rows, :] / l_ref[rows, :]
    o_ref[...] = nat_acc[...].astype(o_ref.dtype)


def _dilated_attn(proj, bias, batch, seq):
    n_pat = len(DIL_PATTERNS)
    assert all(seq // dil // BLOCK >= 2 for _, dil in DIL_PATTERNS), "every residue class needs two key blocks"
    assert tuple(dil for _, dil in DIL_PATTERNS) == (1, 4, 16) and DIL_GROUP % 4 == 0
    col = lambda off: (lambda b, h: (b, off // HEAD_DIM + h))
    return pl.pallas_call(
        functools.partial(_dilated_kernel, seq=seq),
        out_shape=jax.ShapeDtypeStruct((batch * seq, MIX_B), BF16),
        grid=(batch, N_HEADS_DIL),
        in_specs=[pl.BlockSpec((seq, HEAD_DIM), col(E_QB)),
                  pl.BlockSpec((seq, HEAD_DIM), col(E_KB)),
                  pl.BlockSpec((seq, HEAD_DIM), col(E_VB)),
                  pl.BlockSpec((n_pat, 1, 2, BLOCK, 2 * BLOCK), lambda b, h: (0, h, 0, 0, 0))],
        out_specs=pl.BlockSpec((seq, HEAD_DIM), lambda b, h: (b, h)),
        scratch_shapes=[pltpu.VMEM((seq, HEAD_DIM), F32)] * 9,
        compiler_params=_cparams(("parallel", "parallel")),
        name="dilated_attn",
    )(proj, proj, proj, bias)


def _gate_out_kernel(*refs, n_mixers, final_norm):
    mixer_refs = refs[:n_mixers]
    gate_ref, x_ref, w_ref, gf_ref, o_ref = refs[n_mixers:]
    g = gate_ref[...].astype(F32)
    sg = g * jax.nn.sigmoid(g)
    y = x_ref[...]
    lo = 0
    for m_ref in mixer_refs:
        hi = lo + m_ref.shape[1]
        mix = (m_ref[...].astype(F32) * sg[:, lo:hi]).astype(BF16)
        y = y + jnp.dot(mix, w_ref[lo:hi, :], preferred_element_type=F32)
        lo = hi
    if final_norm:
        ms = jnp.mean(y * y, axis=-1, keepdims=True)
        y = y * lax.rsqrt(ms + EPS) * gf_ref[...]
    o_ref[...] = y


def _gate_out_proj(mixers, proj, gate_blk, x2d, w, gf, final_norm, tm):
    t, d = x2d.shape
    mix = w.shape[0]
    assert sum(m.shape[1] for m in mixers) == mix
    return pl.pallas_call(
        functools.partial(_gate_out_kernel, n_mixers=len(mixers), final_norm=final_norm),
        out_shape=jax.ShapeDtypeStruct((t, d), F32),
        grid=(t // tm,),
        in_specs=[pl.BlockSpec((tm, m.shape[1]), lambda i: (i, 0)) for m in mixers] + [
            pl.BlockSpec((tm, mix), lambda i: (i, gate_blk)),
            pl.BlockSpec((tm, d), lambda i: (i, 0)),
            pl.BlockSpec(w.shape, lambda i: (0, 0)),
            pl.BlockSpec((1, d), lambda i: (0, 0))],
        out_specs=pl.BlockSpec((tm, d), lambda i: (i, 0)),
        compiler_params=_cparams(("parallel",)),
        name="gate_out_proj",
    )(*mixers, proj, x2d, w, gf)


SB_TK = 256
SB_CHAINS = 4
SB_TQ = SB_CHAINS * SB_TK
SB_DEAD = -160.0


def _sb_attn_kernel(q_ref, k_ref, v_ref, u_ref, o_ref, acc_ref, carry_ref):
    def step(base, tiles):
        n = len(tiles)
        rows = [slice(c * SB_TK, (c + 1) * SB_TK) for c, _, _ in tiles]
        kv_at = [pl.ds(pl.multiple_of(kb * SB_TK, SB_TK), SB_TK) for _, kb, _ in tiles]
        nz = []
        for i, (c, _, _) in enumerate(tiles):
            q = q_ref[pl.ds(pl.multiple_of((base + c) * SB_TK, SB_TK), SB_TK), :]
            nz.append(lax.dot_general(q, k_ref[kv_at[i], :], (((1,), (1,)), ((), ())),
                                      preferred_element_type=F32))
        row = lax.broadcasted_iota(jnp.int32, (SB_TK, SB_TK), 0)
        col = lax.broadcasted_iota(jnp.int32, (SB_TK, SB_TK), 1)
        before = col < row
        hi_lo = []
        for i, (_, _, masked) in enumerate(tiles):
            neg_abs = lax.bitcast_convert_type(
                lax.bitcast_convert_type(nz[i], jnp.uint32) | jnp.uint32(SIGN_BIT), F32)
            log_rest = jnp.minimum(nz[i], 0.0) - jnp.log(1.0 + jnp.exp2(neg_abs)) * LOG2E
            if masked:
                log_rest = jnp.where(before, log_rest, 0.0)
            hi = log_rest.astype(BF16)
            lo = (log_rest - hi.astype(F32)).astype(BF16)
            hi_lo.append(jnp.concatenate([hi, lo], axis=1))
        cum_all = jnp.dot(jnp.concatenate(hi_lo, axis=0), u_ref[...], preferred_element_type=F32)
        weights, carries = [], []
        for i, (_, _, masked) in enumerate(tiles):
            cum = cum_all[i * SB_TK:(i + 1) * SB_TK]
            carry = carry_ref[rows[i], :]
            a = jnp.exp2(cum + carry - nz[i])
            if masked:
                a = jnp.where(before, a, 0.0)
            weights.append(a.astype(BF16))
            carries.append(carry + cum[:, 0:1])
        accs = [acc_ref[rows[i], :] + jnp.dot(weights[i], v_ref[kv_at[i], :], preferred_element_type=F32)
                for i in range(n)]
        for i in range(n):
            acc_ref[rows[i], :] = accs[i]
            carry_ref[rows[i], :] = carries[i]

    def alive(first_chain=0):
        return jnp.max(carry_ref[first_chain * SB_TK:, :]) > SB_DEAD

    def group(g, carry_unused):
        base = g * SB_CHAINS
        acc_ref[...] = jnp.zeros(acc_ref.shape, F32)
        carry_ref[...] = jnp.zeros(carry_ref.shape, F32)
        step(base, [(c, base + c, True) for c in range(SB_CHAINS)])

        def cond(state):
            t, go = state
            return jnp.logical_and(t <= base, go)

        def body(state):
            t, _ = state
            step(base, [(c, base + c - t, False) for c in range(SB_CHAINS)])
            return t + 1, alive()

        lax.while_loop(cond, body, (jnp.int32(1), alive()))
        for j in range(1, SB_CHAINS):
            @pl.when(alive(first_chain=j))
            def _():
                step(base, [(c, c - j, False) for c in range(j, SB_CHAINS)])

        o_ref[pl.ds(pl.multiple_of(base * SB_TK, SB_TQ), SB_TQ), :] = acc_ref[...].astype(o_ref.dtype)
        return carry_unused

    lax.fori_loop(0, q_ref.shape[0] // SB_TQ, group, 0)


def _sb_attn(proj, batch, seq):
    row = np.arange(SB_TK)[:, None]
    colv = np.arange(SB_TK)[None, :]
    tri = (row >= colv).astype(np.float32)
    u = jnp.asarray(np.concatenate([tri, tri], axis=0), BF16)
    hcol = lambda off: (lambda b, h: (b, off // HEAD_DIM + h))
    return pl.pallas_call(
        _sb_attn_kernel,
        out_shape=jax.ShapeDtypeStruct((batch * seq, MIX_ODD), BF16),
        grid=(batch, N_HEADS_SB),
        in_specs=[pl.BlockSpec((seq, HEAD_DIM), hcol(0)),
                  pl.BlockSpec((seq, HEAD_DIM), hcol(MIX_ODD)),
                  pl.BlockSpec((seq, HEAD_DIM), hcol(2 * MIX_ODD)),
                  pl.BlockSpec((2 * SB_TK, SB_TK), lambda b, h: (0, 0))],
        out_specs=pl.BlockSpec((seq, HEAD_DIM), lambda b, h: (b, h)),
        scratch_shapes=[pltpu.VMEM((SB_TQ, HEAD_DIM), F32), pltpu.VMEM((SB_TQ, 1), F32)],
        compiler_params=_cparams(("parallel", "parallel")),
        name="sb_attn",
    )(proj, proj, proj, u)


def _rotate_half_partner(w):
    half = QK_ROPE // 2
    return jnp.concatenate([-w[..., half:], w[..., :half]], axis=-1)


def _even_in_weight(w_in):
    w_in = w_in.astype(BF16)
    c_q, c_kv, k_rope, q_b, k_b, v_b, gate = jnp.split(
        w_in, np.cumsum((Q_LORA, KV_LORA, QK_ROPE, MIX_B, MIX_B, MIX_B)).tolist(), axis=-1)
    pad = jnp.zeros((w_in.shape[0], E_WIDTH - E_END), w_in.dtype)
    return jnp.concatenate([gate, q_b, k_b, v_b, c_q, c_kv, k_rope, _rotate_half_partner(k_rope), pad], axis=-1)


def _uq_weight(w_uq):
    w = w_uq.astype(BF16).reshape(Q_LORA, N_HEADS_MLA, QK_NOPE + QK_ROPE)
    rot = w[..., QK_NOPE:]
    w = jnp.concatenate([w[..., :QK_NOPE], rot, _rotate_half_partner(rot)], axis=-1)
    return w.reshape(Q_LORA, N_HEADS_MLA * QK_PAD)


def _col_scale(width, start, stop, scale):
    col = np.ones((1, width), np.float32)
    col[:, start:stop] = scale
    return jnp.asarray(col)


def _layer(stacked, j):
    return stacked.reshape(stacked.shape[1:]) if stacked.shape[0] == 1 else stacked[j]


def _cos_sin_table(seq):
    half = QK_ROPE // 2
    inv = 1.0 / (ROPE_THETA ** (jnp.arange(half, dtype=F32) / half))
    ang = jnp.arange(seq).astype(F32)[:, None] * inv[None, :]
    cos, sin = jnp.cos(ang), jnp.sin(ang)
    return jnp.concatenate([cos, cos, sin, sin], axis=-1)


def kernel(x, norm_gain, w_in_even, q_norm_gain, kv_norm_gain, w_uq, w_ukv, w_out_even, rel_bias,
           w_in_odd, w_out_odd, final_norm_gain):
    batch, seq, d_model = x.shape
    t = batch * seq
    x2d = x.reshape(t, d_model)
    head_scale = HEAD_DIM ** -0.5

    proj0 = _rms_proj(x2d, norm_gain[0:1], _even_in_weight(_layer(w_in_even, 0)),
                      _col_scale(E_WIDTH, E_QB, E_KB, head_scale * LOG2E), tm=PROJ_TM, tn=PROJ_TN)
    q_a, k_a, v_a = _mla_prep(proj0, _cos_sin_table(seq), q_norm_gain[0:1], kv_norm_gain[0:1],
                              _uq_weight(_layer(w_uq, 0)), _layer(w_ukv, 0).astype(BF16), seq, tm=PREP_TM)
    o_a = _mla_attn(q_a, k_a, v_a, batch, seq)
    o_b = _dilated_attn(proj0, _bias_tiles(rel_bias), batch, seq)
    x1 = _gate_out_proj([o_a, o_b], proj0, E_GATE // MIX_EVEN, x2d, _layer(w_out_even, 0).astype(BF16),
                        final_norm_gain[None, :], final_norm=False, tm=OUT_TM)

    proj1 = _rms_proj(x1, norm_gain[1:2], _layer(w_in_odd, 0).astype(BF16),
                      _col_scale(4 * MIX_ODD, 0, MIX_ODD, -head_scale * LOG2E), tm=PROJ_TM, tn=PROJ_TN_ODD)
    o_c = _sb_attn(proj1, batch, seq)
    out = _gate_out_proj([o_c], proj1, 3, x1, _layer(w_out_odd, 0).astype(BF16),
                         final_norm_gain[None, :], final_norm=True, tm=OUT_TM)
    return out.reshape(batch, seq, d_model)
```

```python
import functools
import math

import numpy as np
import jax
import jax.numpy as jnp
from jax import lax
from jax.experimental import pallas as pl
from jax.experimental.pallas import tpu as pltpu

F32 = jnp.float32
BF16 = jnp.bfloat16

EPS = 1e-6
N_HEADS_MLA = 8
Q_LORA = 512
KV_LORA = 512
QK_NOPE = 128
QK_ROPE = 64
V_DIM = 128
ROPE_THETA = 10000.0
N_HEADS_DIL = 8
HEAD_DIM = 128
DIL_PATTERNS = ((128, 1), (512, 4), (2048, 16))
N_BUCKETS = 32
BUCKET_MAX_DIST = 2048
N_HEADS_SB = 16
BLOCK = 128
MASK_VALUE = -1e30
LOG2E = 1.4426950408889634
SIGN_BIT = 0x80000000

LANES = 128
VMEM_LIMIT = 48 * 1024 * 1024

MIX_A = N_HEADS_MLA * V_DIM
MIX_B = N_HEADS_DIL * HEAD_DIM
MIX_EVEN = MIX_A + MIX_B
MIX_ODD = N_HEADS_SB * HEAD_DIM
E_GATE = 0
E_QB = E_GATE + MIX_EVEN
E_KB = E_QB + MIX_B
E_VB = E_KB + MIX_B
E_CQ = E_VB + MIX_B
E_CKV = E_CQ + Q_LORA
E_KR = E_CKV + KV_LORA
E_END = E_KR + 2 * QK_ROPE
PROJ_TM = 1024
PROJ_TN = 1280
PROJ_TN_ODD = 1024
PREP_TM = 512
OUT_TM = 512
E_WIDTH = -(-E_END // PROJ_TN) * PROJ_TN
QK_PAD = 2 * LANES
V_PAD = 2 * LANES


def _cparams(sem):
    return pltpu.CompilerParams(dimension_semantics=sem, vmem_limit_bytes=VMEM_LIMIT)


def _bucket_tables():
    qi = np.arange(BLOCK)[:, None]
    kj = np.arange(2 * BLOCK)[None, :]
    rel = BLOCK + qi - kj
    max_exact = N_BUCKETS // 2
    tabs = []
    for window, dil in DIL_PATTERNS:
        span = window // dil
        dist = np.maximum(rel, 0) * dil
        d = np.maximum(dist.astype(np.float64), 1.0)
        frac = np.log(d / max_exact) / math.log(BUCKET_MAX_DIST / max_exact) * (N_BUCKETS - max_exact)
        large = np.minimum(max_exact + np.trunc(frac).astype(np.int64), N_BUCKETS - 1)
        bucket = np.where(dist < max_exact, dist, large)
        valid = (rel >= 0) & (rel <= span)
        tabs.append(np.where(valid, bucket, -1).astype(np.int32))
    return np.stack(tabs)


def _bias_tiles_kernel(bucket_ref, rel_bias_ref, o_ref):
    h = pl.program_id(1)
    bucket = bucket_ref[0]
    out = jnp.full(bucket.shape, MASK_VALUE, F32)
    for i in range(N_BUCKETS):
        out = jnp.where(bucket == i, rel_bias_ref[i, h] * LOG2E, out)
    o_ref[0, 0, 0] = out
    o_ref[0, 0, 1, :, :BLOCK] = out[:, BLOCK:]
    o_ref[0, 0, 1, :, BLOCK:] = jnp.full((BLOCK, BLOCK), MASK_VALUE, F32)


def _bias_tiles(rel_bias):
    n_pat = len(DIL_PATTERNS)
    buckets = jnp.asarray(_bucket_tables())
    return pl.pallas_call(
        _bias_tiles_kernel,
        out_shape=jax.ShapeDtypeStruct((n_pat, N_HEADS_DIL, 2, BLOCK, 2 * BLOCK), F32),
        grid=(n_pat, N_HEADS_DIL),
        in_specs=[pl.BlockSpec((1, BLOCK, 2 * BLOCK), lambda p, h: (p, 0, 0)),
                  pl.BlockSpec(memory_space=pltpu.SMEM)],
        out_specs=pl.BlockSpec((1, 1, 2, BLOCK, 2 * BLOCK), lambda p, h: (p, h, 0, 0, 0)),
        compiler_params=_cparams(("arbitrary", "arbitrary")),
        name="bias_tiles",
    )(buckets, rel_bias)


def _rms_proj_kernel(x_ref, g_ref, w_ref, cs_ref, o_ref, h_ref):
    @pl.when(pl.program_id(1) == 0)
    def _():
        x = x_ref[...]
        ms = jnp.mean(x * x, axis=-1, keepdims=True)
        h_ref[...] = (x * lax.rsqrt(ms + EPS) * g_ref[...]).astype(BF16)

    acc = jnp.dot(h_ref[...], w_ref[...], preferred_element_type=F32)
    o_ref[...] = (acc * cs_ref[...]).astype(o_ref.dtype)


def _rms_proj(x2d, gain, w, col_scale, tm, tn):
    t, d = x2d.shape
    n = w.shape[1]
    return pl.pallas_call(
        _rms_proj_kernel,
        out_shape=jax.ShapeDtypeStruct((t, n), BF16),
        grid=(t // tm, n // tn),
        in_specs=[pl.BlockSpec((tm, d), lambda i, j: (i, 0)),
                  pl.BlockSpec((1, d), lambda i, j: (0, 0)),
                  pl.BlockSpec((d, tn), lambda i, j: (0, j)),
                  pl.BlockSpec((1, tn), lambda i, j: (0, j))],
        out_specs=pl.BlockSpec((tm, tn), lambda i, j: (i, j)),
        scratch_shapes=[pltpu.VMEM((tm, d), BF16)],
        compiler_params=_cparams(("parallel", "arbitrary")),
        name="rms_proj",
    )(x2d, gain, w, col_scale)


def _proj_kernel(h_ref, w_ref, cs_ref, o_ref):
    acc = jnp.dot(h_ref[...], w_ref[...], preferred_element_type=F32)
    o_ref[...] = (acc * cs_ref[...]).astype(o_ref.dtype)


def _proj(h, w, col_scale, tm, tn):
    t, d = h.shape
    n = w.shape[1]
    return pl.pallas_call(
        _proj_kernel,
        out_shape=jax.ShapeDtypeStruct((t, n), BF16),
        grid=(t // tm, n // tn),
        in_specs=[pl.BlockSpec((tm, d), lambda i, j: (i, 0)),
                  pl.BlockSpec((d, tn), lambda i, j: (0, j)),
                  pl.BlockSpec((1, tn), lambda i, j: (0, j))],
        out_specs=pl.BlockSpec((tm, tn), lambda i, j: (i, j)),
        compiler_params=_cparams(("parallel", "arbitrary")),
        name="proj",
    )(h, w, col_scale)


def _rope_chunk(chunk, cs):
    r = chunk * cs
    r = r + pltpu.roll(r, QK_ROPE, axis=1)
    lane = lax.broadcasted_iota(jnp.int32, r.shape, 1)
    return jnp.where(lane < QK_ROPE, r, 0.0)


def _mla_prep_kernel(cq_ref, ckv_ref, kr_ref, cs_ref, gq_ref, gkv_ref, wuq_ref, wukv_ref,
                     q_ref, k_ref, v_ref):
    def latent_norm(c_ref, g_ref):
        c = c_ref[...].astype(F32)
        ms = jnp.mean(c * c, axis=-1, keepdims=True)
        return (c * lax.rsqrt(ms + EPS) * g_ref[...]).astype(BF16)

    cs = cs_ref[...]
    scale = LOG2E * (QK_NOPE + QK_ROPE) ** -0.5
    q = jnp.dot(latent_norm(cq_ref, gq_ref), wuq_ref[...], preferred_element_type=F32)
    for h in range(N_HEADS_MLA):
        lo = h * QK_PAD
        q_ref[:, lo:lo + LANES] = (q[:, lo:lo + LANES] * scale).astype(BF16)
        q_ref[:, lo + LANES:lo + QK_PAD] = (_rope_chunk(q[:, lo + LANES:lo + QK_PAD], cs) * scale).astype(BF16)

    k_rot = _rope_chunk(kr_ref[...].astype(F32), cs).astype(BF16)
    ones_col = jnp.ones((cs.shape[0], V_PAD - V_DIM), BF16)
    kv = jnp.dot(latent_norm(ckv_ref, gkv_ref), wukv_ref[...], preferred_element_type=F32)
    for h in range(N_HEADS_MLA):
        lo = h * (QK_NOPE + V_DIM)
        k_ref[:, h * QK_PAD:h * QK_PAD + LANES] = kv[:, lo:lo + QK_NOPE].astype(BF16)
        k_ref[:, h * QK_PAD + LANES:(h + 1) * QK_PAD] = k_rot
        v_ref[:, h * V_PAD:h * V_PAD + V_DIM] = kv[:, lo + QK_NOPE:lo + QK_NOPE + V_DIM].astype(BF16)
        v_ref[:, h * V_PAD + V_DIM:(h + 1) * V_PAD] = ones_col


def _mla_prep(proj, cos_sin, gq, gkv, wuq, wukv, seq, tm):
    t = proj.shape[0]
    n_seq_tiles = seq // tm
    const = lambda i: (0, 0)
    return pl.pallas_call(
        _mla_prep_kernel,
        out_shape=(jax.ShapeDtypeStruct((t, N_HEADS_MLA * QK_PAD), BF16),
                   jax.ShapeDtypeStruct((t, N_HEADS_MLA * QK_PAD), BF16),
                   jax.ShapeDtypeStruct((t, N_HEADS_MLA * V_PAD), BF16)),
        grid=(t // tm,),
        in_specs=[pl.BlockSpec((tm, Q_LORA), lambda i: (i, E_CQ // Q_LORA)),
                  pl.BlockSpec((tm, KV_LORA), lambda i: (i, E_CKV // KV_LORA)),
                  pl.BlockSpec((tm, LANES), lambda i: (i, E_KR // LANES)),
                  pl.BlockSpec((tm, LANES), lambda i: (i % n_seq_tiles, 0)),
                  pl.BlockSpec((1, Q_LORA), const),
                  pl.BlockSpec((1, KV_LORA), const),
                  pl.BlockSpec(wuq.shape, const),
                  pl.BlockSpec(wukv.shape, const)],
        out_specs=(pl.BlockSpec((tm, N_HEADS_MLA * QK_PAD), lambda i: (i, 0)),
                   pl.BlockSpec((tm, N_HEADS_MLA * QK_PAD), lambda i: (i, 0)),
                   pl.BlockSpec((tm, N_HEADS_MLA * V_PAD), lambda i: (i, 0))),
        compiler_params=_cparams(("parallel",)),
        name="mla_prep",
    )(proj, proj, proj, cos_sin, gq, gkv, wuq, wukv)


MLA_TQ = 1024
MLA_TK = 1024
MLA_SLAB = 256


def _mla_attn_kernel(q_ref, k_ref, v_ref, o_ref, m_ref, acc_ref):
    n_sub = MLA_TQ // MLA_TK

    def step(sup, row_lo, kb, masked):
        start = pl.multiple_of(kb * MLA_TK, MLA_TK)
        k = k_ref[pl.ds(start, MLA_TK), :]
        v = v_ref[pl.ds(start, MLA_TK), :]
        slabs = [slice(r, r + MLA_SLAB) for r in range(row_lo, MLA_TQ, MLA_SLAB)]
        n_keys = [min(rows.stop - row_lo, MLA_TK) if masked else MLA_TK for rows in slabs]
        s = []
        for rows, nk in zip(slabs, n_keys):
            q = q_ref[pl.ds(pl.multiple_of(sup * MLA_TQ + rows.start, MLA_SLAB), MLA_SLAB), :]
            si = lax.dot_general(q, k[:nk], (((1,), (1,)), ((), ())), preferred_element_type=F32)
            if masked and rows.start - row_lo < MLA_TK:
                row = lax.broadcasted_iota(jnp.int32, si.shape, 0) + (rows.start - row_lo)
                col = lax.broadcasted_iota(jnp.int32, si.shape, 1)
                si = jnp.where(col <= row, si, MASK_VALUE)
            s.append(si)
        p, alpha, m_new = [], [], []
        for rows, si, nk in zip(slabs, s, n_keys):
            m_old = m_ref[rows, :]
            mi = jnp.maximum(m_old, jnp.max(si, axis=-1, keepdims=True))
            alpha.append(jnp.exp2(m_old - mi))
            p.append(jnp.exp2(si - jnp.concatenate([mi] * (nk // LANES), axis=1)).astype(BF16))
            m_new.append(mi)
        pv = [jnp.dot(pi, v[:nk], preferred_element_type=F32) for pi, nk in zip(p, n_keys)]
        for rows, ai, mi, pvi in zip(slabs, alpha, m_new, pv):
            acc_ref[rows, :] = jnp.concatenate([ai] * (V_PAD // LANES), axis=1) * acc_ref[rows, :] + pvi
            m_ref[rows, :] = mi

    def query_tile(sup, c):
        m_ref[...] = jnp.full(m_ref.shape, -jnp.inf, F32)
        acc_ref[...] = jnp.zeros(acc_ref.shape, F32)
        for b in range(n_sub):
            step(sup, b * MLA_TK, sup * n_sub + b, True)

        def body(j, carry):
            step(sup, 0, j, False)
            return carry

        lax.fori_loop(0, sup * n_sub, body, 0)
        acc = acc_ref[...]
        o_ref[pl.ds(pl.multiple_of(sup * MLA_TQ, MLA_TQ), MLA_TQ), :] = (
            acc[:, :V_DIM] / acc[:, V_DIM:]).astype(o_ref.dtype)
        return c

    lax.fori_loop(0, q_ref.shape[0] // MLA_TQ, query_tile, 0)


def _mla_attn(q, k, v, batch, seq):
    return pl.pallas_call(
        _mla_attn_kernel,
        out_shape=jax.ShapeDtypeStruct((batch * seq, MIX_A), BF16),
        grid=(batch, N_HEADS_MLA),
        in_specs=[pl.BlockSpec((seq, QK_PAD), lambda b, h: (b, h)),
                  pl.BlockSpec((seq, QK_PAD), lambda b, h: (b, h)),
                  pl.BlockSpec((seq, V_PAD), lambda b, h: (b, h))],
        out_specs=pl.BlockSpec((seq, V_DIM), lambda b, h: (b, h)),
        scratch_shapes=[pltpu.VMEM((MLA_TQ, LANES), F32), pltpu.VMEM((MLA_TQ, V_PAD), F32)],
        compiler_params=_cparams(("parallel", "parallel")),
        name="mla_attn",
    )(q, k, v)


DIL_GROUP = 8


def _dilated_kernel(q_ref, k_ref, v_ref, bias_ref, o_ref, nat_m, nat_l, nat_acc, q4_ref, k4_ref, v4_ref,
                    m_ref, l_ref, acc_ref, *, seq):
    quarter = seq // 4

    def by4(start, n):
        return pl.ds(start, n, stride=4)

    for src, tmp, dst in ((q_ref, nat_m, q4_ref), (k_ref, nat_l, k4_ref), (v_ref, nat_acc, v4_ref)):
        tmp[...] = src[...].astype(F32)
        for r4 in range(4):
            dst[r4 * quarter:(r4 + 1) * quarter, :] = tmp[by4(r4, quarter), :]

    def attend(blocks):
        s = [lax.dot_general(q.astype(BF16), k.astype(BF16), (((1,), (1,)), ((), ())),
                             preferred_element_type=F32) + bias for q, k, _, bias in blocks]
        m_p = [jnp.broadcast_to(jnp.max(si, axis=-1, keepdims=True), (BLOCK, LANES)) for si in s]
        p = [jnp.exp2(si - jnp.concatenate([mi, mi], axis=1)).astype(BF16) for si, mi in zip(s, m_p)]
        ones = jnp.ones((2 * BLOCK, LANES), BF16)
        num_l = [jnp.dot(pi, jnp.concatenate([v.astype(BF16), ones], axis=1), preferred_element_type=F32)
                 for pi, (_, _, v, _) in zip(p, blocks)]
        return [(mi, nl[:, HEAD_DIM:], nl[:, :HEAD_DIM]) for mi, nl in zip(m_p, num_l)]

    def merge(old, new):
        m_old, l_old, acc_old = old
        m_p, l_p, num_p = new
        m_new = jnp.maximum(m_old, m_p)
        a = jnp.exp2(m_old - m_new)
        b = jnp.exp2(m_p - m_new)
        return m_new, l_old * a + l_p * b, acc_old * a + num_p * b

    def bias_of(pat, n):
        return bias_ref[pat, 0, jnp.where(n == 0, 1, 0)]

    def prev_block(n):
        return jnp.maximum(n - 1, 0)

    def pass_natural(g, carry):
        where, blocks = [], []
        for u in range(DIL_GROUP):
            n = g * DIL_GROUP + u
            q_at = pl.ds(pl.multiple_of(n * BLOCK, BLOCK), BLOCK)
            k_at = pl.ds(pl.multiple_of(prev_block(n) * BLOCK, BLOCK), 2 * BLOCK)
            where.append(q_at)
            blocks.append((q_ref[q_at, :], k_ref[k_at, :], v_ref[k_at, :], bias_of(0, n)))
        for q_at, (m_p, l_p, num_p) in zip(where, attend(blocks)):
            nat_m[q_at, :] = m_p
            nat_l[q_at, :] = l_p
            nat_acc[q_at, :] = num_p
        return carry

    def pass_dil4(g, carry):
        where, blocks, old = [], [], []
        for u in range(DIL_GROUP):
            r4 = u % 4
            n = g * (DIL_GROUP // 4) + u // 4
            q_at = pl.ds(pl.multiple_of(r4 * quarter + n * BLOCK, BLOCK), BLOCK)
            k_at = pl.ds(pl.multiple_of(r4 * quarter + prev_block(n) * BLOCK, BLOCK), 2 * BLOCK)
            nat_at = by4(r4 + n * (4 * BLOCK), BLOCK)
            where.append(q_at)
            blocks.append((q4_ref[q_at, :], k4_ref[k_at, :], v4_ref[k_at, :], bias_of(1, n)))
            old.append((nat_m[nat_at, :], nat_l[nat_at, :], nat_acc[nat_at, :]))
        for q_at, o, new in zip(where, old, attend(blocks)):
            m_ref[q_at, :], l_ref[q_at, :], acc_ref[q_at, :] = merge(o, new)
        return carry

    def pass_dil16(g, carry):
        where, blocks, old = [], [], []
        for u in range(DIL_GROUP):
            unit = g * DIL_GROUP + u
            r = lax.rem(unit, 16)
            n = lax.div(unit, 16)
            row0 = lax.rem(r, 4) * quarter + lax.div(r, 4)
            q_at = by4(row0 + n * (4 * BLOCK), BLOCK)
            k_at = by4(row0 + prev_block(n) * (4 * BLOCK), 2 * BLOCK)
            where.append(q_at)
            blocks.append((q4_ref[q_at, :], k4_ref[k_at, :], v4_ref[k_at, :], bias_of(2, n)))
            old.append((m_ref[q_at, :], l_ref[q_at, :], acc_ref[q_at, :]))
        for q_at, o, new in zip(where, old, attend(blocks)):
            m_ref[q_at, :], l_ref[q_at, :], acc_ref[q_at, :] = merge(o, new)
        return carry

    n_units = seq // BLOCK
    lax.fori_loop(0, n_units // DIL_GROUP, pass_natural, 0)
    lax.fori_loop(0, n_units // DIL_GROUP, pass_dil4, 0)
    lax.fori_loop(0, n_units // DIL_GROUP, pass_dil16, 0)

    for r4 in range(4):
        rows = slice(r4 * quarter, (r4 + 1) * quarter)
        nat_acc[by4(r4, quarter), :] = acc_ref[rows, :] / l_ref[rows, :]
    o_ref[...] = nat_acc[...].astype(o_ref.dtype)


def _dilated_attn(proj, bias, batch, seq):
    n_pat = len(DIL_PATTERNS)
    assert all(seq // dil // BLOCK >= 2 for _, dil in DIL_PATTERNS), "every residue class needs two key blocks"
    assert tuple(dil for _, dil in DIL_PATTERNS) == (1, 4, 16) and DIL_GROUP % 4 == 0
    col = lambda off: (lambda b, h: (b, off // HEAD_DIM + h))
    return pl.pallas_call(
        functools.partial(_dilated_kernel, seq=seq),
        out_shape=jax.ShapeDtypeStruct((batch * seq, MIX_B), BF16),
        grid=(batch, N_HEADS_DIL),
        in_specs=[pl.BlockSpec((seq, HEAD_DIM), col(E_QB)),
                  pl.BlockSpec((seq, HEAD_DIM), col(E_KB)),
                  pl.BlockSpec((seq, HEAD_DIM), col(E_VB)),
                  pl.BlockSpec((n_pat, 1, 2, BLOCK, 2 * BLOCK), lambda b, h: (0, h, 0, 0, 0))],
        out_specs=pl.BlockSpec((seq, HEAD_DIM), lambda b, h: (b, h)),
        scratch_shapes=[pltpu.VMEM((seq, HEAD_DIM), F32)] * 9,
        compiler_params=_cparams(("parallel", "parallel")),
        name="dilated_attn",
    )(proj, proj, proj, bias)


def _gate_out_kernel(*refs, n_mixers, final_norm):
    mixer_refs = refs[:n_mixers]
    gate_ref, x_ref, w_ref, gf_ref, o_ref = refs[n_mixers:n_mixers + 5]
    g = gate_ref[...].astype(F32)
    sg = g * jax.nn.sigmoid(g)
    y = x_ref[...]
    lo = 0
    for m_ref in mixer_refs:
        hi = lo + m_ref.shape[1]
        mix = (m_ref[...].astype(F32) * sg[:, lo:hi]).astype(BF16)
        y = y + jnp.dot(mix, w_ref[lo:hi, :], preferred_element_type=F32)
        lo = hi
    ms = jnp.mean(y * y, axis=-1, keepdims=True)
    normed = y * lax.rsqrt(ms + EPS) * gf_ref[...]
    if final_norm:
        o_ref[...] = normed
    else:
        o_ref[...] = y
        refs[n_mixers + 5][...] = normed.astype(BF16)


def _gate_out_proj(mixers, proj, gate_blk, x2d, w, gf, final_norm, tm):
    t, d = x2d.shape
    mix = w.shape[0]
    assert sum(m.shape[1] for m in mixers) == mix
    row_block = pl.BlockSpec((tm, d), lambda i: (i, 0))
    if final_norm:
        out_shape, out_specs = jax.ShapeDtypeStruct((t, d), F32), row_block
    else:
        out_shape = (jax.ShapeDtypeStruct((t, d), F32), jax.ShapeDtypeStruct((t, d), BF16))
        out_specs = (row_block, row_block)
    return pl.pallas_call(
        functools.partial(_gate_out_kernel, n_mixers=len(mixers), final_norm=final_norm),
        out_shape=out_shape,
        grid=(t // tm,),
        in_specs=[pl.BlockSpec((tm, m.shape[1]), lambda i: (i, 0)) for m in mixers] + [
            pl.BlockSpec((tm, mix), lambda i: (i, gate_blk)),
            pl.BlockSpec((tm, d), lambda i: (i, 0)),
            pl.BlockSpec(w.shape, lambda i: (0, 0)),
            pl.BlockSpec((1, d), lambda i: (0, 0))],
        out_specs=out_specs,
        compiler_params=_cparams(("parallel",)),
        name="gate_out_proj",
    )(*mixers, proj, x2d, w, gf)


SB_TK = 256
SB_CHAINS = 4
SB_TQ = SB_CHAINS * SB_TK
SB_DEAD = -160.0


def _sb_attn_kernel(q_ref, k_ref, v_ref, u_ref, o_ref, acc_ref, carry_ref):
    def step(base, tiles):
        n = len(tiles)
        rows = [slice(c * SB_TK, (c + 1) * SB_TK) for c, _, _ in tiles]
        kv_at = [pl.ds(pl.multiple_of(kb * SB_TK, SB_TK), SB_TK) for _, kb, _ in tiles]
        nz = []
        for i, (c, _, _) in enumerate(tiles):
            q = q_ref[pl.ds(pl.multiple_of((base + c) * SB_TK, SB_TK), SB_TK), :]
            nz.append(lax.dot_general(q, k_ref[kv_at[i], :], (((1,), (1,)), ((), ())),
                                      preferred_element_type=F32))
        row = lax.broadcasted_iota(jnp.int32, (SB_TK, SB_TK), 0)
        col = lax.broadcasted_iota(jnp.int32, (SB_TK, SB_TK), 1)
        before = col < row
        hi_lo = []
        for i, (_, _, masked) in enumerate(tiles):
            neg_abs = lax.bitcast_convert_type(
                lax.bitcast_convert_type(nz[i], jnp.uint32) | jnp.uint32(SIGN_BIT), F32)
            log_rest = jnp.minimum(nz[i], 0.0) - jnp.log(1.0 + jnp.exp2(neg_abs)) * LOG2E
            if masked:
                log_rest = jnp.where(before, log_rest, 0.0)
            hi = log_rest.astype(BF16)
            lo = (log_rest - hi.astype(F32)).astype(BF16)
            hi_lo.append(jnp.concatenate([hi, lo], axis=1))
        cum_all = jnp.dot(jnp.concatenate(hi_lo, axis=0), u_ref[...], preferred_element_type=F32)
        weights, carries = [], []
        for i, (_, _, masked) in enumerate(tiles):
            cum = cum_all[i * SB_TK:(i + 1) * SB_TK]
            carry = carry_ref[rows[i], :]
            a = jnp.exp2(cum + carry - nz[i])
            if masked:
                a = jnp.where(before, a, 0.0)
            weights.append(a.astype(BF16))
            carries.append(carry + cum[:, 0:1])
        accs = [acc_ref[rows[i], :] + jnp.dot(weights[i], v_ref[kv_at[i], :], preferred_element_type=F32)
                for i in range(n)]
        for i in range(n):
            acc_ref[rows[i], :] = accs[i]
            carry_ref[rows[i], :] = carries[i]

    def alive(first_chain=0):
        return jnp.max(carry_ref[first_chain * SB_TK:, :]) > SB_DEAD

    def group(g, carry_unused):
        base = g * SB_CHAINS
        acc_ref[...] = jnp.zeros(acc_ref.shape, F32)
        carry_ref[...] = jnp.zeros(carry_ref.shape, F32)
        step(base, [(c, base + c, True) for c in range(SB_CHAINS)])

        def cond(state):
            t, go = state
            return jnp.logical_and(t <= base, go)

        def body(state):
            t, _ = state
            step(base, [(c, base + c - t, False) for c in range(SB_CHAINS)])
            return t + 1, alive()

        lax.while_loop(cond, body, (jnp.int32(1), alive()))
        for j in range(1, SB_CHAINS):
            @pl.when(alive(first_chain=j))
            def _():
                step(base, [(c, c - j, False) for c in range(j, SB_CHAINS)])

        o_ref[pl.ds(pl.multiple_of(base * SB_TK, SB_TQ), SB_TQ), :] = acc_ref[...].astype(o_ref.dtype)
        return carry_unused

    lax.fori_loop(0, q_ref.shape[0] // SB_TQ, group, 0)


def _sb_attn(proj, batch, seq):
    row = np.arange(SB_TK)[:, None]
    colv = np.arange(SB_TK)[None, :]
    tri = (row >= colv).astype(np.float32)
    u = jnp.asarray(np.concatenate([tri, tri], axis=0), BF16)
    hcol = lambda off: (lambda b, h: (b, off // HEAD_DIM + h))
    return pl.pallas_call(
        _sb_attn_kernel,
        out_shape=jax.ShapeDtypeStruct((batch * seq, MIX_ODD), BF16),
        grid=(batch, N_HEADS_SB),
        in_specs=[pl.BlockSpec((seq, HEAD_DIM), hcol(0)),
                  pl.BlockSpec((seq, HEAD_DIM), hcol(MIX_ODD)),
                  pl.BlockSpec((seq, HEAD_DIM), hcol(2 * MIX_ODD)),
                  pl.BlockSpec((2 * SB_TK, SB_TK), lambda b, h: (0, 0))],
        out_specs=pl.BlockSpec((seq, HEAD_DIM), lambda b, h: (b, h)),
        scratch_shapes=[pltpu.VMEM((SB_TQ, HEAD_DIM), F32), pltpu.VMEM((SB_TQ, 1), F32)],
        compiler_params=_cparams(("parallel", "parallel")),
        name="sb_attn",
    )(proj, proj, proj, u)


def _rotate_half_partner(w):
    half = QK_ROPE // 2
    return jnp.concatenate([-w[..., half:], w[..., :half]], axis=-1)


def _even_in_weight(w_in):
    w_in = w_in.astype(BF16)
    c_q, c_kv, k_rope, q_b, k_b, v_b, gate = jnp.split(
        w_in, np.cumsum((Q_LORA, KV_LORA, QK_ROPE, MIX_B, MIX_B, MIX_B)).tolist(), axis=-1)
    pad = jnp.zeros((w_in.shape[0], E_WIDTH - E_END), w_in.dtype)
    return jnp.concatenate([gate, q_b, k_b, v_b, c_q, c_kv, k_rope, _rotate_half_partner(k_rope), pad], axis=-1)


def _uq_weight(w_uq):
    w = w_uq.astype(BF16).reshape(Q_LORA, N_HEADS_MLA, QK_NOPE + QK_ROPE)
    rot = w[..., QK_NOPE:]
    w = jnp.concatenate([w[..., :QK_NOPE], rot, _rotate_half_partner(rot)], axis=-1)
    return w.reshape(Q_LORA, N_HEADS_MLA * QK_PAD)


def _col_scale(width, start, stop, scale):
    col = np.ones((1, width), np.float32)
    col[:, start:stop] = scale
    return jnp.asarray(col)


def _cos_sin_table(seq):
    half = QK_ROPE // 2
    inv = 1.0 / (ROPE_THETA ** (jnp.arange(half, dtype=F32) / half))
    ang = jnp.arange(seq).astype(F32)[:, None] * inv[None, :]
    cos, sin = jnp.cos(ang), jnp.sin(ang)
    return jnp.concatenate([cos, cos, sin, sin], axis=-1)


def kernel(x, norm_gain, w_in_even, q_norm_gain, kv_norm_gain, w_uq, w_ukv, w_out_even, rel_bias,
           w_in_odd, w_out_odd, final_norm_gain):
    batch, seq, d_model = x.shape
    t = batch * seq
    x2d = x.reshape(t, d_model)
    head_scale = HEAD_DIM ** -0.5

    proj0 = _rms_proj(x2d, norm_gain[0:1], _even_in_weight(w_in_even[0]),
                      _col_scale(E_WIDTH, E_QB, E_KB, head_scale * LOG2E), tm=PROJ_TM, tn=PROJ_TN)
    q_a, k_a, v_a = _mla_prep(proj0, _cos_sin_table(seq), q_norm_gain[0:1], kv_norm_gain[0:1],
                              _uq_weight(w_uq[0]), w_ukv[0].astype(BF16), seq, tm=PREP_TM)
    o_a = _mla_attn(q_a, k_a, v_a, batch, seq)
    o_b = _dilated_attn(proj0, _bias_tiles(rel_bias), batch, seq)
    x1, h1 = _gate_out_proj([o_a, o_b], proj0, E_GATE // MIX_EVEN, x2d, w_out_even[0].astype(BF16),
                            norm_gain[1:2], final_norm=False, tm=OUT_TM)

    proj1 = _proj(h1, w_in_odd[0].astype(BF16),
                  _col_scale(4 * MIX_ODD, 0, MIX_ODD, -head_scale * LOG2E), tm=PROJ_TM, tn=PROJ_TN_ODD)
    o_c = _sb_attn(proj1, batch, seq)
    out = _gate_out_proj([o_c], proj1, 3, x1, w_out_odd[0].astype(BF16),
                         final_norm_gain[None, :], final_norm=True, tm=OUT_TM)
    return out.reshape(batch, seq, d_model)
```
